```python
import math
import jax, jax.numpy as jnp
from jax import lax
import numpy as np

D_MODEL = 2048
BATCH = 2
SEQ = 4096
DEPTH = 1
DEC_BATCH = 128
DEC_SEQ = 8
PAST_LEN = 2048
PAGE_SIZE = 128

CHUNK = 128
A_GROUPS = 8
A_WIDTH = D_MODEL // 2
A_GROUP_DIM = A_WIDTH // A_GROUPS
N_HEADS = 16
HEAD_DIM = D_MODEL // N_HEADS
N_KV = 4
GQA = N_HEADS // N_KV
KV_WIDTH = N_KV * HEAD_DIM
CMP_BLOCK = 64
SEL_BLOCK = 64
N_SEL = 16
WINDOW = 512
CMP_HIDDEN = 256
Q_BLOCK = 128
SEL_TOK_BLOCK = 64
ATTN_SCALE = HEAD_DIM ** -0.5
N_BUCKETS = 32
MAX_DISTANCE = 128
N_KEYS = 128
N_EXPERTS = N_KEYS * N_KEYS
PEER_HEADS = 8
PEER_KEY_DIM = 256
PEER_TOPK = 16
PEER_TOK_BLOCK = 128
D_IN = 2 * A_WIDTH + N_HEADS * HEAD_DIM + 6 * KV_WIDTH + 3 * N_HEADS + 2 * D_MODEL
NEG = -1e30
FORCE_BONUS = 1e4
EPS = 1e-6

kernel_name = "hybrid_gmlp_nsa_peer_step"


def _rms(x, g):
    xf = x.astype(jnp.float32)
    r = lax.rsqrt(jnp.mean(xf * xf, axis=-1, keepdims=True) + EPS)
    return (xf * r).astype(x.dtype) * g


def _layernorm(x, g, b):
    xf = x.astype(jnp.float32)
    mu = jnp.mean(xf, axis=-1, keepdims=True)
    var = jnp.mean(jnp.square(xf - mu), axis=-1, keepdims=True)
    return ((xf - mu) * lax.rsqrt(var + EPS)).astype(x.dtype) * g + b


def _masked_softmax(s, mask, axes):
    s = jnp.where(mask, s, NEG)
    m = jnp.max(s, axis=axes, keepdims=True)
    p = jnp.where(mask, jnp.exp(s - m), 0.0)
    den = jnp.sum(p, axis=axes, keepdims=True)
    return p / jnp.maximum(den, 1e-30)


def _rel_bucket(dist):
    n = jnp.maximum(dist, 0)
    max_exact = N_BUCKETS // 2
    nf = jnp.maximum(n, 1).astype(jnp.float32)
    large = max_exact + (jnp.log(nf / max_exact) / math.log(MAX_DISTANCE / max_exact)
                         * (N_BUCKETS - max_exact)).astype(jnp.int32)
    large = jnp.minimum(large, N_BUCKETS - 1)
    return jnp.where(n < max_exact, n, large)


def _split_cols(a):
    sizes = (2 * A_WIDTH, N_HEADS * HEAD_DIM) + (KV_WIDTH,) * 6 + (3 * N_HEADS, 2 * D_MODEL)
    offs, o = [], 0
    for s in sizes[:-1]:
        o += s
        offs.append(o)
    return jnp.split(a, offs, axis=-1)


def _gather_pages(pool, page_table):
    g = pool[page_table]
    return g.reshape((page_table.shape[0], -1) + pool.shape[2:])


def _chunk_mix(u, v, w_s, b_s):
    B, T = u.shape[:2]
    nc = -(-T // CHUNK)
    vp = jnp.pad(v, ((0, 0), (0, nc * CHUNK - T), (0, 0))).reshape(B, nc, CHUNK, A_GROUPS, A_GROUP_DIM)
    s = jnp.einsum('gij,bcjgd->bcigd', jnp.tril(w_s), vp) + b_s.T[None, None, :, :, None]
    s = s.reshape(B, nc * CHUNK, A_WIDTH)[:, :T]
    return u * s


def _compress(rows, pe, w1, w2):
    B, L = rows.shape[:2]
    nb = L // CMP_BLOCK
    blk = rows[:, :nb * CMP_BLOCK].reshape(B, nb, CMP_BLOCK, N_KV, HEAD_DIM) + pe[None, None, :, None, :]
    flat = blk.transpose(0, 1, 3, 2, 4).reshape(B, nb, N_KV, CMP_BLOCK * HEAD_DIM)
    return jax.nn.gelu(flat @ w1) @ w2


def _cmp_branch(q, kc, vc, q_pos, rel_bias):
    T = q.shape[1]
    nb = kc.shape[1]
    s = jnp.einsum('btgrd,bngd->btgrn', q, kc).astype(jnp.float32) * ATTN_SCALE
    blk_end = jnp.arange(nb) * CMP_BLOCK + CMP_BLOCK - 1
    dist = q_pos[:, None] - blk_end[None, :]
    bias = rel_bias[_rel_bucket(dist)].reshape(T, nb, N_KV, GQA).transpose(0, 2, 3, 1)
    mask = (dist >= 0)[None, :, None, None, :]
    p = _masked_softmax(s + bias[None], mask, -1)
    o = jnp.einsum('btgrn,bngd->btgrd', p.astype(vc.dtype), vc)
    return o, p


def _select_blocks(p_cmp, q_pos, L):
    n_sel = -(-L // SEL_BLOCK)
    imp = jnp.sum(p_cmp, axis=3)
    imp = jnp.pad(imp, ((0, 0), (0, 0), (0, 0), (0, n_sel - imp.shape[-1])))
    j = jnp.arange(n_sel)
    cur = (q_pos // SEL_BLOCK)[:, None, None]
    forced = (j == 0) | (j == cur) | (j == cur - 1)
    score = jnp.where(j <= cur, imp + jnp.where(forced, FORCE_BONUS, 0.0), NEG)
    k = min(N_SEL, n_sel)
    return lax.top_k(score, k)[1].astype(jnp.int32)


def _sel_branch(q, ks, vs, sel_idx, q_pos, rel_bias):
    B, T = q.shape[:2]
    L = ks.shape[1]
    n_sel = -(-L // SEL_BLOCK)
    pad_l = n_sel * SEL_BLOCK - L

    def blocks(a):
        a = jnp.pad(a, ((0, 0), (0, pad_l), (0, 0), (0, 0)))
        return a.reshape(B, n_sel, SEL_BLOCK, N_KV, HEAD_DIM).transpose(0, 3, 1, 2, 4)

    kb, vb = blocks(ks), blocks(vs)
    N = B * T
    n_blk = -(-N // SEL_TOK_BLOCK)
    pad = n_blk * SEL_TOK_BLOCK - N

    def flat(a):
        a = a.reshape((N,) + a.shape[2:])
        a = jnp.pad(a, [(0, pad)] + [(0, 0)] * (a.ndim - 1))
        return a.reshape((n_blk, SEL_TOK_BLOCK) + a.shape[1:])

    qf = flat(q)
    idxf = flat(sel_idx)
    bidx = flat(jnp.broadcast_to(jnp.arange(B)[:, None], (B, T)))
    tpos = flat(jnp.broadcast_to(q_pos[None, :], (B, T)))
    bt = rel_bias.reshape(N_BUCKETS, N_KV, GQA).transpose(1, 0, 2)
    g_ar = jnp.arange(N_KV)[None, :, None]

    def one(args):
        qb, ib, bb, tb = args
        kg = kb[bb[:, None, None], g_ar, ib]
        vg = vb[bb[:, None, None], g_ar, ib]
        s = jnp.einsum('ngrd,ngksd->ngrks', qb, kg).astype(jnp.float32) * ATTN_SCALE
        pos = ib[..., None] * SEL_BLOCK + jnp.arange(SEL_BLOCK)
        dist = tb[:, None, None, None] - pos
        bias = bt[g_ar[..., None], _rel_bucket(dist)].transpose(0, 1, 4, 2, 3)
        p = _masked_softmax(s + bias, (dist >= 0)[:, :, None], (-2, -1))
        return jnp.einsum('ngrks,ngksd->ngrd', p.astype(vg.dtype), vg)

    o = lax.map(one, (qf, idxf, bidx, tpos))
    return o.reshape((n_blk * SEL_TOK_BLOCK,) + o.shape[2:])[:N].reshape(B, T, N_KV, GQA, HEAD_DIM)


def _win_branch(q, kw, vw, q_off, rel_bias):
    B, T = q.shape[:2]
    Lw = kw.shape[1]
    qb = min(Q_BLOCK, T)
    n_blk = -(-T // qb)
    Tp = n_blk * qb
    qp = jnp.pad(q, ((0, 0), (0, Tp - T), (0, 0), (0, 0), (0, 0))).reshape(B, n_blk, qb, N_KV, GQA, HEAD_DIM)
    back = max(0, q_off + Tp - Lw)
    kp = jnp.pad(kw, ((0, 0), (WINDOW, back), (0, 0), (0, 0)))
    vp = jnp.pad(vw, ((0, 0), (WINDOW, back), (0, 0), (0, 0)))
    span = qb + WINDOW
    kidx = q_off + jnp.arange(n_blk)[:, None] * qb + jnp.arange(span)[None, :]
    kg, vg = kp[:, kidx], vp[:, kidx]
    s = jnp.einsum('bjqgrd,bjsgd->bjgrqs', qp, kg).astype(jnp.float32) * ATTN_SCALE
    qidx = WINDOW + q_off + jnp.arange(n_blk)[:, None] * qb + jnp.arange(qb)[None, :]
    dist = qidx[:, :, None] - kidx[:, None, :]
    real = (kidx >= WINDOW) & (kidx < WINDOW + Lw)
    mask = (dist >= 0) & (dist < WINDOW) & real[:, None, :]
    bias = rel_bias[_rel_bucket(dist)].reshape(n_blk, qb, span, N_KV, GQA).transpose(0, 3, 4, 1, 2)
    p = _masked_softmax(s + bias[None], mask[None, :, None, None], -1)
    o = jnp.einsum('bjgrqs,bjsgd->bjqgrd', p.astype(vg.dtype), vg)
    return o.reshape(B, Tp, N_KV, GQA, HEAD_DIM)[:, :T]


def _peer(h, w_pq, sk1, sk2, eu, ev):
    B, T, D = h.shape
    N = B * T
    hf = h.reshape(N, D)
    q = (hf @ w_pq).reshape(N, PEER_HEADS, 2, PEER_KEY_DIM // 2)
    s1 = jnp.einsum('nhd,kd->nhk', q[:, :, 0], sk1).astype(jnp.float32)
    s2 = jnp.einsum('nhd,kd->nhk', q[:, :, 1], sk2).astype(jnp.float32)
    v1, i1 = lax.top_k(s1, PEER_TOPK)
    v2, i2 = lax.top_k(s2, PEER_TOPK)
    cand = (v1[..., :, None] + v2[..., None, :]).reshape(N, PEER_HEADS, PEER_TOPK * PEER_TOPK)
    sc, ci = lax.top_k(cand, PEER_TOPK)
    e = (jnp.take_along_axis(i1, ci // PEER_TOPK, axis=-1) * N_KEYS
         + jnp.take_along_axis(i2, ci % PEER_TOPK, axis=-1))
    gate = jax.nn.softmax(sc, axis=-1).astype(h.dtype)
    n_blk = -(-N // PEER_TOK_BLOCK)
    pad = n_blk * PEER_TOK_BLOCK - N

    def blk(a):
        a = jnp.pad(a, [(0, pad)] + [(0, 0)] * (a.ndim - 1))
        return a.reshape((n_blk, PEER_TOK_BLOCK) + a.shape[1:])

    def one(args):
        hb, eb, gb = args
        act = jax.nn.gelu(jnp.einsum('nd,nhkd->nhk', hb, eu[eb]))
        return jnp.einsum('nhk,nhkd->nd', gb * act, ev[eb])

    out = lax.map(one, (blk(hf), blk(e), blk(gate)))
    return out.reshape(n_blk * PEER_TOK_BLOCK, D)[:N].reshape(B, T, D)


def _layer(x, c, past, rel_bias, w_ada, b_ada, g_n1, g_n2, w_in, ln_v_g, ln_v_b, w_s, b_s,
           g_q, g_k, pe_k, w_c1k, w_c2k, pe_v, w_c1v, w_c2v, w_a, w_b, w_o,
           w_pq, sk1, sk2, eu, ev):
    B, T, _ = x.shape
    mod = jax.nn.silu(c) @ w_ada + b_ada
    sh1, sc1, gt1, sh2, sc2, gt2 = jnp.split(mod[:, None, :], 6, axis=-1)
    h = _rms(x, g_n1) * (1.0 + sc1) + sh1
    a_in, q, kc, vc, ks, vs, kw, vw, nsa_g, mg = _split_cols(h @ w_in)
    u, v = jnp.split(jax.nn.gelu(a_in), 2, axis=-1)
    v = _layernorm(v, ln_v_g, ln_v_b)
    y_a = _chunk_mix(u, v, w_s, b_s)
    q = _rms(q.reshape(B, T, N_KV, GQA, HEAD_DIM), g_q)
    kc, vc, vs, vw = [a.reshape(B, T, N_KV, HEAD_DIM) for a in (kc, vc, vs, vw)]
    ks = _rms(ks.reshape(B, T, N_KV, HEAD_DIM), g_k[1])
    kw = _rms(kw.reshape(B, T, N_KV, HEAD_DIM), g_k[2])
    if past is None:
        kc_all, vc_all, ks_all, vs_all, kw_keys, vw_keys = kc, vc, ks, vs, kw, vw
        q_off, pos0, wb = 0, 0, min(WINDOW, T)
    else:
        pkc, pvc, pks, pvs, bkw, bvw = past
        kc_all = jnp.concatenate([pkc, kc], axis=1)
        vc_all = jnp.concatenate([pvc, vc], axis=1)
        ks_all = jnp.concatenate([pks, ks], axis=1)
        vs_all = jnp.concatenate([pvs, vs], axis=1)
        kw_keys = jnp.concatenate([bkw, kw], axis=1)
        vw_keys = jnp.concatenate([bvw, vw], axis=1)
        q_off, pos0, wb = bkw.shape[1], pkc.shape[1], bkw.shape[1]
    q_pos = pos0 + jnp.arange(T)
    L = kc_all.shape[1]
    kcmp = _rms(_compress(kc_all, pe_k, w_c1k, w_c2k), g_k[0])
    vcmp = _compress(vc_all, pe_v, w_c1v, w_c2v)
    o_c, p_c = _cmp_branch(q, kcmp, vcmp, q_pos, rel_bias)
    sel = _select_blocks(p_c, q_pos, L)
    o_s = _sel_branch(q, ks_all, vs_all, sel, q_pos, rel_bias)
    o_w = _win_branch(q, kw_keys, vw_keys, q_off, rel_bias)
    g = jax.nn.sigmoid(nsa_g).reshape(B, T, N_KV, GQA, 3)
    o = g[..., 0:1] * o_c + g[..., 1:2] * o_s + g[..., 2:3] * o_w
    y_b = o.reshape(B, T, N_HEADS * HEAD_DIM)
    ga, gb = jnp.split(jax.nn.sigmoid(mg), 2, axis=-1)
    y_mix = (ga * (y_a @ w_a) + gb * (y_b @ w_b)) @ w_o
    x = x + gt1 * y_mix
    h2 = _rms(x, g_n2) * (1.0 + sc2) + sh2
    x = x + gt2 * _peer(h2, w_pq, sk1, sk2, eu, ev)
    v_chunk = v[:, ((T - 1) // CHUNK) * CHUNK:]
    return x, (kc, vc, ks, vs, kw_keys[:, -wb:], vw_keys[:, -wb:], v_chunk)


def setup_inputs(seed: int = 0) -> dict:
    key = jax.random.key(seed)
    k = jax.random.split(key, 40)
    f32 = jnp.float32
    n_pages = PAST_LEN // PAGE_SIZE
    n_phys = (DEC_BATCH * n_pages * 5) // 4
    wb = min(WINDOW, PAST_LEN)

    def nrm(kk, shape, s=1.0):
        return jax.random.normal(kk, shape, f32) * s

    cache_shape = (DEPTH, n_phys, PAGE_SIZE, N_KV, HEAD_DIM)
    win_shape = (DEPTH, DEC_BATCH, wb, N_KV, HEAD_DIM)
    page_table = jax.random.permutation(k[8], n_phys)[:DEC_BATCH * n_pages]
    page_table = page_table.reshape(DEC_BATCH, n_pages).astype(jnp.int32)
    return {
        "x_prompt": nrm(k[0], (BATCH, SEQ, D_MODEL)),
        "x_sample": nrm(k[1], (DEC_BATCH, DEC_SEQ, D_MODEL)),
        "cache_k_cmp": nrm(k[2], cache_shape),
        "cache_v_cmp": nrm(k[3], cache_shape),
        "cache_k_sel": nrm(k[4], cache_shape),
        "cache_v_sel": nrm(k[5], cache_shape),
        "state_k_win": nrm(k[6], win_shape),
        "state_v_win": nrm(k[7], win_shape),
        "page_table": page_table,
        "c_prompt": nrm(k[9], (BATCH, D_MODEL)),
        "c_sample": nrm(k[10], (DEC_BATCH, D_MODEL)),
        "rel_bias": nrm(k[11], (N_BUCKETS, N_HEADS), 0.5),
        "w_ada": nrm(k[12], (DEPTH, D_MODEL, 6 * D_MODEL), 0.5 * D_MODEL ** -0.5),
        "b_ada": nrm(k[13], (DEPTH, 6 * D_MODEL), 0.01),
        "g_n1": 1.0 + nrm(k[14], (DEPTH, D_MODEL), 0.01),
        "g_n2": 1.0 + nrm(k[15], (DEPTH, D_MODEL), 0.01),
        "w_in": nrm(k[16], (DEPTH, D_MODEL, D_IN), D_MODEL ** -0.5),
        "ln_v_g": 1.0 + nrm(k[17], (DEPTH, A_WIDTH), 0.01),
        "ln_v_b": nrm(k[18], (DEPTH, A_WIDTH), 0.01),
        "w_s": nrm(k[19], (DEPTH, A_GROUPS, CHUNK, CHUNK), CHUNK ** -0.5),
        "b_s": 1.0 + nrm(k[20], (DEPTH, A_GROUPS, CHUNK), 0.01),
        "g_q": 1.0 + nrm(k[21], (DEPTH, HEAD_DIM), 0.01),
        "g_k": 1.0 + nrm(k[22], (DEPTH, 3, HEAD_DIM), 0.01),
        "pe_k": nrm(k[23], (DEPTH, CMP_BLOCK, HEAD_DIM), 0.1),
        "w_c1k": nrm(k[24], (DEPTH, CMP_BLOCK * HEAD_DIM, CMP_HIDDEN), (CMP_BLOCK * HEAD_DIM) ** -0.5),
        "w_c2k": nrm(k[25], (DEPTH, CMP_HIDDEN, HEAD_DIM), CMP_HIDDEN ** -0.5),
        "pe_v": nrm(k[26], (DEPTH, CMP_BLOCK, HEAD_DIM), 0.1),
        "w_c1v": nrm(k[27], (DEPTH, CMP_BLOCK * HEAD_DIM, CMP_HIDDEN), (CMP_BLOCK * HEAD_DIM) ** -0.5),
        "w_c2v": nrm(k[28], (DEPTH, CMP_HIDDEN, HEAD_DIM), CMP_HIDDEN ** -0.5),
        "w_a": nrm(k[29], (DEPTH, A_WIDTH, D_MODEL), A_WIDTH ** -0.5),
        "w_b": nrm(k[30], (DEPTH, N_HEADS * HEAD_DIM, D_MODEL), (N_HEADS * HEAD_DIM) ** -0.5),
        "w_o": nrm(k[31], (DEPTH, D_MODEL, D_MODEL), D_MODEL ** -0.5),
        "w_pq": nrm(k[32], (DEPTH, D_MODEL, PEER_HEADS * PEER_KEY_DIM), D_MODEL ** -0.5),
        "sk1": nrm(k[33], (DEPTH, N_KEYS, PEER_KEY_DIM // 2), (PEER_KEY_DIM // 2) ** -0.5),
        "sk2": nrm(k[34], (DEPTH, N_KEYS, PEER_KEY_DIM // 2), (PEER_KEY_DIM // 2) ** -0.5),
        "expert_u": nrm(k[35], (DEPTH, N_EXPERTS, D_MODEL), D_MODEL ** -0.5),
        "expert_v": nrm(k[36], (DEPTH, N_EXPERTS, D_MODEL), 0.3),
    }


def reference(x_prompt, x_sample, cache_k_cmp, cache_v_cmp, cache_k_sel, cache_v_sel,
              state_k_win, state_v_win, page_table, c_prompt, c_sample, rel_bias,
              w_ada, b_ada, g_n1, g_n2, w_in, ln_v_g, ln_v_b, w_s, b_s, g_q, g_k,
              pe_k, w_c1k, w_c2k, pe_v, w_c1v, w_c2v, w_a, w_b, w_o,
              w_pq, sk1, sk2, expert_u, expert_v):
    y_p, y_s = x_prompt, x_sample
    st_p, st_s = [], []
    for l in range(DEPTH):
        lw = (w_ada[l], b_ada[l], g_n1[l], g_n2[l], w_in[l], ln_v_g[l], ln_v_b[l], w_s[l], b_s[l],
              g_q[l], g_k[l], pe_k[l], w_c1k[l], w_c2k[l], pe_v[l], w_c1v[l], w_c2v[l],
              w_a[l], w_b[l], w_o[l], w_pq[l], sk1[l], sk2[l], expert_u[l], expert_v[l])
        y_p, sp = _layer(y_p, c_prompt, None, rel_bias, *lw)
        past = (_gather_pages(cache_k_cmp[l], page_table), _gather_pages(cache_v_cmp[l], page_table),
                _gather_pages(cache_k_sel[l], page_table), _gather_pages(cache_v_sel[l], page_table),
                state_k_win[l], state_v_win[l])
        y_s, ss = _layer(y_s, c_sample, past, rel_bias, *lw)
        st_p.append(sp)
        st_s.append(ss)
    kc_p, vc_p, ks_p, vs_p, kw_p, vw_p, vch_p = [jnp.stack(a) for a in zip(*st_p)]
    kc_s, vc_s, ks_s, vs_s, kw_s, vw_s, vch_s = [jnp.stack(a) for a in zip(*st_s)]
    return (y_p, y_s, kc_p, vc_p, ks_p, vs_p, kw_p, vw_p, vch_p,
            kc_s, vc_s, ks_s, vs_s, kw_s, vw_s, vch_s)
```

```python
import functools
import math

import numpy as np
import jax
import jax.numpy as jnp
from jax import lax
from jax.experimental import pallas as pl
from jax.experimental.pallas import tpu as pltpu

F32 = jnp.float32
BF16 = jnp.bfloat16

N_HEADS = 16
HEAD_DIM = 128
N_KV = 4
GQA = N_HEADS // N_KV
KV_WIDTH = N_KV * HEAD_DIM
CHUNK = 128
A_GROUPS = 8
CMP_BLOCK = 64
SEL_BLOCK = 64
N_SEL = 16
WINDOW = 512
N_BUCKETS = 32
MAX_DISTANCE = 128
N_KEYS = 128
PEER_HEADS = 8
PEER_TOPK = 16
ATTN_SCALE = HEAD_DIM ** -0.5
NEG = -1e30
FORCE_BONUS = 1e4
EPS = 1e-6
LANE = 128
SUBLANE = 8
PROJ_TILE = 512
ATT_TILE = 256
VMEM_LIMIT = 56 * 1024 * 1024

T_U, T_V, T_Q, T_KC, T_VC, T_KS, T_VS, T_KW, T_VW, T_NSA, T_GA, T_GB, N_TILES = 0, 2, 4, 8, 9, 10, 11, 12, 13, 14, 15, 19, 23


def _bucket_thresholds():
    n = np.arange(0, 2 * MAX_DISTANCE)
    nf = np.maximum(n, 1).astype(np.float32)
    half = N_BUCKETS // 2
    large = half + (np.log(nf / half) / math.log(MAX_DISTANCE / half) * (N_BUCKETS - half)).astype(np.int32)
    b = np.where(n < half, n, np.minimum(large, N_BUCKETS - 1))
    assert np.all(np.diff(b) >= 0) and b[-1] == N_BUCKETS - 1
    return [int(np.argmax(b >= k)) for k in range(N_BUCKETS)]


BUCKET_THR = _bucket_thresholds()
FAR_DIST = BUCKET_THR[-1]
assert FAR_DIST <= MAX_DISTANCE


def _cparams(sem, vmem=VMEM_LIMIT):
    return pltpu.CompilerParams(dimension_semantics=sem, vmem_limit_bytes=vmem)


def _gelu(x):
    c = math.sqrt(2.0 / math.pi)
    return 0.5 * x * (1.0 + jnp.tanh(c * (x + 0.044715 * (x * x * x))))


def _sigmoid(x):
    return 1.0 / (1.0 + jnp.exp(-x))


def _dot_nt(a, b):
    return lax.dot_general(a, b, (((1,), (1,)), ((), ())), preferred_element_type=F32)


def _dot_tn(a, b):
    return lax.dot_general(a, b, (((0,), (0,)), ((), ())), preferred_element_type=F32)


def _bias_chain(dist, rbs):
    b = jnp.full(dist.shape, rbs[0], F32)
    for k in range(1, N_BUCKETS):
        b = jnp.where(dist >= BUCKET_THR[k], rbs[k], b)
    return b


def _mod_kernel(c_ref, w_ref, b_ref, o_ref):
    c = c_ref[...]
    a = (c * _sigmoid(c)).astype(BF16)
    o_ref[...] = jnp.dot(a, w_ref[...].astype(BF16), preferred_element_type=F32) + b_ref[...]


def _modulation(c_all, w_ada, b_ada):
    m, d = c_all.shape
    n = w_ada.shape[1]
    tn = 1024
    return pl.pallas_call(
        _mod_kernel,
        grid=(n // tn,),
        in_specs=[pl.BlockSpec((m, d), lambda j: (0, 0)),
                  pl.BlockSpec((d, tn), lambda j: (0, j)),
                  pl.BlockSpec((1, tn), lambda j: (0, j))],
        out_specs=pl.BlockSpec((m, tn), lambda j: (0, j)),
        out_shape=jax.ShapeDtypeStruct((m, n), F32),
        compiler_params=_cparams(("arbitrary",)),
        name="adaln_mod",
    )(c_all, w_ada, b_ada.reshape(1, n))


def _inproj_kernel(x_ref, sc_ref, sh_ref, gn_ref, w_ref, gain_ref, flag_ref, o_ref, h_scr):
    j = pl.program_id(1)

    @pl.when(j == 0)
    def _():
        x = x_ref[...]
        r = lax.rsqrt(jnp.mean(x * x, axis=-1, keepdims=True) + EPS)
        h = (x * r) * gn_ref[...] * (1.0 + sc_ref[...]) + sh_ref[...]
        h_scr[...] = h.reshape(h_scr.shape).astype(BF16)

    y = jnp.dot(h_scr[...], w_ref[...], preferred_element_type=F32)

    @pl.when(j < T_Q)
    def _():
        o_ref[...] = _gelu(y)

    @pl.when((j >= T_Q) & (j < T_NSA))
    def _():
        parts = []
        for hh in range(PROJ_TILE // HEAD_DIM):
            yh = y[:, hh * HEAD_DIM:(hh + 1) * HEAD_DIM]
            parts.append(yh * lax.rsqrt(jnp.mean(yh * yh, axis=-1, keepdims=True) + EPS))
        yn = jnp.concatenate(parts, axis=1) * gain_ref[...]
        o_ref[...] = jnp.where(flag_ref[...] > 0.5, yn, y)

    @pl.when(j >= T_NSA)
    def _():
        o_ref[...] = _sigmoid(y)


def _in_projection(x3, mod4, g_n1, w_in_p, gain, flag, bt, tt):
    nb, tb, d = x3.shape
    tpb = tb // tt
    tm = bt * tt
    n = nb * tb
    grid = (n // tm, N_TILES)
    return pl.pallas_call(
        _inproj_kernel,
        grid=grid,
        in_specs=[pl.BlockSpec((bt, tt, d), lambda i, j: (i // tpb, i % tpb, 0)),
                  pl.BlockSpec((bt, None, 1, d), lambda i, j: (i // tpb, 1, 0, 0)),
                  pl.BlockSpec((bt, None, 1, d), lambda i, j: (i // tpb, 0, 0, 0)),
                  pl.BlockSpec((1, 1, d), lambda i, j: (0, 0, 0)),
                  pl.BlockSpec((d, PROJ_TILE), lambda i, j: (0, j)),
                  pl.BlockSpec((None, 1, PROJ_TILE), lambda i, j: (j, 0, 0)),
                  pl.BlockSpec((None, 1, PROJ_TILE), lambda i, j: (j, 0, 0))],
        out_specs=pl.BlockSpec((None, tm, PROJ_TILE), lambda i, j: (j, i, 0)),
        out_shape=jax.ShapeDtypeStruct((N_TILES, n, PROJ_TILE), F32),
        scratch_shapes=[pltpu.VMEM((tm, d), BF16)],
        compiler_params=_cparams(("parallel", "arbitrary")),
        name="in_projection",
    )(x3, mod4, mod4, g_n1.reshape(1, 1, d), w_in_p, gain, flag)


def _mixa_kernel(u0_ref, u1_ref, v0_ref, v1_ref, lg_ref, lb_ref, wm_ref, bs_ref, ya_ref, vch_ref):
    v = jnp.concatenate([v0_ref[...], v1_ref[...]], axis=1)
    mu = jnp.mean(v, axis=-1, keepdims=True)
    var = jnp.mean(jnp.square(v - mu), axis=-1, keepdims=True)
    vln = ((v - mu) * lax.rsqrt(var + EPS)) * lg_ref[...] + lb_ref[...]
    vch_ref[...] = vln
    u = jnp.concatenate([u0_ref[...], u1_ref[...]], axis=1)
    vb = vln.astype(BF16)
    gd = vln.shape[1] // A_GROUPS
    for g in range(A_GROUPS):
        sl = slice(g * gd, (g + 1) * gd)
        s = jnp.dot(wm_ref[g], vb[:, sl], preferred_element_type=F32) + bs_ref[g]
        ya_ref[:, sl] = (u[:, sl] * s).astype(BF16)


def _mixer_a(proj, ln_g, ln_b, wm, bsb, vch_blocks, vch_map):
    n = proj.shape[1]
    aw = 2 * PROJ_TILE

    def tile(k):
        return pl.BlockSpec((None, CHUNK, PROJ_TILE), lambda i, k=k: (k, i, 0))

    return pl.pallas_call(
        _mixa_kernel,
        grid=(n // CHUNK,),
        in_specs=[tile(T_U), tile(T_U + 1), tile(T_V), tile(T_V + 1),
                  pl.BlockSpec((1, aw), lambda i: (0, 0)),
                  pl.BlockSpec((1, aw), lambda i: (0, 0)),
                  pl.BlockSpec((A_GROUPS, CHUNK, CHUNK), lambda i: (0, 0, 0)),
                  pl.BlockSpec((A_GROUPS, CHUNK, CHUNK), lambda i: (0, 0, 0))],
        out_specs=[pl.BlockSpec((CHUNK, aw), lambda i: (i, 0)),
                   pl.BlockSpec((CHUNK, aw), lambda i: (vch_map(i), 0))],
        out_shape=[jax.ShapeDtypeStruct((n, aw), BF16),
                   jax.ShapeDtypeStruct((vch_blocks * CHUNK, aw), F32)],
        compiler_params=_cparams(("arbitrary",)),
        name="mixer_a",
    )(proj, proj, proj, proj, ln_g.reshape(1, aw), ln_b.reshape(1, aw), wm, bsb)


def _compress_kernel(x_ref, pe_ref, w1_ref, w2_ref, gain_ref, o_ref, *, tm, do_rms):
    hid = w1_ref.shape[1]
    acc = jnp.zeros((N_KV * tm, hid), F32)
    for c in range(CMP_BLOCK // 2):
        rows = []
        for g in range(N_KV):
            a = x_ref[:, ((2 * c) * N_KV + g) * HEAD_DIM:((2 * c) * N_KV + g + 1) * HEAD_DIM] + pe_ref[2 * c:2 * c + 1, :]
            b = x_ref[:, ((2 * c + 1) * N_KV + g) * HEAD_DIM:((2 * c + 1) * N_KV + g + 1) * HEAD_DIM] + pe_ref[2 * c + 1:2 * c + 2, :]
            rows.append(jnp.concatenate([a, b], axis=1))
        lhs = jnp.concatenate(rows, axis=0).astype(BF16)
        acc = acc + jnp.dot(lhs, w1_ref[c * 2 * HEAD_DIM:(c + 1) * 2 * HEAD_DIM, :], preferred_element_type=F32)
    out = jnp.dot(_gelu(acc).astype(BF16), w2_ref[...], preferred_element_type=F32)
    if do_rms:
        out = out * lax.rsqrt(jnp.mean(out * out, axis=-1, keepdims=True) + EPS) * gain_ref[...]
    for g in range(N_KV):
        o_ref[:, g * HEAD_DIM:(g + 1) * HEAD_DIM] = out[g * tm:(g + 1) * tm]


def _compress(x, x_map, m, pe, w1b, w2b, gain, do_rms, tm=64):
    width = CMP_BLOCK * KV_WIDTH
    tm = min(tm, m)
    blk = (None,) * (x.ndim - 2) + (tm, width)
    hid = w1b.shape[1]
    return pl.pallas_call(
        functools.partial(_compress_kernel, tm=tm, do_rms=do_rms),
        grid=(m // tm,),
        in_specs=[pl.BlockSpec(blk, x_map),
                  pl.BlockSpec((CMP_BLOCK, HEAD_DIM), lambda i: (0, 0)),
                  pl.BlockSpec((CMP_BLOCK * HEAD_DIM, hid), lambda i: (0, 0)),
                  pl.BlockSpec((hid, HEAD_DIM), lambda i: (0, 0)),
                  pl.BlockSpec((1, HEAD_DIM), lambda i: (0, 0))],
        out_specs=pl.BlockSpec((tm, KV_WIDTH), lambda i: (i, 0)),
        out_shape=jax.ShapeDtypeStruct((m, KV_WIDTH), F32),
        compiler_params=_cparams(("arbitrary",)),
        name="compress",
    )(x, pe, w1b, w2b, gain.reshape(1, HEAD_DIM))


def _bias_table_kernel(rb_ref, o_ref, *, ts):
    g = pl.program_id(0)
    i = lax.broadcasted_iota(jnp.int32, (ts, ts), 0)
    j = lax.broadcasted_iota(jnp.int32, (ts, ts), 1)
    for d in range(2):
        dist = d * ts + i - j
        for r in range(GQA):
            rbs = [rb_ref[k, g * GQA + r] for k in range(N_BUCKETS)]
            o_ref[d, r * ts:(r + 1) * ts, :] = _bias_chain(dist, rbs)


def _bias_tables(rel_bias, ts):
    return pl.pallas_call(
        functools.partial(_bias_table_kernel, ts=ts),
        grid=(N_KV,),
        in_specs=[pl.BlockSpec(memory_space=pltpu.SMEM)],
        out_specs=pl.BlockSpec((None, 2, GQA * ts, ts), lambda g: (g, 0, 0, 0)),
        out_shape=jax.ShapeDtypeStruct((N_KV, 2, GQA * ts, ts), F32),
        compiler_params=_cparams(("arbitrary",)),
        name="bias_tables",
    )(rel_bias)


def _rank_select(score, n_sel):
    jidx = lax.broadcasted_iota(jnp.int32, score.shape, 0)
    rank = jnp.zeros(score.shape, F32)
    for i in range(n_sel):
        row = score[i:i + 1, :]
        beats = (row > score) | ((row == score) & (jidx > i))
        rank = rank + beats.astype(F32)
    return ((rank < float(min(N_SEL, n_sel))) & (jidx < n_sel)).astype(F32)


def _cmp_kernel(rb_ref, q0_ref, q1_ref, q2_ref, q3_ref, nsa_ref, kc_ref, vc_ref, oc_ref, sel_ref, *, tq, nb, n_sel):
    qt = pl.program_id(1)
    nsa = nsa_ref[...]
    row = lax.broadcasted_iota(jnp.int32, (tq, LANE), 0) + qt * tq
    col = lax.broadcasted_iota(jnp.int32, (tq, LANE), 1)
    dist = row - (col * CMP_BLOCK + CMP_BLOCK - 1)
    valid = (dist >= 0) & (col < nb)
    cur = row // SEL_BLOCK
    forced = (col == 0) | (col == cur) | (col == cur - 1)
    pad = jnp.zeros((LANE - nb, HEAD_DIM), F32)
    for g in range(N_KV):
        q = (q0_ref, q1_ref, q2_ref, q3_ref)[g][...]
        qst = jnp.concatenate([q[:, r * HEAD_DIM:(r + 1) * HEAD_DIM] for r in range(GQA)], axis=0).astype(BF16)
        kg = jnp.concatenate([kc_ref[:, g * HEAD_DIM:(g + 1) * HEAD_DIM], pad], axis=0).astype(BF16)
        vg = jnp.concatenate([vc_ref[:, g * HEAD_DIM:(g + 1) * HEAD_DIM], pad], axis=0).astype(BF16)
        s = _dot_nt(qst, kg) * ATTN_SCALE
        ps = []
        imp = jnp.zeros((tq, LANE), F32)
        for r in range(GQA):
            h = g * GQA + r
            b = _bias_chain(dist, [rb_ref[k, h] for k in range(N_BUCKETS)])
            sr = jnp.where(valid, s[r * tq:(r + 1) * tq] + b, NEG)
            m = jnp.max(sr, axis=-1, keepdims=True)
            p = jnp.where(valid, jnp.exp(sr - m), 0.0)
            den = jnp.sum(p, axis=-1, keepdims=True)
            p = p * (1.0 / jnp.maximum(den, 1e-30))
            imp = imp + p
            ps.append(p)
        o = jnp.dot(jnp.concatenate(ps, axis=0).astype(BF16), vg, preferred_element_type=F32)
        for r in range(GQA):
            h = g * GQA + r
            oc_ref[:, h * HEAD_DIM:(h + 1) * HEAD_DIM] = o[r * tq:(r + 1) * tq] * nsa[:, 3 * h:3 * h + 1]
        score = jnp.where(col <= cur, imp + jnp.where(forced, FORCE_BONUS, 0.0), NEG)
        score = jnp.where(col < n_sel, score, -3e38)
        sel_ref[g] = _rank_select(score.T, n_sel).T


def _cmp_select(proj, rel_bias, kcmp, vcmp, bsz, t, tq=ATT_TILE):
    nb = kcmp.shape[0] // bsz
    n_sel = -(-t // SEL_BLOCK)
    nq = t // tq
    n = proj.shape[1]

    def tile(k):
        return pl.BlockSpec((None, tq, PROJ_TILE), lambda b, i, k=k: (k, b * nq + i, 0))

    return pl.pallas_call(
        functools.partial(_cmp_kernel, tq=tq, nb=nb, n_sel=n_sel),
        grid=(bsz, nq),
        in_specs=[pl.BlockSpec(memory_space=pltpu.SMEM),
                  tile(T_Q), tile(T_Q + 1), tile(T_Q + 2), tile(T_Q + 3), tile(T_NSA),
                  pl.BlockSpec((nb, KV_WIDTH), lambda b, i: (b, 0)),
                  pl.BlockSpec((nb, KV_WIDTH), lambda b, i: (b, 0))],
        out_specs=[pl.BlockSpec((tq, N_HEADS * HEAD_DIM), lambda b, i: (b * nq + i, 0)),
                   pl.BlockSpec((None, N_KV, tq, LANE), lambda b, i: (b, 0, i, 0))],
        out_shape=[jax.ShapeDtypeStruct((n, N_HEADS * HEAD_DIM), F32),
                   jax.ShapeDtypeStruct((bsz, N_KV, t, LANE), F32)],
        compiler_params=_cparams(("parallel", "arbitrary")),
        name="cmp_select",
    )(rel_bias, proj, proj, proj, proj, proj, kcmp, vcmp)


def _attn_kernel(rb_ref, q_ref, nsa_ref, k_ref, v_ref, tb_ref, *rest, mode, tq, branch):
    if mode == "sel":
        sel_ref, e_ref, o_ref, kb, vb, qs, m_s, l_s, acc_s = rest
    else:
        o_ref, kb, vb, qs, m_s, l_s, acc_s = rest
    g = pl.program_id(1)
    qt = pl.program_id(2)
    tk = tq

    @pl.when(qt == 0)
    def _():
        kb[...] = k_ref[...].astype(BF16)
        vb[...] = v_ref[...].astype(BF16)

    q = q_ref[...]
    qs[...] = jnp.concatenate([q[:, r * HEAD_DIM:(r + 1) * HEAD_DIM] for r in range(GQA)], axis=0).astype(BF16)
    m_s[...] = jnp.full(m_s.shape, NEG, F32)
    l_s[...] = jnp.zeros(l_s.shape, F32)
    acc_s[...] = jnp.zeros(acc_s.shape, F32)
    rowpos = lax.broadcasted_iota(jnp.int32, (GQA * tq, tk), 0) & (tq - 1)
    colpos = lax.broadcasted_iota(jnp.int32, (GQA * tq, tk), 1)
    far_bias = jnp.concatenate([jnp.full((tq, 1), rb_ref[N_BUCKETS - 1, g * GQA + r], F32) for r in range(GQA)], axis=0)
    if mode == "sel":
        sel4 = jnp.concatenate([sel_ref[...]] * GQA, axis=0).astype(BF16)

    def chunk(kt, bias, mask):
        k0 = pl.multiple_of(kt * tk, tk)
        kc = kb[pl.ds(k0, tk), :]
        vc = vb[pl.ds(k0, tk), :]
        s = _dot_nt(qs[...], kc) * ATTN_SCALE + bias
        if mode == "sel":
            smask = jnp.dot(sel4, e_ref[kt], preferred_element_type=F32) > 0.5
            mask = smask if mask is None else (mask & smask)
        if mask is not None:
            s = jnp.where(mask, s, NEG)
        m_old = m_s[...]
        m_new = jnp.maximum(m_old, jnp.max(s, axis=-1, keepdims=True))
        p = jnp.exp(s - m_new)
        if mask is not None:
            p = jnp.where(mask, p, 0.0)
        a = jnp.exp(m_old - m_new)
        l_s[...] = a * l_s[...] + jnp.sum(p, axis=-1, keepdims=True)
        acc_s[...] = a * acc_s[...] + jnp.dot(p.astype(BF16), vc, preferred_element_type=F32)
        m_s[...] = m_new

    if mode == "sel":
        def far_body(kt, carry):
            chunk(kt, far_bias, None)
            return carry
        lax.fori_loop(0, jnp.maximum(qt - 1, 0), far_body, 0)
    else:
        @pl.when(qt >= 2)
        def _():
            chunk(qt - 2, far_bias, colpos > rowpos)

    @pl.when(qt >= 1)
    def _():
        chunk(qt - 1, tb_ref[1], None)

    chunk(qt, tb_ref[0], rowpos >= colpos)

    o = acc_s[...] * (1.0 / jnp.maximum(l_s[...], 1e-30))
    nsa = nsa_ref[...]
    lane = lax.broadcasted_iota(jnp.int32, nsa.shape, 1)
    for r in range(GQA):
        gidx = (g * GQA + r) * 3 + branch
        gate = jnp.sum(jnp.where(lane == gidx, nsa, 0.0), axis=-1, keepdims=True)
        o_ref[:, r * HEAD_DIM:(r + 1) * HEAD_DIM] = o[r * tq:(r + 1) * tq] * gate


def _prompt_attention(proj, rel_bias, tables, bsz, t, mode, sel=None, emat=None, tq=ATT_TILE):
    nq = t // tq
    n = proj.shape[1]
    if mode == "sel":
        tk_, tv_, branch = T_KS, T_VS, 1
    else:
        tk_, tv_, branch = T_KW, T_VW, 2
        assert WINDOW == 2 * tq
    in_specs = [pl.BlockSpec(memory_space=pltpu.SMEM),
                pl.BlockSpec((None, tq, PROJ_TILE), lambda b, g, i: (T_Q + g, b * nq + i, 0)),
                pl.BlockSpec((None, tq, PROJ_TILE), lambda b, g, i: (T_NSA, b * nq + i, 0)),
                pl.BlockSpec((None, t, HEAD_DIM), lambda b, g, i: (tk_, b, g)),
                pl.BlockSpec((None, t, HEAD_DIM), lambda b, g, i: (tv_, b, g)),
                pl.BlockSpec((None, 2, GQA * tq, tq), lambda b, g, i: (g, 0, 0, 0))]
    args = [rel_bias, proj, proj, proj, proj, tables]
    if mode == "sel":
        in_specs += [pl.BlockSpec((None, None, tq, LANE), lambda b, g, i: (b, g, i, 0)),
                     pl.BlockSpec((nq, LANE, tq), lambda b, g, i: (0, 0, 0))]
        args += [sel, emat]
    return pl.pallas_call(
        functools.partial(_attn_kernel, mode=mode, tq=tq, branch=branch),
        grid=(bsz, N_KV, nq),
        in_specs=in_specs,
        out_specs=pl.BlockSpec((tq, GQA * HEAD_DIM), lambda b, g, i: (b * nq + i, g)),
        out_shape=jax.ShapeDtypeStruct((n, N_HEADS * HEAD_DIM), F32),
        scratch_shapes=[pltpu.VMEM((t, HEAD_DIM), BF16), pltpu.VMEM((t, HEAD_DIM), BF16),
                        pltpu.VMEM((GQA * tq, HEAD_DIM), BF16),
                        pltpu.VMEM((GQA * tq, 1), F32), pltpu.VMEM((GQA * tq, 1), F32),
                        pltpu.VMEM((GQA * tq, HEAD_DIM), F32)],
        compiler_params=_cparams(("parallel", "parallel", "arbitrary")),
        name="attn_" + mode,
    )(*args)


def _masked_softmax(s, mask):
    s = jnp.where(mask, s, NEG)
    m = jnp.max(s, axis=-1, keepdims=True)
    p = jnp.where(mask, jnp.exp(s - m), 0.0)
    den = jnp.sum(p, axis=-1, keepdims=True)
    return p * (1.0 / jnp.maximum(den, 1e-30))


def _near_far_bias(rb_ref, g, dist_near, tdec, width, near):
    rows = []
    for r in range(GQA):
        h = g * GQA + r
        rbs = [rb_ref[k, h] for k in range(N_BUCKETS)]
        nb_ = _bias_chain(dist_near, rbs)
        rows.append(jnp.concatenate([jnp.full((tdec, width - near), rbs[-1], F32), nb_], axis=1))
    return jnp.concatenate(rows, axis=0)


def _sattn_kernel(pt_ref, rb_ref, q0_ref, q1_ref, q2_ref, q3_ref, ksn_ref, vsn_ref, kwn_ref, vwn_ref, nsa_ref,
                  kc_ref, vc_ref, *rest, npg, page, tdec, wlen):
    kpages = rest[:npg]
    vpages = rest[npg:2 * npg]
    skw_ref, svw_ref, e_ref, yb_ref, kwo_ref, vwo_ref, kbuf, vbuf, wkb, wvb = rest[2 * npg:]
    past = npg * page
    lk = past + LANE
    nb = kc_ref.shape[0]
    n_sel = -(-(past + tdec) // SEL_BLOCK)
    near = 2 * LANE
    rq = GQA * tdec
    nsa = nsa_ref[...]

    kwo_ref[0:wlen - tdec, :] = skw_ref[tdec:wlen, :]
    kwo_ref[wlen - tdec:wlen, :] = kwn_ref[...]
    vwo_ref[0:wlen - tdec, :] = svw_ref[tdec:wlen, :]
    vwo_ref[wlen - tdec:wlen, :] = vwn_ref[...]

    trow1 = lax.broadcasted_iota(jnp.int32, (tdec, LANE), 0) + past
    col1 = lax.broadcasted_iota(jnp.int32, (tdec, LANE), 1)
    dist_c = trow1 - (col1 * CMP_BLOCK + CMP_BLOCK - 1)
    valid_c = (dist_c >= 0) & (col1 < nb)
    cur = trow1 // SEL_BLOCK
    forced = (col1 == 0) | (col1 == cur) | (col1 == cur - 1)

    trow_s = (lax.broadcasted_iota(jnp.int32, (rq, lk), 0) & (tdec - 1)) + past
    pos_s = lax.broadcasted_iota(jnp.int32, (rq, lk), 1)
    causal_s = pos_s <= trow_s
    dist_sn = (lax.broadcasted_iota(jnp.int32, (tdec, near), 0) + past) - (lax.broadcasted_iota(jnp.int32, (tdec, near), 1) + lk - near)

    wl = wlen + LANE
    qidx_w = (lax.broadcasted_iota(jnp.int32, (rq, wl), 0) & (tdec - 1)) + wlen
    kidx_w = lax.broadcasted_iota(jnp.int32, (rq, wl), 1)
    dist_w = qidx_w - kidx_w
    mask_w = (dist_w >= 0) & (dist_w < WINDOW)
    dist_wn = (lax.broadcasted_iota(jnp.int32, (tdec, near), 0) + wlen) - (lax.broadcasted_iota(jnp.int32, (tdec, near), 1) + wl - near)

    zpad = jnp.zeros((LANE - tdec, HEAD_DIM), F32)
    cpad = jnp.zeros((LANE - nb, HEAD_DIM), F32)
    for g in range(N_KV):
        gs = slice(g * HEAD_DIM, (g + 1) * HEAD_DIM)
        q = (q0_ref, q1_ref, q2_ref, q3_ref)[g][...]
        qst = jnp.concatenate([q[:, r * HEAD_DIM:(r + 1) * HEAD_DIM] for r in range(GQA)], axis=0).astype(BF16)

        kg = jnp.concatenate([kc_ref[:, gs], cpad], axis=0).astype(BF16)
        vg = jnp.concatenate([vc_ref[:, gs], cpad], axis=0).astype(BF16)
        s = _dot_nt(qst, kg) * ATTN_SCALE
        ps = []
        imp = jnp.zeros((tdec, LANE), F32)
        for r in range(GQA):
            h = g * GQA + r
            b = _bias_chain(dist_c, [rb_ref[k, h] for k in range(N_BUCKETS)])
            p = _masked_softmax(s[r * tdec:(r + 1) * tdec] + b, valid_c)
            imp = imp + p
            ps.append(p)
        o_c = jnp.dot(jnp.concatenate(ps, axis=0).astype(BF16), vg, preferred_element_type=F32)

        score = jnp.where(col1 <= cur, imp + jnp.where(forced, FORCE_BONUS, 0.0), NEG)
        score = jnp.where(col1 < n_sel, score, -3e38)
        rank = jnp.zeros((tdec, LANE), F32)
        for i in range(n_sel):
            ci = score[:, i:i + 1]
            rank = rank + ((ci > score) | ((ci == score) & (col1 > i))).astype(F32)
        sel = ((rank < float(min(N_SEL, n_sel))) & (col1 < n_sel)).astype(F32)

        for p_ in range(npg):
            kbuf[p_ * page:(p_ + 1) * page, :] = kpages[p_][:, gs].astype(BF16)
            vbuf[p_ * page:(p_ + 1) * page, :] = vpages[p_][:, gs].astype(BF16)
        kbuf[past:lk, :] = jnp.concatenate([ksn_ref[:, gs], zpad], axis=0).astype(BF16)
        vbuf[past:lk, :] = jnp.concatenate([vsn_ref[:, gs], zpad], axis=0).astype(BF16)
        s = _dot_nt(qst, kbuf[...]) * ATTN_SCALE + _near_far_bias(rb_ref, g, dist_sn, tdec, lk, near)
        sel4 = jnp.concatenate([sel] * GQA, axis=0).astype(BF16)
        mask = (jnp.dot(sel4, e_ref[...], preferred_element_type=F32) > 0.5) & causal_s
        o_s = jnp.dot(_masked_softmax(s, mask).astype(BF16), vbuf[...], preferred_element_type=F32)

        wkb[0:wlen, :] = skw_ref[:, gs].astype(BF16)
        wvb[0:wlen, :] = svw_ref[:, gs].astype(BF16)
        wkb[wlen:wl, :] = jnp.concatenate([kwn_ref[:, gs], zpad], axis=0).astype(BF16)
        wvb[wlen:wl, :] = jnp.concatenate([vwn_ref[:, gs], zpad], axis=0).astype(BF16)
        s = _dot_nt(qst, wkb[...]) * ATTN_SCALE + _near_far_bias(rb_ref, g, dist_wn, tdec, wl, near)
        o_w = jnp.dot(_masked_softmax(s, mask_w).astype(BF16), wvb[...], preferred_element_type=F32)

        for r in range(GQA):
            h = g * GQA + r
            rs = slice(r * tdec, (r + 1) * tdec)
            yb_ref[:, h * HEAD_DIM:(h + 1) * HEAD_DIM] = (nsa[:, 3 * h:3 * h + 1] * o_c[rs]
                                                         + nsa[:, 3 * h + 1:3 * h + 2] * o_s[rs]
                                                         + nsa[:, 3 * h + 2:3 * h + 3] * o_w[rs])


def _sample_attention(proj, rel_bias, page_table, kcmp_g, vcmp_g, ck_sel, cv_sel, skw, svw, emat, bsz, tdec):
    npg = page_table.shape[1]
    page = ck_sel.shape[1]
    wlen = skw.shape[1]
    nb = kcmp_g.shape[1]
    past = npg * page
    lk = past + LANE
    assert tdec == SUBLANE and FAR_DIST <= LANE and wlen == WINDOW

    def tile(k):
        return pl.BlockSpec((None, tdec, PROJ_TILE), lambda b, pt, k=k: (k, b, 0))

    def pagespec(p_):
        return pl.BlockSpec((None, page, KV_WIDTH), lambda b, pt, p_=p_: (pt[b, p_], 0, 0))

    in_specs = ([pl.BlockSpec(memory_space=pltpu.SMEM)]
                + [tile(T_Q + g) for g in range(N_KV)]
                + [tile(T_KS), tile(T_VS), tile(T_KW), tile(T_VW), tile(T_NSA)]
                + [pl.BlockSpec((None, nb, KV_WIDTH), lambda b, pt: (b, 0, 0))] * 2
                + [pagespec(p_) for p_ in range(npg)] * 2
                + [pl.BlockSpec((None, wlen, KV_WIDTH), lambda b, pt: (b, 0, 0))] * 2
                + [pl.BlockSpec((LANE, lk), lambda b, pt: (0, 0))])
    grid_spec = pltpu.PrefetchScalarGridSpec(
        num_scalar_prefetch=1,
        grid=(bsz,),
        in_specs=in_specs,
        out_specs=[pl.BlockSpec((tdec, N_HEADS * HEAD_DIM), lambda b, pt: (b, 0)),
                   pl.BlockSpec((None, wlen, KV_WIDTH), lambda b, pt: (b, 0, 0)),
                   pl.BlockSpec((None, wlen, KV_WIDTH), lambda b, pt: (b, 0, 0))],
        scratch_shapes=[pltpu.VMEM((lk, HEAD_DIM), BF16), pltpu.VMEM((lk, HEAD_DIM), BF16),
                        pltpu.VMEM((wlen + LANE, HEAD_DIM), BF16), pltpu.VMEM((wlen + LANE, HEAD_DIM), BF16)])
    return pl.pallas_call(
        functools.partial(_sattn_kernel, npg=npg, page=page, tdec=tdec, wlen=wlen),
        grid_spec=grid_spec,
        out_shape=[jax.ShapeDtypeStruct((bsz * tdec, N_HEADS * HEAD_DIM), F32),
                   jax.ShapeDtypeStruct((bsz, wlen, KV_WIDTH), F32),
                   jax.ShapeDtypeStruct((bsz, wlen, KV_WIDTH), F32)],
        compiler_params=_cparams(("arbitrary",)),
        name="sample_attention",
    )(page_table, rel_bias, *([proj] * 9), kcmp_g, vcmp_g, *([ck_sel] * npg), *([cv_sel] * npg), skw, svw, emat)


def _merge_kernel(ya_ref, *rest, n_yb):
    yb_refs = rest[:n_yb]
    wa_ref, wb_ref, ga_ref, gb_ref, t_ref, yb_scr = rest[n_yb:]

    @pl.when(pl.program_id(1) == 0)
    def _():
        yb = yb_refs[0][...]
        for r in yb_refs[1:]:
            yb = yb + r[...]
        yb_scr[...] = yb.astype(BF16)

    a = jnp.dot(ya_ref[...], wa_ref[...], preferred_element_type=F32)
    b = jnp.dot(yb_scr[...], wb_ref[...], preferred_element_type=F32)
    t_ref[...] = (ga_ref[...] * a + gb_ref[...] * b).astype(BF16)


def _merge(proj, ya, ybs, wa_b, wb_b, tm=512):
    n, aw = ya.shape
    d = wb_b.shape[0]
    nj = d // PROJ_TILE
    return pl.pallas_call(
        functools.partial(_merge_kernel, n_yb=len(ybs)),
        grid=(n // tm, nj),
        in_specs=([pl.BlockSpec((tm, aw), lambda i, j: (i, 0))]
                  + [pl.BlockSpec((tm, d), lambda i, j: (i, 0))] * len(ybs)
                  + [pl.BlockSpec((aw, PROJ_TILE), lambda i, j: (0, j)),
                     pl.BlockSpec((d, PROJ_TILE), lambda i, j: (0, j)),
                     pl.BlockSpec((None, tm, PROJ_TILE), lambda i, j: (T_GA + j, i, 0)),
                     pl.BlockSpec((None, tm, PROJ_TILE), lambda i, j: (T_GB + j, i, 0))]),
        out_specs=pl.BlockSpec((tm, PROJ_TILE), lambda i, j: (i, j)),
        out_shape=jax.ShapeDtypeStruct((n, d), BF16),
        scratch_shapes=[pltpu.VMEM((tm, d), BF16)],
        compiler_params=_cparams(("parallel", "arbitrary")),
        name="merge",
    )(ya, *ybs, wa_b, wb_b, proj, proj)


def _outproj_kernel(t_ref, x_ref, gt_ref, sc_ref, sh_ref, gn_ref, wo_ref, x1_ref, h2_ref):
    y = jnp.dot(t_ref[...], wo_ref[...], preferred_element_type=F32)
    x1 = x_ref[...] + gt_ref[...] * y.reshape(x_ref.shape)
    x1_ref[...] = x1
    r = lax.rsqrt(jnp.mean(x1 * x1, axis=-1, keepdims=True) + EPS)
    h2 = (x1 * r) * gn_ref[...] * (1.0 + sc_ref[...]) + sh_ref[...]
    h2_ref[...] = h2.reshape(h2_ref.shape).astype(BF16)


def _out_projection(tmix, x3, mod4, g_n2, wo_b, bt, tt):
    nb, tb, d = x3.shape
    tpb = tb // tt
    tm = bt * tt
    n = nb * tb

    def modspec(k):
        return pl.BlockSpec((bt, None, 1, d), lambda i, k=k: (i // tpb, k, 0, 0))

    return pl.pallas_call(
        _outproj_kernel,
        grid=(n // tm,),
        in_specs=[pl.BlockSpec((tm, d), lambda i: (i, 0)),
                  pl.BlockSpec((bt, tt, d), lambda i: (i // tpb, i % tpb, 0)),
                  modspec(2), modspec(4), modspec(3),
                  pl.BlockSpec((1, 1, d), lambda i: (0, 0, 0)),
                  pl.BlockSpec((d, d), lambda i: (0, 0))],
        out_specs=[pl.BlockSpec((bt, tt, d), lambda i: (i // tpb, i % tpb, 0)),
                   pl.BlockSpec((tm, d), lambda i: (i, 0))],
        out_shape=[jax.ShapeDtypeStruct((nb, tb, d), F32), jax.ShapeDtypeStruct((n, d), BF16)],
        compiler_params=_cparams(("arbitrary",)),
        name="out_projection",
    )(tmix, x3, mod4, mod4, mod4, g_n2.reshape(1, 1, d), wo_b)


def _peer_scores_kernel(h_ref, wpq_ref, sk1_ref, sk2_ref, s1_ref, s2_ref):
    pq = jnp.dot(h_ref[...], wpq_ref[...], preferred_element_type=F32)
    kd = sk1_ref.shape[1]
    for hd in range(PEER_HEADS):
        q1 = pq[:, hd * 2 * kd:hd * 2 * kd + kd].astype(BF16)
        q2 = pq[:, hd * 2 * kd + kd:(hd + 1) * 2 * kd].astype(BF16)
        s1_ref[hd] = _dot_nt(sk1_ref[...], q1)
        s2_ref[hd] = _dot_nt(sk2_ref[...], q2)


def _peer_scores(h2, wpq_b, sk1_b, sk2_b, tm=512):
    n, d = h2.shape
    dq = wpq_b.shape[1]
    nk, kd = sk1_b.shape
    return pl.pallas_call(
        _peer_scores_kernel,
        grid=(n // tm,),
        in_specs=[pl.BlockSpec((tm, d), lambda i: (i, 0)),
                  pl.BlockSpec((d, dq), lambda i: (0, 0)),
                  pl.BlockSpec((nk, kd), lambda i: (0, 0)),
                  pl.BlockSpec((nk, kd), lambda i: (0, 0))],
        out_specs=[pl.BlockSpec((PEER_HEADS, nk, tm), lambda i: (0, 0, i))] * 2,
        out_shape=[jax.ShapeDtypeStruct((PEER_HEADS, nk, n), F32)] * 2,
        compiler_params=_cparams(("arbitrary",)),
        name="peer_scores",
    )(h2, wpq_b, sk1_b, sk2_b)


def _staircase():
    return [(a, b) for a in range(PEER_TOPK) for b in range(PEER_TOPK) if (a + 1) * (b + 1) <= PEER_TOPK]


def _extract_top(s, rows_f):
    vals = []
    rank = jnp.full(s.shape, float(PEER_TOPK), F32)
    for a in range(PEER_TOPK):
        m = jnp.max(s, axis=0, keepdims=True)
        idx = jnp.min(jnp.where(s == m, rows_f, 1e9), axis=0, keepdims=True)
        hit = rows_f == idx
        rank = jnp.where(hit, float(a), rank)
        s = jnp.where(hit, -jnp.inf, s)
        vals.append(m)
    return vals, rank


def _peer_topk_kernel(s1_ref, s2_ref, cnt_ref, e1_ref, rk_ref, e2_ref):
    nk, tn = s1_ref.shape[1], s1_ref.shape[2]
    rows_f = lax.broadcasted_iota(jnp.int32, (nk, tn), 0).astype(F32)
    pairs = _staircase()
    npad = -(-len(pairs) // SUBLANE) * SUBLANE
    prow = lax.broadcasted_iota(jnp.int32, (npad, tn), 0)
    flat_f = jnp.full((npad, tn), 1e9, F32)
    arow_f = jnp.full((npad, tn), -1.0, F32)
    for i, (a, b) in enumerate(pairs):
        flat_f = jnp.where(prow == i, float(a * PEER_TOPK + b), flat_f)
        arow_f = jnp.where(prow == i, float(a), arow_f)

    def body(hd, carry):
        s1 = s1_ref[hd]
        s2 = s2_ref[hd]
        v1, rank1 = _extract_top(s1, rows_f)
        v2, rank2 = _extract_top(s2, rows_f)
        cand = jnp.full((npad, tn), -jnp.inf, F32)
        for i, (a, b) in enumerate(pairs):
            cand = jnp.where(prow == i, v1[a] + v2[b], cand)
        m0 = v1[0] + v2[0]
        c = cand
        selected = jnp.zeros((npad, tn), F32)
        for _ in range(PEER_TOPK):
            m = jnp.max(c, axis=0, keepdims=True)
            idx = jnp.min(jnp.where(c == m, flat_f, 2e9), axis=0, keepdims=True)
            hit = flat_f == idx
            selected = jnp.where(hit, 1.0, selected)
            c = jnp.where(hit, -jnp.inf, c)
        z = jnp.sum(jnp.where(selected > 0.5, jnp.exp(cand - m0), 0.0), axis=0, keepdims=True)
        cnt1 = jnp.zeros((nk, tn), F32)
        for a in range(PEER_TOPK):
            cnt_a = jnp.sum(jnp.where(arow_f == float(a), selected, 0.0), axis=0, keepdims=True)
            cnt1 = jnp.where(rank1 == float(a), cnt_a, cnt1)
        cnt_ref[hd] = cnt1
        e1_ref[hd] = jnp.exp(s1 - v1[0]) * (1.0 / z)
        rk_ref[hd] = rank2
        e2_ref[hd] = jnp.exp(s2 - v2[0])
        return carry

    lax.fori_loop(0, PEER_HEADS, body, 0)


def _peer_topk(s1t, s2t, tn=256):
    nh, nk, n = s1t.shape
    spec = pl.BlockSpec((nh, nk, tn), lambda i: (0, 0, i))
    return pl.pallas_call(
        _peer_topk_kernel,
        grid=(n // tn,),
        in_specs=[spec, spec],
        out_specs=[spec] * 4,
        out_shape=[jax.ShapeDtypeStruct((nh, nk, n), F32)] * 4,
        compiler_params=_cparams(("arbitrary",)),
        name="peer_topk",
    )(s1t, s2t)


def _peer_dense_kernel(h_ref, eu_ref, ev_ref, cnt_ref, e1_ref, rk_ref, e2_ref, o_ref, *, te):
    e = pl.program_id(1)
    nk = rk_ref.shape[1]

    @pl.when(e == 0)
    def _():
        o_ref[...] = jnp.zeros(o_ref.shape, F32)

    act = _gelu(_dot_nt(eu_ref[...], h_ref[...]))
    n_i1 = te // nk
    ws = []
    for il in range(n_i1):
        i1 = e * n_i1 + il
        w = jnp.zeros((nk, h_ref.shape[0]), F32)
        for hd in range(PEER_HEADS):
            c = cnt_ref[hd, pl.ds(i1, 1), :]
            g1 = e1_ref[hd, pl.ds(i1, 1), :]
            w = w + jnp.where(rk_ref[hd] < c, e2_ref[hd] * g1, 0.0)
        ws.append(w)
    wa = (jnp.concatenate(ws, axis=0) * act).astype(BF16)
    o_ref[...] += _dot_tn(wa, ev_ref[...])


def _peer_dense(h2, eu_b, ev_b, cnt1, e1, rk2, e2, tm=512, te=512):
    n, d = h2.shape
    ne = eu_b.shape[0]
    nh, nk, _ = cnt1.shape
    res = pl.BlockSpec((nh, nk, tm), lambda i, e: (0, 0, i))
    return pl.pallas_call(
        functools.partial(_peer_dense_kernel, te=te),
        grid=(n // tm, ne // te),
        in_specs=[pl.BlockSpec((tm, d), lambda i, e: (i, 0)),
                  pl.BlockSpec((te, d), lambda i, e: (e, 0)),
                  pl.BlockSpec((te, d), lambda i, e: (e, 0)),
                  res, res, res, res],
        out_specs=pl.BlockSpec((tm, d), lambda i, e: (i, 0)),
        out_shape=jax.ShapeDtypeStruct((n, d), F32),
        compiler_params=_cparams(("parallel", "arbitrary")),
        name="peer_dense",
    )(h2, eu_b, ev_b, cnt1, e1, rk2, e2)


def _final_kernel(x1_ref, p_ref, gt_ref, o_ref):
    o_ref[...] = x1_ref[...] + gt_ref[...] * p_ref[...].reshape(x1_ref.shape)


def _final_residual(x1, peer, row0, mod4, bt, tt):
    nb, tb, d = x1.shape
    tpb = tb // tt
    tm = bt * tt
    n = nb * tb
    off = row0 // tm
    return pl.pallas_call(
        _final_kernel,
        grid=(n // tm,),
        in_specs=[pl.BlockSpec((bt, tt, d), lambda i: (i // tpb, i % tpb, 0)),
                  pl.BlockSpec((tm, d), lambda i: (off + i, 0)),
                  pl.BlockSpec((bt, None, 1, d), lambda i: (i // tpb, 5, 0, 0))],
        out_specs=pl.BlockSpec((bt, tt, d), lambda i: (i // tpb, i % tpb, 0)),
        out_shape=jax.ShapeDtypeStruct((nb, tb, d), F32),
        compiler_params=_cparams(("arbitrary",)),
        name="final_residual",
    )(x1, peer, mod4)


def _block_expand(n_cols, width=LANE):
    j = np.arange(width)[:, None]
    s = np.arange(n_cols)[None, :]
    return (s // SEL_BLOCK == j).astype(np.float32)


def _forward(x_prompt, x_sample, cache_k_cmp, cache_v_cmp, cache_k_sel, cache_v_sel, state_k_win, state_v_win,
             page_table, c_prompt, c_sample, rel_bias, w_ada, b_ada, g_n1, g_n2, w_in, ln_v_g, ln_v_b, w_s, b_s,
             g_q, g_k, pe_k, w_c1k, w_c2k, pe_v, w_c1v, w_c2v, w_a, w_b, w_o, w_pq, sk1, sk2, expert_u, expert_v):
    assert w_ada.shape[0] == 1, "single layer"
    bp, tp, d = x_prompt.shape
    bs, ts, _ = x_sample.shape
    np_, ns_ = bp * tp, bs * ts
    (w_ada, b_ada, g_n1, g_n2, w_in, ln_v_g, ln_v_b, w_s, b_s, g_q, g_k, pe_k, w_c1k, w_c2k, pe_v, w_c1v, w_c2v,
     w_a, w_b, w_o, w_pq, sk1, sk2, expert_u, expert_v) = [a[0] for a in (
         w_ada, b_ada, g_n1, g_n2, w_in, ln_v_g, ln_v_b, w_s, b_s, g_q, g_k, pe_k, w_c1k, w_c2k, pe_v, w_c1v, w_c2v,
         w_a, w_b, w_o, w_pq, sk1, sk2, expert_u, expert_v)]

    n_gate = 3 * N_HEADS
    c0 = T_NSA * PROJ_TILE
    w_in_p = jnp.concatenate([w_in[:, :c0],
                              jnp.pad(w_in[:, c0:c0 + n_gate], ((0, 0), (0, PROJ_TILE - n_gate))),
                              w_in[:, c0 + n_gate:]], axis=1).astype(BF16)
    ones = jnp.ones((PROJ_TILE,), F32)
    zeros = jnp.zeros((PROJ_TILE,), F32)
    rep = PROJ_TILE // HEAD_DIM
    gains = [ones] * N_TILES
    flags = [zeros] * N_TILES
    for k in range(T_Q, T_KC):
        gains[k], flags[k] = jnp.tile(g_q, rep), ones
    gains[T_KS], flags[T_KS] = jnp.tile(g_k[1], rep), ones
    gains[T_KW], flags[T_KW] = jnp.tile(g_k[2], rep), ones
    gain = jnp.stack(gains)[:, None, :]
    flag = jnp.stack(flags)[:, None, :]
    tril = jnp.tril(w_s)
    wm_p = tril.astype(BF16)
    bsb_p = jnp.broadcast_to(b_s[:, :, None], (A_GROUPS, CHUNK, CHUNK))
    nrep = CHUNK // ts
    wm_s = jnp.einsum("ab,gij->gaibj", jnp.eye(nrep, dtype=F32), tril[:, :ts, :ts]).reshape(A_GROUPS, CHUNK, CHUNK).astype(BF16)
    bsb_s = jnp.broadcast_to(jnp.tile(b_s[:, :ts], (1, nrep))[:, :, None], (A_GROUPS, CHUNK, CHUNK))
    w1k_b, w2k_b, w1v_b, w2v_b = [a.astype(BF16) for a in (w_c1k, w_c2k, w_c1v, w_c2v)]
    wa_b, wb_b, wo_b, wpq_b = [a.astype(BF16) for a in (w_a, w_b, w_o, w_pq)]
    sk1_b, sk2_b = sk1.astype(BF16), sk2.astype(BF16)
    eu_b, ev_b = expert_u.astype(BF16), expert_v.astype(BF16)
    one_gain = jnp.ones((HEAD_DIM,), F32)

    nc = bp + bs
    ncp = -(-nc // SUBLANE) * SUBLANE
    c_all = jnp.pad(jnp.concatenate([c_prompt, c_sample], axis=0), ((0, ncp - nc), (0, 0)))
    mod = _modulation(c_all, w_ada, b_ada)
    mod_p = mod[:bp].reshape(bp, 6, 1, d)
    mod_s = mod[bp:nc].reshape(bs, 6, 1, d)

    tm_p = min(1024, tp)
    bt_s = min(1024 // ts, bs)

    proj_p = _in_projection(x_prompt, mod_p, g_n1, w_in_p, gain, flag, 1, tm_p)
    proj_s = _in_projection(x_sample, mod_s, g_n1, w_in_p, gain, flag, bt_s, ts)

    cpb = tp // CHUNK
    ya_p, vch_p = _mixer_a(proj_p, ln_v_g, ln_v_b, wm_p, bsb_p, bp, lambda i: i // cpb)
    ya_s, vch_s = _mixer_a(proj_s, ln_v_g, ln_v_b, wm_s, bsb_s, ns_ // CHUNK, lambda i: i)

    width = CMP_BLOCK * KV_WIDTH
    nbp = tp // CMP_BLOCK
    proj_blk = proj_p.reshape(N_TILES, np_ // CMP_BLOCK, width)
    kcmp_p = _compress(proj_blk, lambda i: (T_KC, i, 0), bp * nbp, pe_k, w1k_b, w2k_b, g_k[0], True)
    vcmp_p = _compress(proj_blk, lambda i: (T_VC, i, 0), bp * nbp, pe_v, w1v_b, w2v_b, one_gain, False)
    n_phys, page = cache_k_cmp.shape[1], cache_k_cmp.shape[2]
    bpp = page // CMP_BLOCK
    pool_k = cache_k_cmp[0].reshape(n_phys * bpp, width)
    pool_v = cache_v_cmp[0].reshape(n_phys * bpp, width)
    kcmp_pool = _compress(pool_k, lambda i: (i, 0), n_phys * bpp, pe_k, w1k_b, w2k_b, g_k[0], True)
    vcmp_pool = _compress(pool_v, lambda i: (i, 0), n_phys * bpp, pe_v, w1v_b, w2v_b, one_gain, False)
    npg = page_table.shape[1]
    kcmp_s = kcmp_pool.reshape(n_phys, bpp * KV_WIDTH)[page_table].reshape(bs, npg * bpp, KV_WIDTH)
    vcmp_s = vcmp_pool.reshape(n_phys, bpp * KV_WIDTH)[page_table].reshape(bs, npg * bpp, KV_WIDTH)

    tables = _bias_tables(rel_bias, ATT_TILE)
    oc_p, sel_p = _cmp_select(proj_p, rel_bias, kcmp_p, vcmp_p, bp, tp)
    emat_p = jnp.asarray(_block_expand(tp).reshape(LANE, tp // ATT_TILE, ATT_TILE).transpose(1, 0, 2), BF16)
    os_p = _prompt_attention(proj_p, rel_bias, tables, bp, tp, "sel", sel_p, emat_p)
    ow_p = _prompt_attention(proj_p, rel_bias, tables, bp, tp, "win")

    past = npg * page
    emat_s = jnp.asarray(_block_expand(past + LANE), BF16)
    yb_s, kwin_s, vwin_s = _sample_attention(
        proj_s, rel_bias, page_table, kcmp_s, vcmp_s,
        cache_k_sel[0].reshape(n_phys, page, KV_WIDTH), cache_v_sel[0].reshape(n_phys, page, KV_WIDTH),
        state_k_win[0].reshape(bs, -1, KV_WIDTH), state_v_win[0].reshape(bs, -1, KV_WIDTH), emat_s, bs, ts)

    t_p = _merge(proj_p, ya_p, [oc_p, os_p, ow_p], wa_b, wb_b)
    t_s = _merge(proj_s, ya_s, [yb_s], wa_b, wb_b)
    x1_p, h2_p = _out_projection(t_p, x_prompt, mod_p, g_n2, wo_b, 1, min(256, tp))
    x1_s, h2_s = _out_projection(t_s, x_sample, mod_s, g_n2, wo_b, min(256 // ts, bs), ts)

    h2 = jnp.concatenate([h2_p, h2_s], axis=0)
    s1t, s2t = _peer_scores(h2, wpq_b, sk1_b, sk2_b)
    cnt1, e1, rk2, e2 = _peer_topk(s1t, s2t)
    peer = _peer_dense(h2, eu_b, ev_b, cnt1, e1, rk2, e2)
    y_p = _final_residual(x1_p, peer, 0, mod_p, 1, min(512, tp))
    y_s = _final_residual(x1_s, peer, np_, mod_s, min(512 // ts, bs), ts)

    def kv_p(k):
        return proj_p[k].reshape(1, bp, tp, N_KV, HEAD_DIM)

    def kv_s(k):
        return proj_s[k].reshape(1, bs, ts, N_KV, HEAD_DIM)

    wb_p = min(WINDOW, tp)
    wlen = state_k_win.shape[2]
    return (y_p, y_s,
            kv_p(T_KC), kv_p(T_VC), kv_p(T_KS), kv_p(T_VS),
            kv_p(T_KW)[:, :, tp - wb_p:], kv_p(T_VW)[:, :, tp - wb_p:],
            vch_p.reshape(1, bp, CHUNK, -1),
            kv_s(T_KC), kv_s(T_VC), kv_s(T_KS), kv_s(T_VS),
            kwin_s.reshape(1, bs, wlen, N_KV, HEAD_DIM), vwin_s.reshape(1, bs, wlen, N_KV, HEAD_DIM),
            vch_s.reshape(1, bs, ts, -1))


def kernel(x_prompt, x_sample, cache_k_cmp, cache_v_cmp, cache_k_sel, cache_v_sel, state_k_win, state_v_win, page_table, c_prompt, c_sample, rel_bias, w_ada, b_ada, g_n1, g_n2, w_in, ln_v_g, ln_v_b, w_s, b_s, g_q, g_k, pe_k, w_c1k, w_c2k, pe_v, w_c1v, w_c2v, w_a, w_b, w_o, w_pq, sk1, sk2, expert_u, expert_v):
    return _forward(x_prompt, x_sample, cache_k_cmp, cache_v_cmp, cache_k_sel, cache_v_sel, state_k_win, state_v_win,
                    page_table, c_prompt, c_sample, rel_bias, w_ada, b_ada, g_n1, g_n2, w_in, ln_v_g, ln_v_b, w_s, b_s,
                    g_q, g_k, pe_k, w_c1k, w_c2k, pe_v, w_c1v, w_c2v, w_a, w_b, w_o, w_pq, sk1, sk2, expert_u, expert_v)
```

```python
import functools
import math

import numpy as np
import jax
import jax.numpy as jnp
from jax import lax
from jax.experimental import pallas as pl
from jax.experimental.pallas import tpu as pltpu

F32 = jnp.float32
BF16 = jnp.bfloat16

N_HEADS = 16
HEAD_DIM = 128
N_KV = 4
GQA = N_HEADS // N_KV
KV_WIDTH = N_KV * HEAD_DIM
CHUNK = 128
A_GROUPS = 8
CMP_BLOCK = 64
SEL_BLOCK = 64
N_SEL = 16
WINDOW = 512
N_BUCKETS = 32
MAX_DISTANCE = 128
N_KEYS = 128
PEER_HEADS = 8
PEER_TOPK = 16
ATTN_SCALE = HEAD_DIM ** -0.5
NEG = -1e30
FORCE_BONUS = 1e4
EPS = 1e-6
LANE = 128
SUBLANE = 8
PROJ_TILE = 512
ATT_TILE = 256
VMEM_LIMIT = 56 * 1024 * 1024

T_U, T_V, T_Q, T_KC, T_VC, T_KS, T_VS, T_KW, T_VW, T_NSA, T_GA, T_GB, N_TILES = 0, 2, 4, 8, 9, 10, 11, 12, 13, 14, 15, 19, 23


def _bucket_thresholds():
    n = np.arange(0, 2 * MAX_DISTANCE)
    nf = np.maximum(n, 1).astype(np.float32)
    half = N_BUCKETS // 2
    large = half + (np.log(nf / half) / math.log(MAX_DISTANCE / half) * (N_BUCKETS - half)).astype(np.int32)
    b = np.where(n < half, n, np.minimum(large, N_BUCKETS - 1))
    assert np.all(np.diff(b) >= 0) and b[-1] == N_BUCKETS - 1
    return [int(np.argmax(b >= k)) for k in range(N_BUCKETS)]


BUCKET_THR = _bucket_thresholds()
FAR_DIST = BUCKET_THR[-1]
assert FAR_DIST <= MAX_DISTANCE


def _cparams(sem, vmem=VMEM_LIMIT):
    return pltpu.CompilerParams(dimension_semantics=sem, vmem_limit_bytes=vmem)


def _gelu(x):
    c = math.sqrt(2.0 / math.pi)
    return 0.5 * x * (1.0 + jnp.tanh(c * (x + 0.044715 * (x * x * x))))


def _sigmoid(x):
    return 1.0 / (1.0 + jnp.exp(-x))


def _dot_nt(a, b):
    return lax.dot_general(a, b, (((1,), (1,)), ((), ())), preferred_element_type=F32)


def _dot_tn(a, b):
    return lax.dot_general(a, b, (((0,), (0,)), ((), ())), preferred_element_type=F32)


def _bias_chain(dist, rbs):
    b = jnp.full(dist.shape, rbs[0], F32)
    for k in range(1, N_BUCKETS):
        b = jnp.where(dist >= BUCKET_THR[k], rbs[k], b)
    return b


def _mod_kernel(c_ref, w_ref, b_ref, o_ref):
    c = c_ref[...]
    a = (c * _sigmoid(c)).astype(BF16)
    o_ref[...] = jnp.dot(a, w_ref[...].astype(BF16), preferred_element_type=F32) + b_ref[...]


def _modulation(c_all, w_ada, b_ada):
    m, d = c_all.shape
    n = w_ada.shape[1]
    tn = 1024
    return pl.pallas_call(
        _mod_kernel,
        grid=(n // tn,),
        in_specs=[pl.BlockSpec((m, d), lambda j: (0, 0)),
                  pl.BlockSpec((d, tn), lambda j: (0, j)),
                  pl.BlockSpec((1, tn), lambda j: (0, j))],
        out_specs=pl.BlockSpec((m, tn), lambda j: (0, j)),
        out_shape=jax.ShapeDtypeStruct((m, n), F32),
        compiler_params=_cparams(("arbitrary",)),
        name="adaln_mod",
    )(c_all, w_ada, b_ada.reshape(1, n))


def _inproj_kernel(x_ref, sc_ref, sh_ref, gn_ref, w_ref, gain_ref, flag_ref, o_ref, h_scr):
    j = pl.program_id(1)

    @pl.when(j == 0)
    def _():
        x = x_ref[...]
        r = lax.rsqrt(jnp.mean(x * x, axis=-1, keepdims=True) + EPS)
        h = (x * r) * gn_ref[...] * (1.0 + sc_ref[...]) + sh_ref[...]
        h_scr[...] = h.reshape(h_scr.shape).astype(BF16)

    y = jnp.dot(h_scr[...], w_ref[...], preferred_element_type=F32)

    @pl.when(j < T_Q)
    def _():
        o_ref[...] = _gelu(y)

    @pl.when((j >= T_Q) & (j < T_NSA))
    def _():
        parts = []
        for hh in range(PROJ_TILE // HEAD_DIM):
            yh = y[:, hh * HEAD_DIM:(hh + 1) * HEAD_DIM]
            parts.append(yh * lax.rsqrt(jnp.mean(yh * yh, axis=-1, keepdims=True) + EPS))
        yn = jnp.concatenate(parts, axis=1) * gain_ref[...]
        o_ref[...] = jnp.where(flag_ref[...] > 0.5, yn, y)

    @pl.when(j >= T_NSA)
    def _():
        o_ref[...] = _sigmoid(y)


def _in_projection(x3, mod4, g_n1, w_in_p, gain, flag, bt, tt):
    nb, tb, d = x3.shape
    tpb = tb // tt
    tm = bt * tt
    n = nb * tb
    grid = (n // tm, N_TILES)
    return pl.pallas_call(
        _inproj_kernel,
        grid=grid,
        in_specs=[pl.BlockSpec((bt, tt, d), lambda i, j: (i // tpb, i % tpb, 0)),
                  pl.BlockSpec((bt, None, 1, d), lambda i, j: (i // tpb, 1, 0, 0)),
                  pl.BlockSpec((bt, None, 1, d), lambda i, j: (i // tpb, 0, 0, 0)),
                  pl.BlockSpec((1, 1, d), lambda i, j: (0, 0, 0)),
                  pl.BlockSpec((d, PROJ_TILE), lambda i, j: (0, j)),
                  pl.BlockSpec((None, 1, PROJ_TILE), lambda i, j: (j, 0, 0)),
                  pl.BlockSpec((None, 1, PROJ_TILE), lambda i, j: (j, 0, 0))],
        out_specs=pl.BlockSpec((None, tm, PROJ_TILE), lambda i, j: (j, i, 0)),
        out_shape=jax.ShapeDtypeStruct((N_TILES, n, PROJ_TILE), F32),
        scratch_shapes=[pltpu.VMEM((tm, d), BF16)],
        compiler_params=_cparams(("parallel", "arbitrary")),
        name="in_projection",
    )(x3, mod4, mod4, g_n1.reshape(1, 1, d), w_in_p, gain, flag)


def _mixa_kernel(u0_ref, u1_ref, v0_ref, v1_ref, lg_ref, lb_ref, wm_ref, bs_ref, ya_ref, vch_ref):
    v = jnp.concatenate([v0_ref[...], v1_ref[...]], axis=1)
    mu = jnp.mean(v, axis=-1, keepdims=True)
    var = jnp.mean(jnp.square(v - mu), axis=-1, keepdims=True)
    vln = ((v - mu) * lax.rsqrt(var + EPS)) * lg_ref[...] + lb_ref[...]
    vch_ref[...] = vln
    u = jnp.concatenate([u0_ref[...], u1_ref[...]], axis=1)
    vb = vln.astype(BF16)
    gd = vln.shape[1] // A_GROUPS
    for g in range(A_GROUPS):
        sl = slice(g * gd, (g + 1) * gd)
        s = jnp.dot(wm_ref[g], vb[:, sl], preferred_element_type=F32) + bs_ref[g]
        ya_ref[:, sl] = (u[:, sl] * s).astype(BF16)


def _mixer_a(proj, ln_g, ln_b, wm, bsb, vch_blocks, vch_map):
    n = proj.shape[1]
    aw = 2 * PROJ_TILE

    def tile(k):
        return pl.BlockSpec((None, CHUNK, PROJ_TILE), lambda i, k=k: (k, i, 0))

    return pl.pallas_call(
        _mixa_kernel,
        grid=(n // CHUNK,),
        in_specs=[tile(T_U), tile(T_U + 1), tile(T_V), tile(T_V + 1),
                  pl.BlockSpec((1, aw), lambda i: (0, 0)),
                  pl.BlockSpec((1, aw), lambda i: (0, 0)),
                  pl.BlockSpec((A_GROUPS, CHUNK, CHUNK), lambda i: (0, 0, 0)),
                  pl.BlockSpec((A_GROUPS, CHUNK, CHUNK), lambda i: (0, 0, 0))],
        out_specs=[pl.BlockSpec((CHUNK, aw), lambda i: (i, 0)),
                   pl.BlockSpec((CHUNK, aw), lambda i: (vch_map(i), 0))],
        out_shape=[jax.ShapeDtypeStruct((n, aw), BF16),
                   jax.ShapeDtypeStruct((vch_blocks * CHUNK, aw), F32)],
        compiler_params=_cparams(("arbitrary",)),
        name="mixer_a",
    )(proj, proj, proj, proj, ln_g.reshape(1, aw), ln_b.reshape(1, aw), wm, bsb)


def _compress_tail(hid, w2_ref, gain_ref, do_rms):
    out = jnp.dot(_gelu(hid).astype(BF16), w2_ref[...], preferred_element_type=F32)
    if do_rms:
        out = out * lax.rsqrt(jnp.mean(out * out, axis=-1, keepdims=True) + EPS) * gain_ref[...]
    return out


def _compress_prompt_kernel(x0_ref, x1_ref, x2_ref, x3_ref, pe_ref, w1_ref, w2_ref, gain_ref, o_ref, lhs_scr, *, nb, do_rms):
    for s_ in range(CMP_BLOCK):
        for g, x_ref in enumerate((x0_ref, x1_ref, x2_ref, x3_ref)):
            rows = x_ref[pl.ds(s_, nb, stride=CMP_BLOCK), :]
            lhs_scr[g * nb:(g + 1) * nb, s_ * HEAD_DIM:(s_ + 1) * HEAD_DIM] = (rows + pe_ref[s_:s_ + 1, :]).astype(BF16)
    hid = jnp.dot(lhs_scr[...], w1_ref[...], preferred_element_type=F32)
    out = _compress_tail(hid, w2_ref, gain_ref, do_rms)
    for g in range(N_KV):
        o_ref[:, g * HEAD_DIM:(g + 1) * HEAD_DIM] = out[g * nb:(g + 1) * nb]


def _compress_prompt(proj, tile, bsz, t, pe, w1b, w2b, gain, do_rms):
    nb = t // CMP_BLOCK
    hid = w1b.shape[1]
    return pl.pallas_call(
        functools.partial(_compress_prompt_kernel, nb=nb, do_rms=do_rms),
        grid=(bsz,),
        in_specs=[pl.BlockSpec((None, t, HEAD_DIM), lambda b, g=g: (tile, b, g)) for g in range(N_KV)] + [
                  pl.BlockSpec((CMP_BLOCK, HEAD_DIM), lambda b: (0, 0)),
                  pl.BlockSpec((CMP_BLOCK * HEAD_DIM, hid), lambda b: (0, 0)),
                  pl.BlockSpec((hid, HEAD_DIM), lambda b: (0, 0)),
                  pl.BlockSpec((1, HEAD_DIM), lambda b: (0, 0))],
        out_specs=pl.BlockSpec((nb, KV_WIDTH), lambda b: (b, 0)),
        out_shape=jax.ShapeDtypeStruct((bsz * nb, KV_WIDTH), F32),
        scratch_shapes=[pltpu.VMEM((N_KV * nb, CMP_BLOCK * HEAD_DIM), BF16)],
        compiler_params=_cparams(("arbitrary",)),
        name="compress_prompt",
    )(proj, proj, proj, proj, pe, w1b, w2b, gain.reshape(1, HEAD_DIM))


def _compress_pool_kernel(x_ref, pe8_ref, w1_ref, w2_ref, gain_ref, o_ref, lhs_scr, *, do_rms):
    tb = x_ref.shape[0]
    m = tb * SUBLANE
    hid = w2_ref.shape[0]
    for j in range(CMP_BLOCK // 2):
        xj = x_ref[:, SUBLANE * j:SUBLANE * (j + 1), :] + pe8_ref[j]
        lhs_scr[:, j * HEAD_DIM:(j + 1) * HEAD_DIM] = xj.reshape(m, HEAD_DIM).astype(BF16)
    acc = jnp.dot(lhs_scr[...], w1_ref[...], preferred_element_type=F32)
    hidv = acc[:, :hid] + pltpu.roll(acc[:, hid:], m - N_KV, 0)
    out = _compress_tail(hidv, w2_ref, gain_ref, do_rms)
    o_ref[...] = out.reshape(tb, SUBLANE, HEAD_DIM)


def _compress_pool(x3, pe8, w1x, w2b, gain, do_rms, tb=64):
    nblk = x3.shape[0]
    tb = min(tb, nblk)
    hid = w2b.shape[0]
    kdim = CMP_BLOCK // 2 * HEAD_DIM
    return pl.pallas_call(
        functools.partial(_compress_pool_kernel, do_rms=do_rms),
        grid=(nblk // tb,),
        in_specs=[pl.BlockSpec((tb, CMP_BLOCK * N_KV, HEAD_DIM), lambda i: (i, 0, 0)),
                  pl.BlockSpec((CMP_BLOCK // 2, SUBLANE, HEAD_DIM), lambda i: (0, 0, 0)),
                  pl.BlockSpec((kdim, 2 * hid), lambda i: (0, 0)),
                  pl.BlockSpec((hid, HEAD_DIM), lambda i: (0, 0)),
                  pl.BlockSpec((1, HEAD_DIM), lambda i: (0, 0))],
        out_specs=pl.BlockSpec((tb, SUBLANE, HEAD_DIM), lambda i: (i, 0, 0)),
        out_shape=jax.ShapeDtypeStruct((nblk, SUBLANE, HEAD_DIM), F32),
        scratch_shapes=[pltpu.VMEM((tb * SUBLANE, kdim), BF16)],
        compiler_params=_cparams(("arbitrary",)),
        name="compress_pool",
    )(x3, pe8, w1x, w2b, gain.reshape(1, HEAD_DIM))


def _bias_table_kernel(rb_ref, o_ref, *, ts):
    g = pl.program_id(0)
    i = lax.broadcasted_iota(jnp.int32, (ts, ts), 0)
    j = lax.broadcasted_iota(jnp.int32, (ts, ts), 1)
    for d in range(2):
        dist = d * ts + i - j
        for r in range(GQA):
            rbs = [rb_ref[k, g * GQA + r] for k in range(N_BUCKETS)]
            o_ref[d, r * ts:(r + 1) * ts, :] = _bias_chain(dist, rbs) - rbs[-1]


def _bias_tables(rel_bias, ts):
    return pl.pallas_call(
        functools.partial(_bias_table_kernel, ts=ts),
        grid=(N_KV,),
        in_specs=[pl.BlockSpec(memory_space=pltpu.SMEM)],
        out_specs=pl.BlockSpec((None, 2, GQA * ts, ts), lambda g: (g, 0, 0, 0)),
        out_shape=jax.ShapeDtypeStruct((N_KV, 2, GQA * ts, ts), F32),
        compiler_params=_cparams(("arbitrary",)),
        name="bias_tables",
    )(rel_bias)


def _rank_select(score, n_sel):
    jidx = lax.broadcasted_iota(jnp.int32, score.shape, 0)
    rank = jnp.zeros(score.shape, F32)
    for i in range(n_sel):
        row = score[i:i + 1, :]
        beats = (row > score) | ((row == score) & (jidx > i))
        rank = rank + beats.astype(F32)
    return ((rank < float(min(N_SEL, n_sel))) & (jidx < n_sel)).astype(F32)


def _cmp_kernel(rb_ref, q0_ref, q1_ref, q2_ref, q3_ref, nsa_ref, kc_ref, vc_ref, oc_ref, sel_ref, *, tq, nb, n_sel):
    qt = pl.program_id(1)
    nsa = nsa_ref[...]
    row = lax.broadcasted_iota(jnp.int32, (tq, LANE), 0) + qt * tq
    col = lax.broadcasted_iota(jnp.int32, (tq, LANE), 1)
    dist = row - (col * CMP_BLOCK + CMP_BLOCK - 1)
    valid = (dist >= 0) & (col < nb)
    cur = row // SEL_BLOCK
    forced = (col == 0) | (col == cur) | (col == cur - 1)
    pad = jnp.zeros((LANE - nb, HEAD_DIM), F32)
    for g in range(N_KV):
        q = (q0_ref, q1_ref, q2_ref, q3_ref)[g][...]
        qst = jnp.concatenate([q[:, r * HEAD_DIM:(r + 1) * HEAD_DIM] for r in range(GQA)], axis=0).astype(BF16)
        kg = jnp.concatenate([kc_ref[:, g * HEAD_DIM:(g + 1) * HEAD_DIM], pad], axis=0).astype(BF16)
        vg = jnp.concatenate([vc_ref[:, g * HEAD_DIM:(g + 1) * HEAD_DIM], pad], axis=0).astype(BF16)
        s = _dot_nt(qst, kg) * ATTN_SCALE
        ps = []
        imp = jnp.zeros((tq, LANE), F32)
        for r in range(GQA):
            h = g * GQA + r
            b = _bias_chain(dist, [rb_ref[k, h] for k in range(N_BUCKETS)])
            sr = jnp.where(valid, s[r * tq:(r + 1) * tq] + b, NEG)
            m = jnp.max(sr, axis=-1, keepdims=True)
            p = jnp.where(valid, jnp.exp(sr - m), 0.0)
            den = jnp.sum(p, axis=-1, keepdims=True)
            p = p * (1.0 / jnp.maximum(den, 1e-30))
            imp = imp + p
            ps.append(p)
        o = jnp.dot(jnp.concatenate(ps, axis=0).astype(BF16), vg, preferred_element_type=F32)
        for r in range(GQA):
            h = g * GQA + r
            oc_ref[:, h * HEAD_DIM:(h + 1) * HEAD_DIM] = o[r * tq:(r + 1) * tq] * nsa[:, 3 * h:3 * h + 1]
        score = jnp.where(col <= cur, imp + jnp.where(forced, FORCE_BONUS, 0.0), NEG)
        score = jnp.where(col < n_sel, score, -3e38)
        sel_ref[g] = _rank_select(score.T, n_sel).T


def _cmp_select(proj, rel_bias, kcmp, vcmp, bsz, t, tq=ATT_TILE):
    nb = kcmp.shape[0] // bsz
    n_sel = -(-t // SEL_BLOCK)
    nq = t // tq
    n = proj.shape[1]

    def tile(k):
        return pl.BlockSpec((None, tq, PROJ_TILE), lambda b, i, k=k: (k, b * nq + i, 0))

    return pl.pallas_call(
        functools.partial(_cmp_kernel, tq=tq, nb=nb, n_sel=n_sel),
        grid=(bsz, nq),
        in_specs=[pl.BlockSpec(memory_space=pltpu.SMEM),
                  tile(T_Q), tile(T_Q + 1), tile(T_Q + 2), tile(T_Q + 3), tile(T_NSA),
                  pl.BlockSpec((nb, KV_WIDTH), lambda b, i: (b, 0)),
                  pl.BlockSpec((nb, KV_WIDTH), lambda b, i: (b, 0))],
        out_specs=[pl.BlockSpec((tq, N_HEADS * HEAD_DIM), lambda b, i: (b * nq + i, 0)),
                   pl.BlockSpec((None, N_KV, tq, LANE), lambda b, i: (b, 0, i, 0))],
        out_shape=[jax.ShapeDtypeStruct((n, N_HEADS * HEAD_DIM), F32),
                   jax.ShapeDtypeStruct((bsz, N_KV, t, LANE), F32)],
        compiler_params=_cparams(("parallel", "arbitrary")),
        name="cmp_select",
    )(rel_bias, proj, proj, proj, proj, proj, kcmp, vcmp)


MASK_BIG = 2.0 ** 100
AUG = 2 * HEAD_DIM
ROW_BLOCK = 128


def _attn_kernel(rb_ref, q_ref, nsa_ref, k_ref, v_ref, tb_ref, *rest, mode, tq, branch):
    if mode == "sel":
        sel_ref, o_ref, kb, vb, qa, s_scr, p_scr, m_s, a_s, acc_s = rest
    else:
        o_ref, kb, vb, qa, s_scr, p_scr, m_s, a_s, acc_s = rest
    g = pl.program_id(1)
    qt = pl.program_id(2)
    tk = tq
    rows4 = GQA * tq
    t_all = kb.shape[0]

    @pl.when(qt == 0)
    def _():
        krow = lax.broadcasted_iota(jnp.int32, (t_all, LANE), 0)
        lane = lax.broadcasted_iota(jnp.int32, (t_all, LANE), 1)
        onehot = ((lane < SEL_BLOCK) & (krow // SEL_BLOCK == lane)) | (lane == SEL_BLOCK) | (lane == SEL_BLOCK + 1)
        kb[:, 0:HEAD_DIM] = k_ref[...].astype(BF16)
        kb[:, HEAD_DIM:AUG] = onehot.astype(BF16)
        vb[:, 0:HEAD_DIM] = v_ref[...].astype(BF16)
        vb[:, HEAD_DIM:AUG] = (lane == 0).astype(BF16)

    q = q_ref[...]
    lane_q = lax.broadcasted_iota(jnp.int32, (tq, LANE), 1)
    if mode == "sel":
        selm = jnp.where(lane_q < SEL_BLOCK, (sel_ref[...] - 1.0) * MASK_BIG, 0.0)
    else:
        selm = jnp.zeros((tq, LANE), F32)
    for r in range(GQA):
        b_far = jnp.full((tq, LANE), rb_ref[N_BUCKETS - 1, g * GQA + r], F32)
        b_hi = b_far.astype(BF16).astype(F32)
        ext = jnp.where(lane_q == SEL_BLOCK, b_hi, jnp.where(lane_q == SEL_BLOCK + 1, b_far - b_hi, selm))
        qa[r * tq:(r + 1) * tq, 0:HEAD_DIM] = (q[:, r * HEAD_DIM:(r + 1) * HEAD_DIM] * ATTN_SCALE).astype(BF16)
        qa[r * tq:(r + 1) * tq, HEAD_DIM:AUG] = ext.astype(BF16)
    m_s[...] = jnp.full(m_s.shape, NEG, F32)
    acc_s[...] = jnp.zeros(acc_s.shape, F32)
    nrb = rows4 // ROW_BLOCK
    rowpos = lax.broadcasted_iota(jnp.int32, (ROW_BLOCK, tk), 0)
    colpos = lax.broadcasted_iota(jnp.int32, (ROW_BLOCK, tk), 1)

    def chunk(kt, table, mask_kind):
        k0 = pl.multiple_of(kt * tk, tk)
        s_scr[...] = _dot_nt(qa[...], kb[pl.ds(k0, tk), :])
        for rb in range(nrb):
            rs = slice(rb * ROW_BLOCK, (rb + 1) * ROW_BLOCK)
            s = s_scr[rs, :]
            if table is not None:
                s = s + tb_ref[table, rs, :]
            if mask_kind is not None:
                rp = rowpos + (rb * ROW_BLOCK) % tq
                keep = (rp >= colpos) if mask_kind == "causal" else (colpos > rp)
                s = jnp.where(keep, s, NEG)
            m_old = m_s[rs, :]
            m_new = jnp.maximum(m_old, jnp.max(s, axis=-1, keepdims=True))
            p_scr[rs, :] = jnp.exp(s - jnp.concatenate([m_new] * (tk // LANE), axis=1)).astype(BF16)
            a_s[rs, :] = jnp.exp(m_old - m_new)
            m_s[rs, :] = m_new
        pv = jnp.dot(p_scr[...], vb[pl.ds(k0, tk), :], preferred_element_type=F32)
        a = a_s[...]
        acc_s[...] = jnp.concatenate([a] * (AUG // LANE), axis=1) * acc_s[...] + pv

    chunk(qt, 0, "causal")

    @pl.when(qt >= 1)
    def _():
        chunk(qt - 1, 1, None)

    if mode == "sel":
        def far_body(kt, carry):
            chunk(kt, None, None)
            return carry
        lax.fori_loop(0, jnp.maximum(qt - 1, 0), far_body, 0)
    else:
        @pl.when(qt >= 2)
        def _():
            chunk(qt - 2, None, "window")

    acc = acc_s[...]
    o = acc[:, 0:HEAD_DIM] * (1.0 / jnp.maximum(acc[:, HEAD_DIM:HEAD_DIM + 1], 1e-30))
    nsa = nsa_ref[...]
    lane = lax.broadcasted_iota(jnp.int32, nsa.shape, 1)
    for r in range(GQA):
        gidx = (g * GQA + r) * 3 + branch
        gate = jnp.sum(jnp.where(lane == gidx, nsa, 0.0), axis=-1, keepdims=True)
        o_ref[:, r * HEAD_DIM:(r + 1) * HEAD_DIM] = o[r * tq:(r + 1) * tq] * gate


def _prompt_attention(proj, rel_bias, tables, bsz, t, mode, sel=None, tq=ATT_TILE):
    nq = t // tq
    n = proj.shape[1]
    assert t // SEL_BLOCK <= SEL_BLOCK and tq % ROW_BLOCK == 0
    if mode == "sel":
        tk_, tv_, branch = T_KS, T_VS, 1
    else:
        tk_, tv_, branch = T_KW, T_VW, 2
        assert WINDOW == 2 * tq
    in_specs = [pl.BlockSpec(memory_space=pltpu.SMEM),
                pl.BlockSpec((None, tq, PROJ_TILE), lambda b, g, i: (T_Q + g, b * nq + i, 0)),
                pl.BlockSpec((None, tq, PROJ_TILE), lambda b, g, i: (T_NSA, b * nq + i, 0)),
                pl.BlockSpec((None, t, HEAD_DIM), lambda b, g, i: (tk_, b, g)),
                pl.BlockSpec((None, t, HEAD_DIM), lambda b, g, i: (tv_, b, g)),
                pl.BlockSpec((None, 2, GQA * tq, tq), lambda b, g, i: (g, 0, 0, 0))]
    args = [rel_bias, proj, proj, proj, proj, tables]
    if mode == "sel":
        in_specs += [pl.BlockSpec((None, None, tq, LANE), lambda b, g, i: (b, g, i, 0))]
        args += [sel]
    return pl.pallas_call(
        functools.partial(_attn_kernel, mode=mode, tq=tq, branch=branch),
        grid=(bsz, N_KV, nq),
        in_specs=in_specs,
        out_specs=pl.BlockSpec((tq, GQA * HEAD_DIM), lambda b, g, i: (b * nq + i, g)),
        out_shape=jax.ShapeDtypeStruct((n, N_HEADS * HEAD_DIM), F32),
        scratch_shapes=[pltpu.VMEM((t, AUG), BF16), pltpu.VMEM((t, AUG), BF16),
                        pltpu.VMEM((GQA * tq, AUG), BF16),
                        pltpu.VMEM((GQA * tq, tq), F32), pltpu.VMEM((GQA * tq, tq), BF16),
                        pltpu.VMEM((GQA * tq, LANE), F32), pltpu.VMEM((GQA * tq, LANE), F32),
                        pltpu.VMEM((GQA * tq, AUG), F32)],
        compiler_params=_cparams(("parallel", "parallel", "arbitrary")),
        name="attn_" + mode,
    )(*args)


def _masked_softmax(s, mask):
    s = jnp.where(mask, s, NEG)
    m = jnp.max(s, axis=-1, keepdims=True)
    p = jnp.where(mask, jnp.exp(s - m), 0.0)
    den = jnp.sum(p, axis=-1, keepdims=True)
    return p * (1.0 / jnp.maximum(den, 1e-30))


def _near_far_bias(rb_ref, g, dist_near, tdec, width, near):
    rows = []
    for r in range(GQA):
        h = g * GQA + r
        rbs = [rb_ref[k, h] for k in range(N_BUCKETS)]
        nb_ = _bias_chain(dist_near, rbs)
        rows.append(jnp.concatenate([jnp.full((tdec, width - near), rbs[-1], F32), nb_], axis=1))
    return jnp.concatenate(rows, axis=0)


def _sattn_kernel(pt_ref, rb_ref, q0_ref, q1_ref, q2_ref, q3_ref, ksn_ref, vsn_ref, kwn_ref, vwn_ref, nsa_ref,
                  kc_ref, vc_ref, *rest, npg, page, tdec, wlen):
    kpages = rest[:npg]
    vpages = rest[npg:2 * npg]
    skw_ref, svw_ref, e_ref, yb_ref, kwo_ref, vwo_ref, kbuf, vbuf, wkb, wvb = rest[2 * npg:]
    past = npg * page
    lk = past + LANE
    nb = kc_ref.shape[0]
    n_sel = -(-(past + tdec) // SEL_BLOCK)
    rows_w = wlen * N_KV
    near = 2 * LANE
    rq = GQA * tdec
    nsa = nsa_ref[...]

    kwo_ref[0:rows_w - tdec * N_KV, :] = skw_ref[tdec * N_KV:rows_w, :]
    vwo_ref[0:rows_w - tdec * N_KV, :] = svw_ref[tdec * N_KV:rows_w, :]
    for g in range(N_KV):
        kwo_ref[pl.ds(rows_w - tdec * N_KV + g, tdec, stride=N_KV), :] = kwn_ref[:, g * HEAD_DIM:(g + 1) * HEAD_DIM]
        vwo_ref[pl.ds(rows_w - tdec * N_KV + g, tdec, stride=N_KV), :] = vwn_ref[:, g * HEAD_DIM:(g + 1) * HEAD_DIM]

    trow1 = lax.broadcasted_iota(jnp.int32, (tdec, LANE), 0) + past
    col1 = lax.broadcasted_iota(jnp.int32, (tdec, LANE), 1)
    dist_c = trow1 - (col1 * CMP_BLOCK + CMP_BLOCK - 1)
    valid_c = (dist_c >= 0) & (col1 < nb)
    cur = trow1 // SEL_BLOCK
    forced = (col1 == 0) | (col1 == cur) | (col1 == cur - 1)

    trow_s = (lax.broadcasted_iota(jnp.int32, (rq, lk), 0) & (tdec - 1)) + past
    pos_s = lax.broadcasted_iota(jnp.int32, (rq, lk), 1)
    causal_s = pos_s <= trow_s
    dist_sn = (lax.broadcasted_iota(jnp.int32, (tdec, near), 0) + past) - (lax.broadcasted_iota(jnp.int32, (tdec, near), 1) + lk - near)

    wl = wlen + LANE
    qidx_w = (lax.broadcasted_iota(jnp.int32, (rq, wl), 0) & (tdec - 1)) + wlen
    kidx_w = lax.broadcasted_iota(jnp.int32, (rq, wl), 1)
    dist_w = qidx_w - kidx_w
    mask_w = (dist_w >= 0) & (dist_w < WINDOW)
    dist_wn = (lax.broadcasted_iota(jnp.int32, (tdec, near), 0) + wlen) - (lax.broadcasted_iota(jnp.int32, (tdec, near), 1) + wl - near)

    zpad = jnp.zeros((LANE - tdec, HEAD_DIM), F32)
    cpad = jnp.zeros((LANE - nb, HEAD_DIM), F32)
    for g in range(N_KV):
        gs = slice(g * HEAD_DIM, (g + 1) * HEAD_DIM)
        q = (q0_ref, q1_ref, q2_ref, q3_ref)[g][...]
        qst = jnp.concatenate([q[:, r * HEAD_DIM:(r + 1) * HEAD_DIM] for r in range(GQA)], axis=0).astype(BF16)

        kg = jnp.concatenate([kc_ref[:, g, :], cpad], axis=0).astype(BF16)
        vg = jnp.concatenate([vc_ref[:, g, :], cpad], axis=0).astype(BF16)
        s = _dot_nt(qst, kg) * ATTN_SCALE
        ps = []
        imp = jnp.zeros((tdec, LANE), F32)
        for r in range(GQA):
            h = g * GQA + r
            b = _bias_chain(dist_c, [rb_ref[k, h] for k in range(N_BUCKETS)])
            p = _masked_softmax(s[r * tdec:(r + 1) * tdec] + b, valid_c)
            imp = imp + p
            ps.append(p)
        o_c = jnp.dot(jnp.concatenate(ps, axis=0).astype(BF16), vg, preferred_element_type=F32)

        score = jnp.where(col1 <= cur, imp + jnp.where(forced, FORCE_BONUS, 0.0), NEG)
        score = jnp.where(col1 < n_sel, score, -3e38)
        rank = jnp.zeros((tdec, LANE), F32)
        for i in range(n_sel):
            ci = score[:, i:i + 1]
            rank = rank + ((ci > score) | ((ci == score) & (col1 > i))).astype(F32)
        sel = ((rank < float(min(N_SEL, n_sel))) & (col1 < n_sel)).astype(F32)

        for p_ in range(npg):
            kbuf[p_ * page:(p_ + 1) * page, :] = kpages[p_][pl.ds(g, page, stride=N_KV), :].astype(BF16)
            vbuf[p_ * page:(p_ + 1) * page, :] = vpages[p_][pl.ds(g, page, stride=N_KV), :].astype(BF16)
        kbuf[past:lk, :] = jnp.concatenate([ksn_ref[:, gs], zpad], axis=0).astype(BF16)
        vbuf[past:lk, :] = jnp.concatenate([vsn_ref[:, gs], zpad], axis=0).astype(BF16)
        s = _dot_nt(qst, kbuf[...]) * ATTN_SCALE + _near_far_bias(rb_ref, g, dist_sn, tdec, lk, near)
        sel4 = jnp.concatenate([sel] * GQA, axis=0).astype(BF16)
        mask = (jnp.dot(sel4, e_ref[...], preferred_element_type=F32) > 0.5) & causal_s
        o_s = jnp.dot(_masked_softmax(s, mask).astype(BF16), vbuf[...], preferred_element_type=F32)

        wkb[0:wlen, :] = skw_ref[pl.ds(g, wlen, stride=N_KV), :].astype(BF16)
        wvb[0:wlen, :] = svw_ref[pl.ds(g, wlen, stride=N_KV), :].astype(BF16)
        wkb[wlen:wl, :] = jnp.concatenate([kwn_ref[:, gs], zpad], axis=0).astype(BF16)
        wvb[wlen:wl, :] = jnp.concatenate([vwn_ref[:, gs], zpad], axis=0).astype(BF16)
        s = _dot_nt(qst, wkb[...]) * ATTN_SCALE + _near_far_bias(rb_ref, g, dist_wn, tdec, wl, near)
        o_w = jnp.dot(_masked_softmax(s, mask_w).astype(BF16), wvb[...], preferred_element_type=F32)

        for r in range(GQA):
            h = g * GQA + r
            rs = slice(r * tdec, (r + 1) * tdec)
            yb_ref[:, h * HEAD_DIM:(h + 1) * HEAD_DIM] = (nsa[:, 3 * h:3 * h + 1] * o_c[rs]
                                                         + nsa[:, 3 * h + 1:3 * h + 2] * o_s[rs]
                                                         + nsa[:, 3 * h + 2:3 * h + 3] * o_w[rs])


def _sample_attention(proj, rel_bias, page_table, kcmp_g, vcmp_g, ck_sel, cv_sel, skw, svw, emat, bsz, tdec):
    npg = page_table.shape[1]
    page = ck_sel.shape[1] // N_KV
    wlen = skw.shape[1] // N_KV
    nb = kcmp_g.shape[1]
    past = npg * page
    lk = past + LANE
    assert tdec == SUBLANE and FAR_DIST <= LANE and wlen == WINDOW

    def tile(k):
        return pl.BlockSpec((None, tdec, PROJ_TILE), lambda b, pt, k=k: (k, b, 0))

    def pagespec(p_):
        return pl.BlockSpec((None, page * N_KV, HEAD_DIM), lambda b, pt, p_=p_: (pt[b, p_], 0, 0))

    in_specs = ([pl.BlockSpec(memory_space=pltpu.SMEM)]
                + [tile(T_Q + g) for g in range(N_KV)]
                + [tile(T_KS), tile(T_VS), tile(T_KW), tile(T_VW), tile(T_NSA)]
                + [pl.BlockSpec((None, nb, SUBLANE, HEAD_DIM), lambda b, pt: (b, 0, 0, 0))] * 2
                + [pagespec(p_) for p_ in range(npg)] * 2
                + [pl.BlockSpec((None, wlen * N_KV, HEAD_DIM), lambda b, pt: (b, 0, 0))] * 2
                + [pl.BlockSpec((LANE, lk), lambda b, pt: (0, 0))])
    grid_spec = pltpu.PrefetchScalarGridSpec(
        num_scalar_prefetch=1,
        grid=(bsz,),
        in_specs=in_specs,
        out_specs=[pl.BlockSpec((tdec, N_HEADS * HEAD_DIM), lambda b, pt: (b, 0)),
                   pl.BlockSpec((None, wlen * N_KV, HEAD_DIM), lambda b, pt: (b, 0, 0)),
                   pl.BlockSpec((None, wlen * N_KV, HEAD_DIM), lambda b, pt: (b, 0, 0))],
        scratch_shapes=[pltpu.VMEM((lk, HEAD_DIM), BF16), pltpu.VMEM((lk, HEAD_DIM), BF16),
                        pltpu.VMEM((wlen + LANE, HEAD_DIM), BF16), pltpu.VMEM((wlen + LANE, HEAD_DIM), BF16)])
    return pl.pallas_call(
        functools.partial(_sattn_kernel, npg=npg, page=page, tdec=tdec, wlen=wlen),
        grid_spec=grid_spec,
        out_shape=[jax.ShapeDtypeStruct((bsz * tdec, N_HEADS * HEAD_DIM), F32),
                   jax.ShapeDtypeStruct((bsz, wlen * N_KV, HEAD_DIM), F32),
                   jax.ShapeDtypeStruct((bsz, wlen * N_KV, HEAD_DIM), F32)],
        compiler_params=_cparams(("arbitrary",)),
        name="sample_attention",
    )(page_table, rel_bias, *([proj] * 9), kcmp_g, vcmp_g, *([ck_sel] * npg), *([cv_sel] * npg), skw, svw, emat)


def _merge_kernel(ya_ref, *rest, n_yb):
    yb_refs = rest[:n_yb]
    wa_ref, wb_ref, ga_ref, gb_ref, t_ref, yb_scr = rest[n_yb:]

    @pl.when(pl.program_id(1) == 0)
    def _():
        yb = yb_refs[0][...]
        for r in yb_refs[1:]:
            yb = yb + r[...]
        yb_scr[...] = yb.astype(BF16)

    a = jnp.dot(ya_ref[...], wa_ref[...], preferred_element_type=F32)
    b = jnp.dot(yb_scr[...], wb_ref[...], preferred_element_type=F32)
    t_ref[...] = (ga_ref[...] * a + gb_ref[...] * b).astype(BF16)


def _merge(proj, ya, ybs, wa_b, wb_b, tm=512):
    n, aw = ya.shape
    d = wb_b.shape[0]
    nj = d // PROJ_TILE
    return pl.pallas_call(
        functools.partial(_merge_kernel, n_yb=len(ybs)),
        grid=(n // tm, nj),
        in_specs=([pl.BlockSpec((tm, aw), lambda i, j: (i, 0))]
                  + [pl.BlockSpec((tm, d), lambda i, j: (i, 0))] * len(ybs)
                  + [pl.BlockSpec((aw, PROJ_TILE), lambda i, j: (0, j)),
                     pl.BlockSpec((d, PROJ_TILE), lambda i, j: (0, j)),
                     pl.BlockSpec((None, tm, PROJ_TILE), lambda i, j: (T_GA + j, i, 0)),
                     pl.BlockSpec((None, tm, PROJ_TILE), lambda i, j: (T_GB + j, i, 0))]),
        out_specs=pl.BlockSpec((tm, PROJ_TILE), lambda i, j: (i, j)),
        out_shape=jax.ShapeDtypeStruct((n, d), BF16),
        scratch_shapes=[pltpu.VMEM((tm, d), BF16)],
        compiler_params=_cparams(("parallel", "arbitrary")),
        name="merge",
    )(ya, *ybs, wa_b, wb_b, proj, proj)


def _outproj_kernel(t_ref, x_ref, gt_ref, sc_ref, sh_ref, gn_ref, wo_ref, x1_ref, h2_ref):
    y = jnp.dot(t_ref[...], wo_ref[...], preferred_element_type=F32)
    x1 = x_ref[...] + gt_ref[...] * y.reshape(x_ref.shape)
    x1_ref[...] = x1
    r = lax.rsqrt(jnp.mean(x1 * x1, axis=-1, keepdims=True) + EPS)
    h2 = (x1 * r) * gn_ref[...] * (1.0 + sc_ref[...]) + sh_ref[...]
    h2_ref[...] = h2.reshape(h2_ref.shape).astype(BF16)


def _out_projection(tmix, x3, mod4, g_n2, wo_b, bt, tt):
    nb, tb, d = x3.shape
    tpb = tb // tt
    tm = bt * tt
    n = nb * tb

    def modspec(k):
        return pl.BlockSpec((bt, None, 1, d), lambda i, k=k: (i // tpb, k, 0, 0))

    return pl.pallas_call(
        _outproj_kernel,
        grid=(n // tm,),
        in_specs=[pl.BlockSpec((tm, d), lambda i: (i, 0)),
                  pl.BlockSpec((bt, tt, d), lambda i: (i // tpb, i % tpb, 0)),
                  modspec(2), modspec(4), modspec(3),
                  pl.BlockSpec((1, 1, d), lambda i: (0, 0, 0)),
                  pl.BlockSpec((d, d), lambda i: (0, 0))],
        out_specs=[pl.BlockSpec((bt, tt, d), lambda i: (i // tpb, i % tpb, 0)),
                   pl.BlockSpec((tm, d), lambda i: (i, 0))],
        out_shape=[jax.ShapeDtypeStruct((nb, tb, d), F32), jax.ShapeDtypeStruct((n, d), BF16)],
        compiler_params=_cparams(("arbitrary",)),
        name="out_projection",
    )(tmix, x3, mod4, mod4, mod4, g_n2.reshape(1, 1, d), wo_b)


def _peer_scores_kernel(h_ref, wpq_ref, sk1_ref, sk2_ref, s1_ref, s2_ref):
    pq = jnp.dot(h_ref[...], wpq_ref[...], preferred_element_type=F32)
    kd = sk1_ref.shape[1]
    for hd in range(PEER_HEADS):
        q1 = pq[:, hd * 2 * kd:hd * 2 * kd + kd].astype(BF16)
        q2 = pq[:, hd * 2 * kd + kd:(hd + 1) * 2 * kd].astype(BF16)
        s1_ref[hd] = _dot_nt(sk1_ref[...], q1)
        s2_ref[hd] = _dot_nt(sk2_ref[...], q2)


def _peer_scores(h2, wpq_b, sk1_b, sk2_b, tm=512):
    n, d = h2.shape
    dq = wpq_b.shape[1]
    nk, kd = sk1_b.shape
    return pl.pallas_call(
        _peer_scores_kernel,
        grid=(n // tm,),
        in_specs=[pl.BlockSpec((tm, d), lambda i: (i, 0)),
                  pl.BlockSpec((d, dq), lambda i: (0, 0)),
                  pl.BlockSpec((nk, kd), lambda i: (0, 0)),
                  pl.BlockSpec((nk, kd), lambda i: (0, 0))],
        out_specs=[pl.BlockSpec((PEER_HEADS, nk, tm), lambda i: (0, 0, i))] * 2,
        out_shape=[jax.ShapeDtypeStruct((PEER_HEADS, nk, n), F32)] * 2,
        compiler_params=_cparams(("arbitrary",)),
        name="peer_scores",
    )(h2, wpq_b, sk1_b, sk2_b)


def _staircase():
    return [(a, b) for a in range(PEER_TOPK) for b in range(PEER_TOPK) if (a + 1) * (b + 1) <= PEER_TOPK]


def _extract_top(s, rows_f):
    vals = []
    rank = jnp.full(s.shape, float(PEER_TOPK), F32)
    for a in range(PEER_TOPK):
        m = jnp.max(s, axis=0, keepdims=True)
        idx = jnp.min(jnp.where(s == m, rows_f, 1e9), axis=0, keepdims=True)
        hit = rows_f == idx
        rank = jnp.where(hit, float(a), rank)
        s = jnp.where(hit, -jnp.inf, s)
        vals.append(m)
    return vals, rank


def _peer_topk_kernel(s1_ref, s2_ref, cnt_ref, e1_ref, rk_ref, e2_ref):
    nk, tn = s1_ref.shape[1], s1_ref.shape[2]
    rows_f = lax.broadcasted_iota(jnp.int32, (nk, tn), 0).astype(F32)
    pairs = _staircase()
    npad = -(-len(pairs) // SUBLANE) * SUBLANE
    prow = lax.broadcasted_iota(jnp.int32, (npad, tn), 0)
    flat_f = jnp.full((npad, tn), 1e9, F32)
    arow_f = jnp.full((npad, tn), -1.0, F32)
    for i, (a, b) in enumerate(pairs):
        flat_f = jnp.where(prow == i, float(a * PEER_TOPK + b), flat_f)
        arow_f = jnp.where(prow == i, float(a), arow_f)

    def body(hd, carry):
        s1 = s1_ref[hd]
        s2 = s2_ref[hd]
        v1, rank1 = _extract_top(s1, rows_f)
        v2, rank2 = _extract_top(s2, rows_f)
        cand = jnp.full((npad, tn), -jnp.inf, F32)
        for i, (a, b) in enumerate(pairs):
            cand = jnp.where(prow == i, v1[a] + v2[b], cand)
        m0 = v1[0] + v2[0]
        c = cand
        selected = jnp.zeros((npad, tn), F32)
        for _ in range(PEER_TOPK):
            m = jnp.max(c, axis=0, keepdims=True)
            idx = jnp.min(jnp.where(c == m, flat_f, 2e9), axis=0, keepdims=True)
            hit = flat_f == idx
            selected = jnp.where(hit, 1.0, selected)
            c = jnp.where(hit, -jnp.inf, c)
        z = jnp.sum(jnp.where(selected > 0.5, jnp.exp(cand - m0), 0.0), axis=0, keepdims=True)
        cnt1 = jnp.zeros((nk, tn), F32)
        for a in range(PEER_TOPK):
            cnt_a = jnp.sum(jnp.where(arow_f == float(a), selected, 0.0), axis=0, keepdims=True)
            cnt1 = jnp.where(rank1 == float(a), cnt_a, cnt1)
        cnt_ref[hd] = cnt1
        e1_ref[hd] = jnp.exp(s1 - v1[0]) * (1.0 / z)
        rk_ref[hd] = rank2
        e2_ref[hd] = jnp.exp(s2 - v2[0])
        return carry

    lax.fori_loop(0, PEER_HEADS, body, 0)


def _peer_topk(s1t, s2t, tn=256):
    nh, nk, n = s1t.shape
    spec = pl.BlockSpec((nh, nk, tn), lambda i: (0, 0, i))
    return pl.pallas_call(
        _peer_topk_kernel,
        grid=(n // tn,),
        in_specs=[spec, spec],
        out_specs=[spec] * 4,
        out_shape=[jax.ShapeDtypeStruct((nh, nk, n), F32)] * 4,
        compiler_params=_cparams(("arbitrary",)),
        name="peer_topk",
    )(s1t, s2t)


PEER_SUB = 256


def _peer_dense_kernel(h_ref, eu_ref, ev_ref, cnt_ref, e1_ref, rk_ref, e2_ref, o_ref, at_scr, wa_scr, *, te):
    e = pl.program_id(1)
    nk = rk_ref.shape[1]
    tm, d = h_ref.shape

    @pl.when(e == 0)
    def _():
        o_ref[...] = jnp.zeros(o_ref.shape, F32)

    n_i1 = PEER_SUB // nk
    nsub = te // PEER_SUB
    assert te // nk == SUBLANE
    i1_base = pl.multiple_of(e * SUBLANE, SUBLANE)
    tok_piece = 2 * LANE
    n_tok = tm // tok_piece
    col_piece = 2 * LANE
    n_col = d // col_piece
    tiles = [(il, tb) for il in range(n_i1) for tb in range(tm // LANE)]

    def pre_activation(sb, k):
        ts_ = slice(k * tok_piece, (k + 1) * tok_piece)
        at_scr[sb, :, ts_] = _dot_nt(eu_ref[sb * PEER_SUB:(sb + 1) * PEER_SUB, :], h_ref[ts_, :])

    def down_projection(sb, k):
        cs_ = slice(k * col_piece, (k + 1) * col_piece)
        o_ref[:, cs_] += _dot_tn(wa_scr[sb], ev_ref[sb * PEER_SUB:(sb + 1) * PEER_SUB, cs_])

    def gate_tile(sb, il, tb):
        j1 = sb * n_i1 + il
        ks = slice(il * nk, (il + 1) * nk)
        cs = slice(tb * LANE, (tb + 1) * LANE)
        w = jnp.zeros((nk, LANE), F32)
        for hd in range(PEER_HEADS):
            c = cnt_ref[hd, pl.ds(i1_base, SUBLANE), cs][j1:j1 + 1, :]
            g1 = e1_ref[hd, pl.ds(i1_base, SUBLANE), cs][j1:j1 + 1, :]
            w = w + jnp.where(rk_ref[hd, :, cs] < c, e2_ref[hd, :, cs] * g1, 0.0)
        wa_scr[sb, ks, cs] = (w * _gelu(at_scr[sb, ks, cs])).astype(BF16)

    for k in range(n_tok):
        pre_activation(0, k)
    nslot = max(len(tiles), n_col)
    for sb in range(nsub):
        for k in range(nslot):
            if sb >= 1 and k < n_col:
                down_projection(sb - 1, k)
            if sb + 1 < nsub and k % (nslot // n_tok) == 0:
                pre_activation(sb + 1, k // (nslot // n_tok))
            if k < len(tiles):
                gate_tile(sb, *tiles[k])
    for k in range(n_col):
        down_projection(nsub - 1, k)


def _peer_dense(h2, eu_b, ev_b, cnt1, e1, rk2, e2, tm=512, te=1024):
    n, d = h2.shape
    ne = eu_b.shape[0]
    nh, nk, _ = cnt1.shape
    res = pl.BlockSpec((nh, nk, tm), lambda i, e: (0, 0, i))
    return pl.pallas_call(
        functools.partial(_peer_dense_kernel, te=te),
        grid=(n // tm, ne // te),
        in_specs=[pl.BlockSpec((tm, d), lambda i, e: (i, 0)),
                  pl.BlockSpec((te, d), lambda i, e: (e, 0)),
                  pl.BlockSpec((te, d), lambda i, e: (e, 0)),
                  res, res, res, res],
        out_specs=pl.BlockSpec((tm, d), lambda i, e: (i, 0)),
        out_shape=jax.ShapeDtypeStruct((n, d), F32),
        scratch_shapes=[pltpu.VMEM((te // PEER_SUB, PEER_SUB, tm), F32), pltpu.VMEM((te // PEER_SUB, PEER_SUB, tm), BF16)],
        compiler_params=_cparams(("parallel", "arbitrary")),
        name="peer_dense",
    )(h2, eu_b, ev_b, cnt1, e1, rk2, e2)


def _final_kernel(x1_ref, p_ref, gt_ref, o_ref):
    o_ref[...] = x1_ref[...] + gt_ref[...] * p_ref[...].reshape(x1_ref.shape)


def _final_residual(x1, peer, row0, mod4, bt, tt):
    nb, tb, d = x1.shape
    tpb = tb // tt
    tm = bt * tt
    n = nb * tb
    off = row0 // tm
    return pl.pallas_call(
        _final_kernel,
        grid=(n // tm,),
        in_specs=[pl.BlockSpec((bt, tt, d), lambda i: (i // tpb, i % tpb, 0)),
                  pl.BlockSpec((tm, d), lambda i: (off + i, 0)),
                  pl.BlockSpec((bt, None, 1, d), lambda i: (i // tpb, 5, 0, 0))],
        out_specs=pl.BlockSpec((bt, tt, d), lambda i: (i // tpb, i % tpb, 0)),
        out_shape=jax.ShapeDtypeStruct((nb, tb, d), F32),
        compiler_params=_cparams(("arbitrary",)),
        name="final_residual",
    )(x1, peer, mod4)


def _block_expand(n_cols, width=LANE):
    j = np.arange(width)[:, None]
    s = np.arange(n_cols)[None, :]
    return (s // SEL_BLOCK == j).astype(np.float32)


def _forward(x_prompt, x_sample, cache_k_cmp, cache_v_cmp, cache_k_sel, cache_v_sel, state_k_win, state_v_win,
             page_table, c_prompt, c_sample, rel_bias, w_ada, b_ada, g_n1, g_n2, w_in, ln_v_g, ln_v_b, w_s, b_s,
             g_q, g_k, pe_k, w_c1k, w_c2k, pe_v, w_c1v, w_c2v, w_a, w_b, w_o, w_pq, sk1, sk2, expert_u, expert_v):
    assert w_ada.shape[0] == 1, "single layer"
    bp, tp, d = x_prompt.shape
    bs, ts, _ = x_sample.shape
    np_, ns_ = bp * tp, bs * ts
    (w_ada, b_ada, g_n1, g_n2, w_in, ln_v_g, ln_v_b, w_s, b_s, g_q, g_k, pe_k, w_c1k, w_c2k, pe_v, w_c1v, w_c2v,
     w_a, w_b, w_o, w_pq, sk1, sk2, expert_u, expert_v) = [a[0] for a in (
         w_ada, b_ada, g_n1, g_n2, w_in, ln_v_g, ln_v_b, w_s, b_s, g_q, g_k, pe_k, w_c1k, w_c2k, pe_v, w_c1v, w_c2v,
         w_a, w_b, w_o, w_pq, sk1, sk2, expert_u, expert_v)]

    n_gate = 3 * N_HEADS
    c0 = T_NSA * PROJ_TILE
    w_in_p = jnp.concatenate([w_in[:, :c0],
                              jnp.pad(w_in[:, c0:c0 + n_gate], ((0, 0), (0, PROJ_TILE - n_gate))),
                              w_in[:, c0 + n_gate:]], axis=1).astype(BF16)
    ones = jnp.ones((PROJ_TILE,), F32)
    zeros = jnp.zeros((PROJ_TILE,), F32)
    rep = PROJ_TILE // HEAD_DIM
    gains = [ones] * N_TILES
    flags = [zeros] * N_TILES
    for k in range(T_Q, T_KC):
        gains[k], flags[k] = jnp.tile(g_q, rep), ones
    gains[T_KS], flags[T_KS] = jnp.tile(g_k[1], rep), ones
    gains[T_KW], flags[T_KW] = jnp.tile(g_k[2], rep), ones
    gain = jnp.stack(gains)[:, None, :]
    flag = jnp.stack(flags)[:, None, :]
    tril = jnp.tril(w_s)
    wm_p = tril.astype(BF16)
    bsb_p = jnp.broadcast_to(b_s[:, :, None], (A_GROUPS, CHUNK, CHUNK))
    nrep = CHUNK // ts
    wm_s = jnp.einsum("ab,gij->gaibj", jnp.eye(nrep, dtype=F32), tril[:, :ts, :ts]).reshape(A_GROUPS, CHUNK, CHUNK).astype(BF16)
    bsb_s = jnp.broadcast_to(jnp.tile(b_s[:, :ts], (1, nrep))[:, :, None], (A_GROUPS, CHUNK, CHUNK))
    w1k_b, w2k_b, w1v_b, w2v_b = [a.astype(BF16) for a in (w_c1k, w_c2k, w_c1v, w_c2v)]
    wa_b, wb_b, wo_b, wpq_b = [a.astype(BF16) for a in (w_a, w_b, w_o, w_pq)]
    sk1_b, sk2_b = sk1.astype(BF16), sk2.astype(BF16)
    eu_b, ev_b = expert_u.astype(BF16), expert_v.astype(BF16)
    one_gain = jnp.ones((HEAD_DIM,), F32)

    def pool_weights(w1, pe):
        hid = w1.shape[1]
        w1x = w1.reshape(CMP_BLOCK // 2, 2, HEAD_DIM, hid).transpose(0, 2, 1, 3).reshape(CMP_BLOCK // 2 * HEAD_DIM, 2 * hid)
        pe8 = jnp.repeat(pe.reshape(CMP_BLOCK // 2, 2, 1, HEAD_DIM), N_KV, axis=2).reshape(CMP_BLOCK // 2, SUBLANE, HEAD_DIM)
        return w1x.astype(BF16), pe8

    w1k_x, pe8_k = pool_weights(w_c1k, pe_k)
    w1v_x, pe8_v = pool_weights(w_c1v, pe_v)

    nc = bp + bs
    ncp = -(-nc // SUBLANE) * SUBLANE
    c_all = jnp.pad(jnp.concatenate([c_prompt, c_sample], axis=0), ((0, ncp - nc), (0, 0)))
    mod = _modulation(c_all, w_ada, b_ada)
    mod_p = mod[:bp].reshape(bp, 6, 1, d)
    mod_s = mod[bp:nc].reshape(bs, 6, 1, d)

    tm_p = min(1024, tp)
    bt_s = min(1024 // ts, bs)

    proj_p = _in_projection(x_prompt, mod_p, g_n1, w_in_p, gain, flag, 1, tm_p)
    proj_s = _in_projection(x_sample, mod_s, g_n1, w_in_p, gain, flag, bt_s, ts)

    cpb = tp // CHUNK
    ya_p, vch_p = _mixer_a(proj_p, ln_v_g, ln_v_b, wm_p, bsb_p, bp, lambda i: i // cpb)
    ya_s, vch_s = _mixer_a(proj_s, ln_v_g, ln_v_b, wm_s, bsb_s, ns_ // CHUNK, lambda i: i)

    kcmp_p = _compress_prompt(proj_p, T_KC, bp, tp, pe_k, w1k_b, w2k_b, g_k[0], True)
    vcmp_p = _compress_prompt(proj_p, T_VC, bp, tp, pe_v, w1v_b, w2v_b, one_gain, False)
    n_phys, page = cache_k_cmp.shape[1], cache_k_cmp.shape[2]
    bpp = page // CMP_BLOCK
    blk_rows = CMP_BLOCK * N_KV
    kcmp_pool = _compress_pool(cache_k_cmp.reshape(n_phys * bpp, blk_rows, HEAD_DIM), pe8_k, w1k_x, w2k_b, g_k[0], True)
    vcmp_pool = _compress_pool(cache_v_cmp.reshape(n_phys * bpp, blk_rows, HEAD_DIM), pe8_v, w1v_x, w2v_b, one_gain, False)
    npg = page_table.shape[1]
    kcmp_s = kcmp_pool.reshape(n_phys, bpp * SUBLANE * HEAD_DIM)[page_table].reshape(bs, npg * bpp, SUBLANE, HEAD_DIM)
    vcmp_s = vcmp_pool.reshape(n_phys, bpp * SUBLANE * HEAD_DIM)[page_table].reshape(bs, npg * bpp, SUBLANE, HEAD_DIM)

    tables = _bias_tables(rel_bias, ATT_TILE)
    oc_p, sel_p = _cmp_select(proj_p, rel_bias, kcmp_p, vcmp_p, bp, tp)
    os_p = _prompt_attention(proj_p, rel_bias, tables, bp, tp, "sel", sel_p)
    ow_p = _prompt_attention(proj_p, rel_bias, tables, bp, tp, "win")

    past = npg * page
    emat_s = jnp.asarray(_block_expand(past + LANE), BF16)
    yb_s, kwin_s, vwin_s = _sample_attention(
        proj_s, rel_bias, page_table, kcmp_s, vcmp_s,
        cache_k_sel.reshape(n_phys, page * N_KV, HEAD_DIM), cache_v_sel.reshape(n_phys, page * N_KV, HEAD_DIM),
        state_k_win.reshape(bs, -1, HEAD_DIM), state_v_win.reshape(bs, -1, HEAD_DIM), emat_s, bs, ts)

    t_p = _merge(proj_p, ya_p, [oc_p, os_p, ow_p], wa_b, wb_b)
    t_s = _merge(proj_s, ya_s, [yb_s], wa_b, wb_b)
    x1_p, h2_p = _out_projection(t_p, x_prompt, mod_p, g_n2, wo_b, 1, min(256, tp))
    x1_s, h2_s = _out_projection(t_s, x_sample, mod_s, g_n2, wo_b, min(256 // ts, bs), ts)

    h2 = jnp.concatenate([h2_p, h2_s], axis=0)
    s1t, s2t = _peer_scores(h2, wpq_b, sk1_b, sk2_b)
    cnt1, e1, rk2, e2 = _peer_topk(s1t, s2t)
    peer = _peer_dense(h2, eu_b, ev_b, cnt1, e1, rk2, e2)
    y_p = _final_residual(x1_p, peer, 0, mod_p, 1, min(512, tp))
    y_s = _final_residual(x1_s, peer, np_, mod_s, min(512 // ts, bs), ts)

    def kv_p(k):
        return proj_p[k].reshape(1, bp, tp, N_KV, HEAD_DIM)

    def kv_s(k):
        return proj_s[k].reshape(1, bs, ts, N_KV, HEAD_DIM)

    wb_p = min(WINDOW, tp)
    wlen = state_k_win.shape[2]
    return (y_p, y_s,
            kv_p(T_KC), kv_p(T_VC), kv_p(T_KS), kv_p(T_VS),
            kv_p(T_KW)[:, :, tp - wb_p:], kv_p(T_VW)[:, :, tp - wb_p:],
            vch_p.reshape(1, bp, CHUNK, -1),
            kv_s(T_KC), kv_s(T_VC), kv_s(T_KS), kv_s(T_VS),
            kwin_s.reshape(1, bs, wlen, N_KV, HEAD_DIM), vwin_s.reshape(1, bs, wlen, N_KV, HEAD_DIM),
            vch_s.reshape(1, bs, ts, -1))


def kernel(x_prompt, x_sample, cache_k_cmp, cache_v_cmp, cache_k_sel, cache_v_sel, state_k_win, state_v_win, page_table, c_prompt, c_sample, rel_bias, w_ada, b_ada, g_n1, g_n2, w_in, ln_v_g, ln_v_b, w_s, b_s, g_q, g_k, pe_k, w_c1k, w_c2k, pe_v, w_c1v, w_c2v, w_a, w_b, w_o, w_pq, sk1, sk2, expert_u, expert_v):
    return _forward(x_prompt, x_sample, cache_k_cmp, cache_v_cmp, cache_k_sel, cache_v_sel, state_k_win, state_v_win,
                    page_table, c_prompt, c_sample, rel_bias, w_ada, b_ada, g_n1, g_n2, w_in, ln_v_g, ln_v_b, w_s, b_s,
                    g_q, g_k, pe_k, w_c1k, w_c2k, pe_v, w_c1v, w_c2v, w_a, w_b, w_o, w_pq, sk1, sk2, expert_u, expert_v)
```

```python
import functools
import math

import numpy as np
import jax
import jax.numpy as jnp
from jax import lax
from jax.experimental import pallas as pl
from jax.experimental.pallas import tpu as pltpu

F32 = jnp.float32
BF16 = jnp.bfloat16

N_HEADS = 16
HEAD_DIM = 128
N_KV = 4
GQA = N_HEADS // N_KV
KV_WIDTH = N_KV * HEAD_DIM
CHUNK = 128
A_GROUPS = 8
CMP_BLOCK = 64
SEL_BLOCK = 64
N_SEL = 16
WINDOW = 512
N_BUCKETS = 32
MAX_DISTANCE = 128
N_KEYS = 128
PEER_HEADS = 8
PEER_TOPK = 16
ATTN_SCALE = HEAD_DIM ** -0.5
NEG = -1e30
FORCE_BONUS = 1e4
EPS = 1e-6
LANE = 128
SUBLANE = 8
PROJ_TILE = 512
ATT_TILE = 256
VMEM_LIMIT = 56 * 1024 * 1024

T_U, T_V, T_Q, T_KC, T_VC, T_KS, T_VS, T_KW, T_VW, T_NSA, T_GA, T_GB, N_TILES = 0, 2, 4, 8, 9, 10, 11, 12, 13, 14, 15, 19, 23


def _bucket_thresholds():
    n = np.arange(0, 2 * MAX_DISTANCE)
    nf = np.maximum(n, 1).astype(np.float32)
    half = N_BUCKETS // 2
    large = half + (np.log(nf / half) / math.log(MAX_DISTANCE / half) * (N_BUCKETS - half)).astype(np.int32)
    b = np.where(n < half, n, np.minimum(large, N_BUCKETS - 1))
    assert np.all(np.diff(b) >= 0) and b[-1] == N_BUCKETS - 1
    return [int(np.argmax(b >= k)) for k in range(N_BUCKETS)]


BUCKET_THR = _bucket_thresholds()
FAR_DIST = BUCKET_THR[-1]
assert FAR_DIST <= MAX_DISTANCE


def _cparams(sem, vmem=VMEM_LIMIT):
    return pltpu.CompilerParams(dimension_semantics=sem, vmem_limit_bytes=vmem)


def _gelu(x):
    c = 2.0 * math.sqrt(2.0 / math.pi)
    u = (x * x) * (-0.044715 * c) - c
    return x * (1.0 / (1.0 + jnp.exp(x * u)))


def _sigmoid(x):
    return 1.0 / (1.0 + jnp.exp(-x))


def _dot_nt(a, b):
    return lax.dot_general(a, b, (((1,), (1,)), ((), ())), preferred_element_type=F32)


def _dot_tn(a, b):
    return lax.dot_general(a, b, (((0,), (0,)), ((), ())), preferred_element_type=F32)


def _bias_chain(dist, rbs):
    b = jnp.full(dist.shape, rbs[0], F32)
    for k in range(1, N_BUCKETS):
        b = jnp.where(dist >= BUCKET_THR[k], rbs[k], b)
    return b


def _mod_kernel(c_ref, w_ref, b_ref, o_ref):
    c = c_ref[...]
    a = (c * _sigmoid(c)).astype(BF16)
    o_ref[...] = jnp.dot(a, w_ref[...].astype(BF16), preferred_element_type=F32) + b_ref[...]


def _modulation(c_all, w_ada, b_ada):
    m, d = c_all.shape
    n = w_ada.shape[1]
    tn = 1024
    return pl.pallas_call(
        _mod_kernel,
        grid=(n // tn,),
        in_specs=[pl.BlockSpec((m, d), lambda j: (0, 0)),
                  pl.BlockSpec((d, tn), lambda j: (0, j)),
                  pl.BlockSpec((1, tn), lambda j: (0, j))],
        out_specs=pl.BlockSpec((m, tn), lambda j: (0, j)),
        out_shape=jax.ShapeDtypeStruct((m, n), F32),
        compiler_params=_cparams(("arbitrary",)),
        name="adaln_mod",
    )(c_all, w_ada, b_ada.reshape(1, n))


def _inproj_kernel(x_ref, sc_ref, sh_ref, gn_ref, w_ref, gain_ref, flag_ref, o_ref, h_scr):
    j = pl.program_id(1)

    @pl.when(j == 0)
    def _():
        x = x_ref[...]
        r = lax.rsqrt(jnp.mean(x * x, axis=-1, keepdims=True) + EPS)
        h = (x * r) * gn_ref[...] * (1.0 + sc_ref[...]) + sh_ref[...]
        h_scr[...] = h.reshape(h_scr.shape).astype(BF16)

    y = jnp.dot(h_scr[...], w_ref[...], preferred_element_type=F32)

    @pl.when(j < T_Q)
    def _():
        o_ref[...] = _gelu(y)

    @pl.when((j >= T_Q) & (j < T_NSA))
    def _():
        parts = []
        for hh in range(PROJ_TILE // HEAD_DIM):
            yh = y[:, hh * HEAD_DIM:(hh + 1) * HEAD_DIM]
            parts.append(yh * lax.rsqrt(jnp.mean(yh * yh, axis=-1, keepdims=True) + EPS))
        yn = jnp.concatenate(parts, axis=1) * gain_ref[...]
        o_ref[...] = jnp.where(flag_ref[...] > 0.5, yn, y)

    @pl.when(j >= T_NSA)
    def _():
        o_ref[...] = _sigmoid(y)


def _in_projection(x3, mod4, g_n1, w_in_p, gain, flag, bt, tt):
    nb, tb, d = x3.shape
    tpb = tb // tt
    tm = bt * tt
    n = nb * tb
    grid = (n // tm, N_TILES)
    return pl.pallas_call(
        _inproj_kernel,
        grid=grid,
        in_specs=[pl.BlockSpec((bt, tt, d), lambda i, j: (i // tpb, i % tpb, 0)),
                  pl.BlockSpec((bt, None, 1, d), lambda i, j: (i // tpb, 1, 0, 0)),
                  pl.BlockSpec((bt, None, 1, d), lambda i, j: (i // tpb, 0, 0, 0)),
                  pl.BlockSpec((1, 1, d), lambda i, j: (0, 0, 0)),
                  pl.BlockSpec((d, PROJ_TILE), lambda i, j: (0, j)),
                  pl.BlockSpec((None, 1, PROJ_TILE), lambda i, j: (j, 0, 0)),
                  pl.BlockSpec((None, 1, PROJ_TILE), lambda i, j: (j, 0, 0))],
        out_specs=pl.BlockSpec((None, tm, PROJ_TILE), lambda i, j: (j, i, 0)),
        out_shape=jax.ShapeDtypeStruct((N_TILES, n, PROJ_TILE), F32),
        scratch_shapes=[pltpu.VMEM((tm, d), BF16)],
        compiler_params=_cparams(("parallel", "arbitrary")),
        name="in_projection",
    )(x3, mod4, mod4, g_n1.reshape(1, 1, d), w_in_p, gain, flag)


def _mixa_kernel(u0_ref, u1_ref, v0_ref, v1_ref, lg_ref, lb_ref, wm_ref, bs_ref, ya_ref, vch_ref):
    v = jnp.concatenate([v0_ref[...], v1_ref[...]], axis=1)
    mu = jnp.mean(v, axis=-1, keepdims=True)
    var = jnp.mean(jnp.square(v - mu), axis=-1, keepdims=True)
    vln = ((v - mu) * lax.rsqrt(var + EPS)) * lg_ref[...] + lb_ref[...]
    vch_ref[...] = vln
    u = jnp.concatenate([u0_ref[...], u1_ref[...]], axis=1)
    vb = vln.astype(BF16)
    gd = vln.shape[1] // A_GROUPS
    for g in range(A_GROUPS):
        sl = slice(g * gd, (g + 1) * gd)
        s = jnp.dot(wm_ref[g], vb[:, sl], preferred_element_type=F32) + bs_ref[g]
        ya_ref[:, sl] = (u[:, sl] * s).astype(BF16)


def _mixer_a(proj, ln_g, ln_b, wm, bsb, vch_blocks, vch_map):
    n = proj.shape[1]
    aw = 2 * PROJ_TILE

    def tile(k):
        return pl.BlockSpec((None, CHUNK, PROJ_TILE), lambda i, k=k: (k, i, 0))

    return pl.pallas_call(
        _mixa_kernel,
        grid=(n // CHUNK,),
        in_specs=[tile(T_U), tile(T_U + 1), tile(T_V), tile(T_V + 1),
                  pl.BlockSpec((1, aw), lambda i: (0, 0)),
                  pl.BlockSpec((1, aw), lambda i: (0, 0)),
                  pl.BlockSpec((A_GROUPS, CHUNK, CHUNK), lambda i: (0, 0, 0)),
                  pl.BlockSpec((A_GROUPS, CHUNK, CHUNK), lambda i: (0, 0, 0))],
        out_specs=[pl.BlockSpec((CHUNK, aw), lambda i: (i, 0)),
                   pl.BlockSpec((CHUNK, aw), lambda i: (vch_map(i), 0))],
        out_shape=[jax.ShapeDtypeStruct((n, aw), BF16),
                   jax.ShapeDtypeStruct((vch_blocks * CHUNK, aw), F32)],
        compiler_params=_cparams(("arbitrary",)),
        name="mixer_a",
    )(proj, proj, proj, proj, ln_g.reshape(1, aw), ln_b.reshape(1, aw), wm, bsb)


def _compress_tail(hid, w2_ref, gain_ref, do_rms):
    out = jnp.dot(_gelu(hid).astype(BF16), w2_ref[...], preferred_element_type=F32)
    if do_rms:
        out = out * lax.rsqrt(jnp.mean(out * out, axis=-1, keepdims=True) + EPS) * gain_ref[...]
    return out


def _compress_prompt_kernel(x0_ref, x1_ref, x2_ref, x3_ref, pe_ref, w1_ref, w2_ref, gain_ref, o_ref, lhs_scr, *, nb, do_rms):
    for s_ in range(CMP_BLOCK):
        for g, x_ref in enumerate((x0_ref, x1_ref, x2_ref, x3_ref)):
            rows = x_ref[pl.ds(s_, nb, stride=CMP_BLOCK), :]
            lhs_scr[g * nb:(g + 1) * nb, s_ * HEAD_DIM:(s_ + 1) * HEAD_DIM] = (rows + pe_ref[s_:s_ + 1, :]).astype(BF16)
    hid = jnp.dot(lhs_scr[...], w1_ref[...], preferred_element_type=F32)
    out = _compress_tail(hid, w2_ref, gain_ref, do_rms)
    for g in range(N_KV):
        o_ref[:, g * HEAD_DIM:(g + 1) * HEAD_DIM] = out[g * nb:(g + 1) * nb]


def _compress_prompt(proj, tile, bsz, t, pe, w1b, w2b, gain, do_rms):
    nb = t // CMP_BLOCK
    hid = w1b.shape[1]
    return pl.pallas_call(
        functools.partial(_compress_prompt_kernel, nb=nb, do_rms=do_rms),
        grid=(bsz,),
        in_specs=[pl.BlockSpec((None, t, HEAD_DIM), lambda b, g=g: (tile, b, g)) for g in range(N_KV)] + [
                  pl.BlockSpec((CMP_BLOCK, HEAD_DIM), lambda b: (0, 0)),
                  pl.BlockSpec((CMP_BLOCK * HEAD_DIM, hid), lambda b: (0, 0)),
                  pl.BlockSpec((hid, HEAD_DIM), lambda b: (0, 0)),
                  pl.BlockSpec((1, HEAD_DIM), lambda b: (0, 0))],
        out_specs=pl.BlockSpec((nb, KV_WIDTH), lambda b: (b, 0)),
        out_shape=jax.ShapeDtypeStruct((bsz * nb, KV_WIDTH), F32),
        scratch_shapes=[pltpu.VMEM((N_KV * nb, CMP_BLOCK * HEAD_DIM), BF16)],
        compiler_params=_cparams(("arbitrary",)),
        name="compress_prompt",
    )(proj, proj, proj, proj, pe, w1b, w2b, gain.reshape(1, HEAD_DIM))


def _compress_pool_kernel(x_ref, pe8_ref, w1_ref, w2_ref, gain_ref, o_ref, lhs_scr, *, do_rms):
    tb = x_ref.shape[0]
    m = tb * SUBLANE
    hid = w2_ref.shape[0]
    for j in range(CMP_BLOCK // 2):
        xj = x_ref[:, SUBLANE * j:SUBLANE * (j + 1), :] + pe8_ref[j]
        lhs_scr[:, j * HEAD_DIM:(j + 1) * HEAD_DIM] = xj.reshape(m, HEAD_DIM).astype(BF16)
    acc = jnp.dot(lhs_scr[...], w1_ref[...], preferred_element_type=F32)
    hidv = acc[:, :hid] + pltpu.roll(acc[:, hid:], m - N_KV, 0)
    out = _compress_tail(hidv, w2_ref, gain_ref, do_rms)
    o_ref[...] = out.reshape(tb, SUBLANE, HEAD_DIM)


def _compress_pool(x3, pe8, w1x, w2b, gain, do_rms, tb=64):
    nblk = x3.shape[0]
    tb = min(tb, nblk)
    hid = w2b.shape[0]
    kdim = CMP_BLOCK // 2 * HEAD_DIM
    return pl.pallas_call(
        functools.partial(_compress_pool_kernel, do_rms=do_rms),
        grid=(nblk // tb,),
        in_specs=[pl.BlockSpec((tb, CMP_BLOCK * N_KV, HEAD_DIM), lambda i: (i, 0, 0)),
                  pl.BlockSpec((CMP_BLOCK // 2, SUBLANE, HEAD_DIM), lambda i: (0, 0, 0)),
                  pl.BlockSpec((kdim, 2 * hid), lambda i: (0, 0)),
                  pl.BlockSpec((hid, HEAD_DIM), lambda i: (0, 0)),
                  pl.BlockSpec((1, HEAD_DIM), lambda i: (0, 0))],
        out_specs=pl.BlockSpec((tb, SUBLANE, HEAD_DIM), lambda i: (i, 0, 0)),
        out_shape=jax.ShapeDtypeStruct((nblk, SUBLANE, HEAD_DIM), F32),
        scratch_shapes=[pltpu.VMEM((tb * SUBLANE, kdim), BF16)],
        compiler_params=_cparams(("arbitrary",)),
        name="compress_pool",
    )(x3, pe8, w1x, w2b, gain.reshape(1, HEAD_DIM))


def _bias_table_kernel(rb_ref, o_ref, c_ref, *, ts):
    g = pl.program_id(0)
    i = lax.broadcasted_iota(jnp.int32, (ts, ts), 0)
    j = lax.broadcasted_iota(jnp.int32, (ts, ts), 1)
    for d in range(2):
        dist = d * ts + i - j
        for r in range(GQA):
            rbs = [rb_ref[k, g * GQA + r] for k in range(N_BUCKETS)]
            o_ref[d, r * ts:(r + 1) * ts, :] = _bias_chain(dist, rbs) - rbs[-1]
    ic = lax.broadcasted_iota(jnp.int32, (ts, LANE), 0)
    nc = lax.broadcasted_iota(jnp.int32, (ts, LANE), 1) - LANE // 2
    dist_c = ic - (nc * CMP_BLOCK + CMP_BLOCK - 1)
    for r in range(GQA):
        rbs = [rb_ref[k, g * GQA + r] for k in range(N_BUCKETS)]
        c_ref[r * ts:(r + 1) * ts, :] = _bias_chain(dist_c, rbs)


def _bias_tables(rel_bias, ts):
    return pl.pallas_call(
        functools.partial(_bias_table_kernel, ts=ts),
        grid=(N_KV,),
        in_specs=[pl.BlockSpec(memory_space=pltpu.SMEM)],
        out_specs=[pl.BlockSpec((None, 2, GQA * ts, ts), lambda g: (g, 0, 0, 0)),
                   pl.BlockSpec((None, GQA * ts, LANE), lambda g: (g, 0, 0))],
        out_shape=[jax.ShapeDtypeStruct((N_KV, 2, GQA * ts, ts), F32),
                   jax.ShapeDtypeStruct((N_KV, GQA * ts, LANE), F32)],
        compiler_params=_cparams(("arbitrary",)),
        name="bias_tables",
    )(rel_bias)


def _rank_select(score, n_sel):
    t = score.shape[1]
    ngrp = -(-n_sel // SUBLANE)
    jrow = lax.broadcasted_iota(jnp.int32, (SUBLANE, t), 0)
    sel = []
    for gb in range(ngrp):
        blk = score[gb * SUBLANE:(gb + 1) * SUBLANE, :]
        rank = jnp.zeros((SUBLANE, t), F32)
        for i in range(n_sel):
            row = score[i:i + 1, :]
            if i < gb * SUBLANE:
                beats = row >= blk
            elif i >= (gb + 1) * SUBLANE:
                beats = row > blk
            else:
                beats = (row > blk) | ((jrow > i - gb * SUBLANE) & (row == blk))
            rank = rank + jnp.where(beats, 1.0, 0.0)
        keep = (rank < float(min(N_SEL, n_sel))) & (jrow + gb * SUBLANE < n_sel)
        sel.append(jnp.where(keep, 1.0, 0.0))
    sel.append(jnp.zeros((score.shape[0] - ngrp * SUBLANE, t), F32))
    return jnp.concatenate(sel, axis=0)


def _cmp_kernel(ct_ref, q0_ref, q1_ref, q2_ref, q3_ref, nsa_ref, kc_ref, vc_ref, oc_ref, sel_ref, *, tq, nb, n_sel):
    qt = pl.program_id(1)
    nsa = nsa_ref[...]
    row = lax.broadcasted_iota(jnp.int32, (tq, LANE), 0) + qt * tq
    col = lax.broadcasted_iota(jnp.int32, (tq, LANE), 1)
    dist = row - (col * CMP_BLOCK + CMP_BLOCK - 1)
    valid = (dist >= 0) & (col < nb)
    cur = row // SEL_BLOCK
    forced = (col == 0) | (col == cur) | (col == cur - 1)
    shift = (qt * (tq // CMP_BLOCK) + LANE // 2) % LANE
    pad = jnp.zeros((LANE - nb, HEAD_DIM), F32)
    for g in range(N_KV):
        q = (q0_ref, q1_ref, q2_ref, q3_ref)[g][...]
        qst = jnp.concatenate([q[:, r * HEAD_DIM:(r + 1) * HEAD_DIM] for r in range(GQA)], axis=0).astype(BF16)
        kg = jnp.concatenate([kc_ref[:, g * HEAD_DIM:(g + 1) * HEAD_DIM], pad], axis=0).astype(BF16)
        vg = jnp.concatenate([vc_ref[:, g * HEAD_DIM:(g + 1) * HEAD_DIM], pad], axis=0).astype(BF16)
        s = _dot_nt(qst, kg) * ATTN_SCALE
        ps = []
        imp = jnp.zeros((tq, LANE), F32)
        for r in range(GQA):
            h = g * GQA + r
            b = pltpu.roll(ct_ref[g, r * tq:(r + 1) * tq, :], shift, 1)
            sr = jnp.where(valid, s[r * tq:(r + 1) * tq] + b, NEG)
            m = jnp.max(sr, axis=-1, keepdims=True)
            p = jnp.where(valid, jnp.exp(sr - m), 0.0)
            den = jnp.sum(p, axis=-1, keepdims=True)
            p = p * (1.0 / jnp.maximum(den, 1e-30))
            imp = imp + p
            ps.append(p)
        o = jnp.dot(jnp.concatenate(ps, axis=0).astype(BF16), vg, preferred_element_type=F32)
        for r in range(GQA):
            h = g * GQA + r
            oc_ref[:, h * HEAD_DIM:(h + 1) * HEAD_DIM] = o[r * tq:(r + 1) * tq] * nsa[:, 3 * h:3 * h + 1]
        score = jnp.where(col <= cur, imp + jnp.where(forced, FORCE_BONUS, 0.0), NEG)
        score = jnp.where(col < n_sel, score, -3e38)
        sel_ref[g] = _rank_select(score.T, n_sel).T


def _cmp_select(proj, ctab, kcmp, vcmp, bsz, t, tq=ATT_TILE):
    nb = kcmp.shape[0] // bsz
    n_sel = -(-t // SEL_BLOCK)
    nq = t // tq
    n = proj.shape[1]
    assert nb <= LANE // 2 and SEL_BLOCK == CMP_BLOCK

    def tile(k):
        return pl.BlockSpec((None, tq, PROJ_TILE), lambda b, i, k=k: (k, b * nq + i, 0))

    return pl.pallas_call(
        functools.partial(_cmp_kernel, tq=tq, nb=nb, n_sel=n_sel),
        grid=(bsz, nq),
        in_specs=[pl.BlockSpec((N_KV, GQA * tq, LANE), lambda b, i: (0, 0, 0)),
                  tile(T_Q), tile(T_Q + 1), tile(T_Q + 2), tile(T_Q + 3), tile(T_NSA),
                  pl.BlockSpec((nb, KV_WIDTH), lambda b, i: (b, 0)),
                  pl.BlockSpec((nb, KV_WIDTH), lambda b, i: (b, 0))],
        out_specs=[pl.BlockSpec((tq, N_HEADS * HEAD_DIM), lambda b, i: (b * nq + i, 0)),
                   pl.BlockSpec((None, N_KV, tq, LANE), lambda b, i: (b, 0, i, 0))],
        out_shape=[jax.ShapeDtypeStruct((n, N_HEADS * HEAD_DIM), F32),
                   jax.ShapeDtypeStruct((bsz, N_KV, t, LANE), F32)],
        compiler_params=_cparams(("parallel", "arbitrary")),
        name="cmp_select",
    )(ctab, proj, proj, proj, proj, proj, kcmp, vcmp)


MASK_BIG = 2.0 ** 100
AUG = 2 * HEAD_DIM
ROW_BLOCK = 128


def _attn_kernel(rb_ref, q_ref, nsa_ref, k_ref, v_ref, tb_ref, *rest, mode, tq, branch):
    if mode == "sel":
        sel_ref, o_ref, kb, vb, qa, s_scr, p_scr, m_s, a_s, acc_s = rest
    else:
        o_ref, kb, vb, qa, s_scr, p_scr, m_s, a_s, acc_s = rest
    g = pl.program_id(1)
    qt = pl.program_id(2)
    tk = tq
    rows4 = GQA * tq
    t_all = kb.shape[0]

    @pl.when(qt == 0)
    def _():
        krow = lax.broadcasted_iota(jnp.int32, (t_all, LANE), 0)
        lane = lax.broadcasted_iota(jnp.int32, (t_all, LANE), 1)
        onehot = ((lane < SEL_BLOCK) & (krow // SEL_BLOCK == lane)) | (lane == SEL_BLOCK) | (lane == SEL_BLOCK + 1)
        kb[:, 0:HEAD_DIM] = k_ref[...].astype(BF16)
        kb[:, HEAD_DIM:AUG] = onehot.astype(BF16)
        vb[:, 0:HEAD_DIM] = v_ref[...].astype(BF16)
        vb[:, HEAD_DIM:AUG] = (lane == 0).astype(BF16)

    q = q_ref[...]
    lane_q = lax.broadcasted_iota(jnp.int32, (tq, LANE), 1)
    if mode == "sel":
        selm = jnp.where(lane_q < SEL_BLOCK, (sel_ref[...] - 1.0) * MASK_BIG, 0.0)
    else:
        selm = jnp.zeros((tq, LANE), F32)
    for r in range(GQA):
        b_far = jnp.full((tq, LANE), rb_ref[N_BUCKETS - 1, g * GQA + r], F32)
        b_hi = b_far.astype(BF16).astype(F32)
        ext = jnp.where(lane_q == SEL_BLOCK, b_hi, jnp.where(lane_q == SEL_BLOCK + 1, b_far - b_hi, selm))
        qa[r * tq:(r + 1) * tq, 0:HEAD_DIM] = (q[:, r * HEAD_DIM:(r + 1) * HEAD_DIM] * ATTN_SCALE).astype(BF16)
        qa[r * tq:(r + 1) * tq, HEAD_DIM:AUG] = ext.astype(BF16)
    m_s[...] = jnp.full(m_s.shape, NEG, F32)
    acc_s[...] = jnp.zeros(acc_s.shape, F32)
    nrb = rows4 // ROW_BLOCK
    rowpos = lax.broadcasted_iota(jnp.int32, (ROW_BLOCK, tk), 0)
    colpos = lax.broadcasted_iota(jnp.int32, (ROW_BLOCK, tk), 1)

    def chunk(kt, table, mask_kind):
        k0 = pl.multiple_of(kt * tk, tk)
        s_scr[...] = _dot_nt(qa[...], kb[pl.ds(k0, tk), :])
        for rb in range(nrb):
            rs = slice(rb * ROW_BLOCK, (rb + 1) * ROW_BLOCK)
            s = s_scr[rs, :]
            if table is not None:
                s = s + tb_ref[table, rs, :]
            if mask_kind is not None:
                rp = rowpos + (rb * ROW_BLOCK) % tq
                keep = (rp >= colpos) if mask_kind == "causal" else (colpos > rp)
                s = jnp.where(keep, s, NEG)
            m_old = m_s[rs, :]
            m_new = jnp.maximum(m_old, jnp.max(s, axis=-1, keepdims=True))
            p_scr[rs, :] = jnp.exp(s - jnp.concatenate([m_new] * (tk // LANE), axis=1)).astype(BF16)
            a_s[rs, :] = jnp.exp(m_old - m_new)
            m_s[rs, :] = m_new
        pv = jnp.dot(p_scr[...], vb[pl.ds(k0, tk), :], preferred_element_type=F32)
        a = a_s[...]
        acc_s[...] = jnp.concatenate([a] * (AUG // LANE), axis=1) * acc_s[...] + pv

    chunk(qt, 0, "causal")

    @pl.when(qt >= 1)
    def _():
        chunk(qt - 1, 1, None)

    if mode == "sel":
        def far_body(kt, carry):
            chunk(kt, None, None)
            return carry
        lax.fori_loop(0, jnp.maximum(qt - 1, 0), far_body, 0)
    else:
        @pl.when(qt >= 2)
        def _():
            chunk(qt - 2, None, "window")

    acc = acc_s[...]
    o = acc[:, 0:HEAD_DIM] * (1.0 / jnp.maximum(acc[:, HEAD_DIM:HEAD_DIM + 1], 1e-30))
    nsa = nsa_ref[...]
    lane = lax.broadcasted_iota(jnp.int32, nsa.shape, 1)
    for r in range(GQA):
        gidx = (g * GQA + r) * 3 + branch
        gate = jnp.sum(jnp.where(lane == gidx, nsa, 0.0), axis=-1, keepdims=True)
        o_ref[:, r * HEAD_DIM:(r + 1) * HEAD_DIM] = o[r * tq:(r + 1) * tq] * gate


def _prompt_attention(proj, rel_bias, tables, bsz, t, mode, sel=None, tq=ATT_TILE):
    nq = t // tq
    n = proj.shape[1]
    assert t // SEL_BLOCK <= SEL_BLOCK and tq % ROW_BLOCK == 0
    if mode == "sel":
        tk_, tv_, branch = T_KS, T_VS, 1
    else:
        tk_, tv_, branch = T_KW, T_VW, 2
        assert WINDOW == 2 * tq
    in_specs = [pl.BlockSpec(memory_space=pltpu.SMEM),
                pl.BlockSpec((None, tq, PROJ_TILE), lambda b, g, i: (T_Q + g, b * nq + i, 0)),
                pl.BlockSpec((None, tq, PROJ_TILE), lambda b, g, i: (T_NSA, b * nq + i, 0)),
                pl.BlockSpec((None, t, HEAD_DIM), lambda b, g, i: (tk_, b, g)),
                pl.BlockSpec((None, t, HEAD_DIM), lambda b, g, i: (tv_, b, g)),
                pl.BlockSpec((None, 2, GQA * tq, tq), lambda b, g, i: (g, 0, 0, 0))]
    args = [rel_bias, proj, proj, proj, proj, tables]
    if mode == "sel":
        in_specs += [pl.BlockSpec((None, None, tq, LANE), lambda b, g, i: (b, g, i, 0))]
        args += [sel]
    return pl.pallas_call(
        functools.partial(_attn_kernel, mode=mode, tq=tq, branch=branch),
        grid=(bsz, N_KV, nq),
        in_specs=in_specs,
        out_specs=pl.BlockSpec((tq, GQA * HEAD_DIM), lambda b, g, i: (b * nq + i, g)),
        out_shape=jax.ShapeDtypeStruct((n, N_HEADS * HEAD_DIM), F32),
        scratch_shapes=[pltpu.VMEM((t, AUG), BF16), pltpu.VMEM((t, AUG), BF16),
                        pltpu.VMEM((GQA * tq, AUG), BF16),
                        pltpu.VMEM((GQA * tq, tq), F32), pltpu.VMEM((GQA * tq, tq), BF16),
                        pltpu.VMEM((GQA * tq, LANE), F32), pltpu.VMEM((GQA * tq, LANE), F32),
                        pltpu.VMEM((GQA * tq, AUG), F32)],
        compiler_params=_cparams(("parallel", "parallel", "arbitrary")),
        name="attn_" + mode,
    )(*args)


def _softmax_rows(s):
    p = jnp.exp(s - jnp.max(s, axis=-1, keepdims=True))
    return p * (1.0 / jnp.maximum(jnp.sum(p, axis=-1, keepdims=True), 1e-30))


def _sattn_kernel(pt_ref, rb_ref, q0_ref, q1_ref, q2_ref, q3_ref, ksn_ref, vsn_ref, kwn_ref, vwn_ref, nsa_ref,
                  kc_ref, vc_ref, *rest, npg, page, tdec, wlen):
    kpages = rest[:npg]
    vpages = rest[npg:2 * npg]
    skw_ref, svw_ref, yb_ref, kwo_ref, vwo_ref, kbuf, vbuf, wkb, wvb, bc_s, bs_s, bw_s = rest[2 * npg:]
    past = npg * page
    lk = past + LANE
    wl = wlen + LANE
    nb = kc_ref.shape[0]
    n_sel = -(-(past + tdec) // SEL_BLOCK)
    rows_w = wlen * N_KV
    near = 2 * LANE
    nsa = nsa_ref[...]

    trow1 = lax.broadcasted_iota(jnp.int32, (tdec, LANE), 0) + past
    col1 = lax.broadcasted_iota(jnp.int32, (tdec, LANE), 1)

    @pl.when(pl.program_id(0) == 0)
    def _():
        dist_c = trow1 - (col1 * CMP_BLOCK + CMP_BLOCK - 1)
        valid_c = (dist_c >= 0) & (col1 < nb)
        t_n = lax.broadcasted_iota(jnp.int32, (tdec, near), 0)
        c_n = lax.broadcasted_iota(jnp.int32, (tdec, near), 1)
        dist_sn = (t_n + past) - (c_n + lk - near)
        dist_wn = (t_n + wlen) - (c_n + wl - near)
        t_f = lax.broadcasted_iota(jnp.int32, (tdec, wl - near), 0)
        c_f = lax.broadcasted_iota(jnp.int32, (tdec, wl - near), 1)
        for g in range(N_KV):
            for r in range(GQA):
                rbs = [rb_ref[k, g * GQA + r] for k in range(N_BUCKETS)]
                rs = slice(r * tdec, (r + 1) * tdec)
                bc_s[g, rs, :] = jnp.where(valid_c, _bias_chain(dist_c, rbs), NEG)
                bs_s[g, rs, 0:lk - near] = jnp.full((tdec, lk - near), rbs[-1], F32)
                bs_s[g, rs, lk - near:lk] = jnp.where(dist_sn >= 0, _bias_chain(dist_sn, rbs), NEG)
                bw_s[g, rs, 0:wl - near] = jnp.where(c_f > t_f, rbs[-1], NEG)
                bw_s[g, rs, wl - near:wl] = jnp.where(dist_wn >= 0, _bias_chain(dist_wn, rbs), NEG)
        krow = lax.broadcasted_iota(jnp.int32, (lk, LANE), 0)
        klane = lax.broadcasted_iota(jnp.int32, (lk, LANE), 1)
        kbuf[:, HEAD_DIM:AUG] = (krow // SEL_BLOCK == klane).astype(BF16)

    kwo_ref[0:rows_w - tdec * N_KV, :] = skw_ref[tdec * N_KV:rows_w, :]
    vwo_ref[0:rows_w - tdec * N_KV, :] = svw_ref[tdec * N_KV:rows_w, :]
    for g in range(N_KV):
        kwo_ref[pl.ds(rows_w - tdec * N_KV + g, tdec, stride=N_KV), :] = kwn_ref[:, g * HEAD_DIM:(g + 1) * HEAD_DIM]
        vwo_ref[pl.ds(rows_w - tdec * N_KV + g, tdec, stride=N_KV), :] = vwn_ref[:, g * HEAD_DIM:(g + 1) * HEAD_DIM]

    cur = trow1 // SEL_BLOCK
    forced = (col1 == 0) | (col1 == cur) | (col1 == cur - 1)
    zpad = jnp.zeros((LANE - tdec, HEAD_DIM), F32)
    cpad = jnp.zeros((LANE - nb, HEAD_DIM), F32)
    for g in range(N_KV):
        gs = slice(g * HEAD_DIM, (g + 1) * HEAD_DIM)
        q = (q0_ref, q1_ref, q2_ref, q3_ref)[g][...] * ATTN_SCALE
        qst = jnp.concatenate([q[:, r * HEAD_DIM:(r + 1) * HEAD_DIM] for r in range(GQA)], axis=0).astype(BF16)

        kg = jnp.concatenate([kc_ref[:, g, :], cpad], axis=0).astype(BF16)
        vg = jnp.concatenate([vc_ref[:, g, :], cpad], axis=0).astype(BF16)
        p = _softmax_rows(_dot_nt(qst, kg) + bc_s[g])
        o_c = jnp.dot(p.astype(BF16), vg, preferred_element_type=F32)
        imp = p[0:tdec]
        for r in range(1, GQA):
            imp = imp + p[r * tdec:(r + 1) * tdec]

        score = jnp.where(col1 <= cur, imp + jnp.where(forced, FORCE_BONUS, 0.0), NEG)
        score = jnp.where(col1 < n_sel, score, -3e38)
        rank = jnp.zeros((tdec, LANE), F32)
        for i in range(n_sel):
            ci = score[:, i:i + 1]
            rank = rank + ((ci > score) | ((ci == score) & (col1 > i))).astype(F32)
        sel = (rank < float(min(N_SEL, n_sel))) & (col1 < n_sel)

        for p_ in range(npg):
            kbuf[p_ * page:(p_ + 1) * page, 0:HEAD_DIM] = kpages[p_][pl.ds(g, page, stride=N_KV), :].astype(BF16)
            vbuf[p_ * page:(p_ + 1) * page, :] = vpages[p_][pl.ds(g, page, stride=N_KV), :].astype(BF16)
        kbuf[past:lk, 0:HEAD_DIM] = jnp.concatenate([ksn_ref[:, gs], zpad], axis=0).astype(BF16)
        vbuf[past:lk, :] = jnp.concatenate([vsn_ref[:, gs], zpad], axis=0).astype(BF16)
        selm = jnp.where(sel, 0.0, -MASK_BIG).astype(BF16)
        qa = jnp.concatenate([qst, jnp.concatenate([selm] * GQA, axis=0)], axis=1)
        p = _softmax_rows(_dot_nt(qa, kbuf[...]) + bs_s[g])
        o_s = jnp.dot(p.astype(BF16), vbuf[...], preferred_element_type=F32)

        wkb[0:wlen, :] = skw_ref[pl.ds(g, wlen, stride=N_KV), :].astype(BF16)
        wvb[0:wlen, :] = svw_ref[pl.ds(g, wlen, stride=N_KV), :].astype(BF16)
        wkb[wlen:wl, :] = jnp.concatenate([kwn_ref[:, gs], zpad], axis=0).astype(BF16)
        wvb[wlen:wl, :] = jnp.concatenate([vwn_ref[:, gs], zpad], axis=0).astype(BF16)
        p = _softmax_rows(_dot_nt(qst, wkb[...]) + bw_s[g])
        o_w = jnp.dot(p.astype(BF16), wvb[...], preferred_element_type=F32)

        for r in range(GQA):
            h = g * GQA + r
            rs = slice(r * tdec, (r + 1) * tdec)
            yb_ref[:, h * HEAD_DIM:(h + 1) * HEAD_DIM] = (nsa[:, 3 * h:3 * h + 1] * o_c[rs]
                                                         + nsa[:, 3 * h + 1:3 * h + 2] * o_s[rs]
                                                         + nsa[:, 3 * h + 2:3 * h + 3] * o_w[rs])


def _sample_attention(proj, rel_bias, page_table, kcmp_g, vcmp_g, ck_sel, cv_sel, skw, svw, bsz, tdec):
    npg = page_table.shape[1]
    page = ck_sel.shape[1] // N_KV
    wlen = skw.shape[1] // N_KV
    nb = kcmp_g.shape[1]
    past = npg * page
    lk = past + LANE
    assert tdec == SUBLANE and FAR_DIST <= LANE and wlen == WINDOW and past >= CMP_BLOCK
    assert -(-(past + tdec) // SEL_BLOCK) <= LANE
    rq = GQA * tdec

    def tile(k):
        return pl.BlockSpec((None, tdec, PROJ_TILE), lambda b, pt, k=k: (k, b, 0))

    def pagespec(p_):
        return pl.BlockSpec((None, page * N_KV, HEAD_DIM), lambda b, pt, p_=p_: (pt[b, p_], 0, 0))

    in_specs = ([pl.BlockSpec(memory_space=pltpu.SMEM)]
                + [tile(T_Q + g) for g in range(N_KV)]
                + [tile(T_KS), tile(T_VS), tile(T_KW), tile(T_VW), tile(T_NSA)]
                + [pl.BlockSpec((None, nb, SUBLANE, HEAD_DIM), lambda b, pt: (b, 0, 0, 0))] * 2
                + [pagespec(p_) for p_ in range(npg)] * 2
                + [pl.BlockSpec((None, wlen * N_KV, HEAD_DIM), lambda b, pt: (b, 0, 0))] * 2)
    grid_spec = pltpu.PrefetchScalarGridSpec(
        num_scalar_prefetch=1,
        grid=(bsz,),
        in_specs=in_specs,
        out_specs=[pl.BlockSpec((tdec, N_HEADS * HEAD_DIM), lambda b, pt: (b, 0)),
                   pl.BlockSpec((None, wlen * N_KV, HEAD_DIM), lambda b, pt: (b, 0, 0)),
                   pl.BlockSpec((None, wlen * N_KV, HEAD_DIM), lambda b, pt: (b, 0, 0))],
        scratch_shapes=[pltpu.VMEM((lk, AUG), BF16), pltpu.VMEM((lk, HEAD_DIM), BF16),
                        pltpu.VMEM((wlen + LANE, HEAD_DIM), BF16), pltpu.VMEM((wlen + LANE, HEAD_DIM), BF16),
                        pltpu.VMEM((N_KV, rq, LANE), F32), pltpu.VMEM((N_KV, rq, lk), F32),
                        pltpu.VMEM((N_KV, rq, wlen + LANE), F32)])
    return pl.pallas_call(
        functools.partial(_sattn_kernel, npg=npg, page=page, tdec=tdec, wlen=wlen),
        grid_spec=grid_spec,
        out_shape=[jax.ShapeDtypeStruct((bsz * tdec, N_HEADS * HEAD_DIM), F32),
                   jax.ShapeDtypeStruct((bsz, wlen * N_KV, HEAD_DIM), F32),
                   jax.ShapeDtypeStruct((bsz, wlen * N_KV, HEAD_DIM), F32)],
        compiler_params=_cparams(("arbitrary",)),
        name="sample_attention",
    )(page_table, rel_bias, *([proj] * 9), kcmp_g, vcmp_g, *([ck_sel] * npg), *([cv_sel] * npg), skw, svw)


def _merge_kernel(ya_ref, *rest, n_yb):
    yb_refs = rest[:n_yb]
    wa_ref, wb_ref, ga_ref, gb_ref, t_ref, yb_scr = rest[n_yb:]

    @pl.when(pl.program_id(1) == 0)
    def _():
        yb = yb_refs[0][...]
        for r in yb_refs[1:]:
            yb = yb + r[...]
        yb_scr[...] = yb.astype(BF16)

    a = jnp.dot(ya_ref[...], wa_ref[...], preferred_element_type=F32)
    b = jnp.dot(yb_scr[...], wb_ref[...], preferred_element_type=F32)
    t_ref[...] = (ga_ref[...] * a + gb_ref[...] * b).astype(BF16)


def _merge(proj, ya, ybs, wa_b, wb_b, tm=512):
    n, aw = ya.shape
    d = wb_b.shape[0]
    nj = d // PROJ_TILE
    return pl.pallas_call(
        functools.partial(_merge_kernel, n_yb=len(ybs)),
        grid=(n // tm, nj),
        in_specs=([pl.BlockSpec((tm, aw), lambda i, j: (i, 0))]
                  + [pl.BlockSpec((tm, d), lambda i, j: (i, 0))] * len(ybs)
                  + [pl.BlockSpec((aw, PROJ_TILE), lambda i, j: (0, j)),
                     pl.BlockSpec((d, PROJ_TILE), lambda i, j: (0, j)),
                     pl.BlockSpec((None, tm, PROJ_TILE), lambda i, j: (T_GA + j, i, 0)),
                     pl.BlockSpec((None, tm, PROJ_TILE), lambda i, j: (T_GB + j, i, 0))]),
        out_specs=pl.BlockSpec((tm, PROJ_TILE), lambda i, j: (i, j)),
        out_shape=jax.ShapeDtypeStruct((n, d), BF16),
        scratch_shapes=[pltpu.VMEM((tm, d), BF16)],
        compiler_params=_cparams(("parallel", "arbitrary")),
        name="merge",
    )(ya, *ybs, wa_b, wb_b, proj, proj)


def _outproj_kernel(t_ref, x_ref, gt_ref, sc_ref, sh_ref, gn_ref, wo_ref, x1_ref, h2_ref):
    y = jnp.dot(t_ref[...], wo_ref[...], preferred_element_type=F32)
    x1 = x_ref[...] + gt_ref[...] * y.reshape(x_ref.shape)
    x1_ref[...] = x1
    r = lax.rsqrt(jnp.mean(x1 * x1, axis=-1, keepdims=True) + EPS)
    h2 = (x1 * r) * gn_ref[...] * (1.0 + sc_ref[...]) + sh_ref[...]
    h2_ref[...] = h2.reshape(h2_ref.shape).astype(BF16)


def _out_projection(tmix, x3, mod4, g_n2, wo_b, bt, tt):
    nb, tb, d = x3.shape
    tpb = tb // tt
    tm = bt * tt
    n = nb * tb

    def modspec(k):
        return pl.BlockSpec((bt, None, 1, d), lambda i, k=k: (i // tpb, k, 0, 0))

    return pl.pallas_call(
        _outproj_kernel,
        grid=(n // tm,),
        in_specs=[pl.BlockSpec((tm, d), lambda i: (i, 0)),
                  pl.BlockSpec((bt, tt, d), lambda i: (i // tpb, i % tpb, 0)),
                  modspec(2), modspec(4), modspec(3),
                  pl.BlockSpec((1, 1, d), lambda i: (0, 0, 0)),
                  pl.BlockSpec((d, d), lambda i: (0, 0))],
        out_specs=[pl.BlockSpec((bt, tt, d), lambda i: (i // tpb, i % tpb, 0)),
                   pl.BlockSpec((tm, d), lambda i: (i, 0))],
        out_shape=[jax.ShapeDtypeStruct((nb, tb, d), F32), jax.ShapeDtypeStruct((n, d), BF16)],
        compiler_params=_cparams(("arbitrary",)),
        name="out_projection",
    )(tmix, x3, mod4, mod4, mod4, g_n2.reshape(1, 1, d), wo_b)


def _peer_scores_kernel(h_ref, wpq_ref, sk1_ref, sk2_ref, s1_ref, s2_ref):
    pq = jnp.dot(h_ref[...], wpq_ref[...], preferred_element_type=F32)
    kd = sk1_ref.shape[1]
    for hd in range(PEER_HEADS):
        q1 = pq[:, hd * 2 * kd:hd * 2 * kd + kd].astype(BF16)
        q2 = pq[:, hd * 2 * kd + kd:(hd + 1) * 2 * kd].astype(BF16)
        s1_ref[hd] = _dot_nt(sk1_ref[...], q1)
        s2_ref[hd] = _dot_nt(sk2_ref[...], q2)


def _peer_scores(h2, wpq_b, sk1_b, sk2_b, tm=512):
    n, d = h2.shape
    dq = wpq_b.shape[1]
    nk, kd = sk1_b.shape
    return pl.pallas_call(
        _peer_scores_kernel,
        grid=(n // tm,),
        in_specs=[pl.BlockSpec((tm, d), lambda i: (i, 0)),
                  pl.BlockSpec((d, dq), lambda i: (0, 0)),
                  pl.BlockSpec((nk, kd), lambda i: (0, 0)),
                  pl.BlockSpec((nk, kd), lambda i: (0, 0))],
        out_specs=[pl.BlockSpec((PEER_HEADS, nk, tm), lambda i: (0, 0, i))] * 2,
        out_shape=[jax.ShapeDtypeStruct((PEER_HEADS, nk, n), F32)] * 2,
        compiler_params=_cparams(("arbitrary",)),
        name="peer_scores",
    )(h2, wpq_b, sk1_b, sk2_b)


def _staircase():
    return [(a, b) for a in range(PEER_TOPK) for b in range(PEER_TOPK) if (a + 1) * (b + 1) <= PEER_TOPK]


def _extract_top(s, rows_f):
    vals = []
    rank = jnp.full(s.shape, float(PEER_TOPK), F32)
    for a in range(PEER_TOPK):
        m = jnp.max(s, axis=0, keepdims=True)
        idx = jnp.min(jnp.where(s == m, rows_f, 1e9), axis=0, keepdims=True)
        hit = rows_f == idx
        rank = jnp.where(hit, float(a), rank)
        s = jnp.where(hit, -jnp.inf, s)
        vals.append(m)
    return vals, rank


def _peer_topk_kernel(s1_ref, s2_ref, cnt_ref, e1_ref, rk_ref, e2_ref):
    nk, tn = s1_ref.shape[1], s1_ref.shape[2]
    rows_f = lax.broadcasted_iota(jnp.int32, (nk, tn), 0).astype(F32)
    pairs = _staircase()
    npad = -(-len(pairs) // SUBLANE) * SUBLANE
    prow = lax.broadcasted_iota(jnp.int32, (npad, tn), 0)
    flat_f = jnp.full((npad, tn), 1e9, F32)
    arow_f = jnp.full((npad, tn), -1.0, F32)
    for i, (a, b) in enumerate(pairs):
        flat_f = jnp.where(prow == i, float(a * PEER_TOPK + b), flat_f)
        arow_f = jnp.where(prow == i, float(a), arow_f)

    def body(hd, carry):
        s1 = s1_ref[hd]
        s2 = s2_ref[hd]
        v1, rank1 = _extract_top(s1, rows_f)
        v2, rank2 = _extract_top(s2, rows_f)
        cand = jnp.full((npad, tn), -jnp.inf, F32)
        for i, (a, b) in enumerate(pairs):
            cand = jnp.where(prow == i, v1[a] + v2[b], cand)
        m0 = v1[0] + v2[0]
        c = cand
        selected = jnp.zeros((npad, tn), F32)
        for _ in range(PEER_TOPK):
            m = jnp.max(c, axis=0, keepdims=True)
            idx = jnp.min(jnp.where(c == m, flat_f, 2e9), axis=0, keepdims=True)
            hit = flat_f == idx
            selected = jnp.where(hit, 1.0, selected)
            c = jnp.where(hit, -jnp.inf, c)
        z = jnp.sum(jnp.where(selected > 0.5, jnp.exp(cand - m0), 0.0), axis=0, keepdims=True)
        cnt1 = jnp.zeros((nk, tn), F32)
        for a in range(PEER_TOPK):
            cnt_a = jnp.sum(jnp.where(arow_f == float(a), selected, 0.0), axis=0, keepdims=True)
            cnt1 = jnp.where(rank1 == float(a), cnt_a, cnt1)
        cnt_ref[hd] = cnt1
        e1_ref[hd] = jnp.exp(s1 - v1[0]) * (1.0 / z)
        rk_ref[hd] = rank2
        e2_ref[hd] = jnp.exp(s2 - v2[0])
        return carry

    lax.fori_loop(0, PEER_HEADS, body, 0)


def _peer_topk(s1t, s2t, tn=256):
    nh, nk, n = s1t.shape
    spec = pl.BlockSpec((nh, nk, tn), lambda i: (0, 0, i))
    return pl.pallas_call(
        _peer_topk_kernel,
        grid=(n // tn,),
        in_specs=[spec, spec],
        out_specs=[spec] * 4,
        out_shape=[jax.ShapeDtypeStruct((nh, nk, n), F32)] * 4,
        compiler_params=_cparams(("arbitrary",)),
        name="peer_topk",
    )(s1t, s2t)


PEER_SUB = 256


def _peer_dense_kernel(h_ref, eu_ref, ev_ref, cnt_ref, e1_ref, rk_ref, e2_ref, o_ref, at_scr, wa_scr, *, te):
    e = pl.program_id(1)
    nk = rk_ref.shape[1]
    tm, d = h_ref.shape

    @pl.when(e == 0)
    def _():
        o_ref[...] = jnp.zeros(o_ref.shape, F32)

    n_i1 = PEER_SUB // nk
    nsub = te // PEER_SUB
    assert te // nk == SUBLANE
    i1_base = pl.multiple_of(e * SUBLANE, SUBLANE)
    tok_piece = 2 * LANE
    n_tok = tm // tok_piece
    col_piece = 2 * LANE
    n_col = d // col_piece
    tiles = [(il, tb) for il in range(n_i1) for tb in range(tm // LANE)]

    def pre_activation(sb, k):
        ts_ = slice(k * tok_piece, (k + 1) * tok_piece)
        at_scr[sb, :, ts_] = _dot_nt(eu_ref[sb * PEER_SUB:(sb + 1) * PEER_SUB, :], h_ref[ts_, :])

    def down_projection(sb, k):
        cs_ = slice(k * col_piece, (k + 1) * col_piece)
        o_ref[:, cs_] += _dot_tn(wa_scr[sb], ev_ref[sb * PEER_SUB:(sb + 1) * PEER_SUB, cs_])

    def gate_tile(sb, il, tb):
        j1 = sb * n_i1 + il
        ks = slice(il * nk, (il + 1) * nk)
        cs = slice(tb * LANE, (tb + 1) * LANE)
        w = jnp.zeros((nk, LANE), F32)
        for hd in range(PEER_HEADS):
            c = cnt_ref[hd, pl.ds(i1_base, SUBLANE), cs][j1:j1 + 1, :]
            g1 = e1_ref[hd, pl.ds(i1_base, SUBLANE), cs][j1:j1 + 1, :]
            w = w + jnp.where(rk_ref[hd, :, cs] < c, e2_ref[hd, :, cs] * g1, 0.0)
        wa_scr[sb, ks, cs] = (w * _gelu(at_scr[sb, ks, cs])).astype(BF16)

    for k in range(n_tok):
        pre_activation(0, k)
    nslot = max(len(tiles), n_col)
    for sb in range(nsub):
        for k in range(nslot):
            if sb >= 1 and k < n_col:
                down_projection(sb - 1, k)
            if sb + 1 < nsub and k % (nslot // n_tok) == 0:
                pre_activation(sb + 1, k // (nslot // n_tok))
            if k < len(tiles):
                gate_tile(sb, *tiles[k])
    for k in range(n_col):
        down_projection(nsub - 1, k)


def _peer_dense(h2, eu_b, ev_b, cnt1, e1, rk2, e2, tm=512, te=1024):
    n, d = h2.shape
    ne = eu_b.shape[0]
    nh, nk, _ = cnt1.shape
    res = pl.BlockSpec((nh, nk, tm), lambda i, e: (0, 0, i))
    return pl.pallas_call(
        functools.partial(_peer_dense_kernel, te=te),
        grid=(n // tm, ne // te),
        in_specs=[pl.BlockSpec((tm, d), lambda i, e: (i, 0)),
                  pl.BlockSpec((te, d), lambda i, e: (e, 0)),
                  pl.BlockSpec((te, d), lambda i, e: (e, 0)),
                  res, res, res, res],
        out_specs=pl.BlockSpec((tm, d), lambda i, e: (i, 0)),
        out_shape=jax.ShapeDtypeStruct((n, d), F32),
        scratch_shapes=[pltpu.VMEM((te // PEER_SUB, PEER_SUB, tm), F32), pltpu.VMEM((te // PEER_SUB, PEER_SUB, tm), BF16)],
        compiler_params=_cparams(("parallel", "arbitrary")),
        name="peer_dense",
    )(h2, eu_b, ev_b, cnt1, e1, rk2, e2)


def _final_kernel(x1_ref, p_ref, gt_ref, o_ref):
    o_ref[...] = x1_ref[...] + gt_ref[...] * p_ref[...].reshape(x1_ref.shape)


def _final_residual(x1, peer, row0, mod4, bt, tt):
    nb, tb, d = x1.shape
    tpb = tb // tt
    tm = bt * tt
    n = nb * tb
    off = row0 // tm
    return pl.pallas_call(
        _final_kernel,
        grid=(n // tm,),
        in_specs=[pl.BlockSpec((bt, tt, d), lambda i: (i // tpb, i % tpb, 0)),
                  pl.BlockSpec((tm, d), lambda i: (off + i, 0)),
                  pl.BlockSpec((bt, None, 1, d), lambda i: (i // tpb, 5, 0, 0))],
        out_specs=pl.BlockSpec((bt, tt, d), lambda i: (i // tpb, i % tpb, 0)),
        out_shape=jax.ShapeDtypeStruct((nb, tb, d), F32),
        compiler_params=_cparams(("arbitrary",)),
        name="final_residual",
    )(x1, peer, mod4)


def _forward(x_prompt, x_sample, cache_k_cmp, cache_v_cmp, cache_k_sel, cache_v_sel, state_k_win, state_v_win,
             page_table, c_prompt, c_sample, rel_bias, w_ada, b_ada, g_n1, g_n2, w_in, ln_v_g, ln_v_b, w_s, b_s,
             g_q, g_k, pe_k, w_c1k, w_c2k, pe_v, w_c1v, w_c2v, w_a, w_b, w_o, w_pq, sk1, sk2, expert_u, expert_v):
    assert w_ada.shape[0] == 1, "single layer"
    bp, tp, d = x_prompt.shape
    bs, ts, _ = x_sample.shape
    np_, ns_ = bp * tp, bs * ts
    (w_ada, b_ada, g_n1, g_n2, w_in, ln_v_g, ln_v_b, w_s, b_s, g_q, g_k, pe_k, w_c1k, w_c2k, pe_v, w_c1v, w_c2v,
     w_a, w_b, w_o, w_pq, sk1, sk2, expert_u, expert_v) = [a[0] for a in (
         w_ada, b_ada, g_n1, g_n2, w_in, ln_v_g, ln_v_b, w_s, b_s, g_q, g_k, pe_k, w_c1k, w_c2k, pe_v, w_c1v, w_c2v,
         w_a, w_b, w_o, w_pq, sk1, sk2, expert_u, expert_v)]

    n_gate = 3 * N_HEADS
    c0 = T_NSA * PROJ_TILE
    w_in_p = jnp.concatenate([w_in[:, :c0],
                              jnp.pad(w_in[:, c0:c0 + n_gate], ((0, 0), (0, PROJ_TILE - n_gate))),
                              w_in[:, c0 + n_gate:]], axis=1).astype(BF16)
    ones = jnp.ones((PROJ_TILE,), F32)
    zeros = jnp.zeros((PROJ_TILE,), F32)
    rep = PROJ_TILE // HEAD_DIM
    gains = [ones] * N_TILES
    flags = [zeros] * N_TILES
    for k in range(T_Q, T_KC):
        gains[k], flags[k] = jnp.tile(g_q, rep), ones
    gains[T_KS], flags[T_KS] = jnp.tile(g_k[1], rep), ones
    gains[T_KW], flags[T_KW] = jnp.tile(g_k[2], rep), ones
    gain = jnp.stack(gains)[:, None, :]
    flag = jnp.stack(flags)[:, None, :]
    tril = jnp.tril(w_s)
    wm_p = tril.astype(BF16)
    bsb_p = jnp.broadcast_to(b_s[:, :, None], (A_GROUPS, CHUNK, CHUNK))
    nrep = CHUNK // ts
    wm_s = jnp.einsum("ab,gij->gaibj", jnp.eye(nrep, dtype=F32), tril[:, :ts, :ts]).reshape(A_GROUPS, CHUNK, CHUNK).astype(BF16)
    bsb_s = jnp.broadcast_to(jnp.tile(b_s[:, :ts], (1, nrep))[:, :, None], (A_GROUPS, CHUNK, CHUNK))
    w1k_b, w2k_b, w1v_b, w2v_b = [a.astype(BF16) for a in (w_c1k, w_c2k, w_c1v, w_c2v)]
    wa_b, wb_b, wo_b, wpq_b = [a.astype(BF16) for a in (w_a, w_b, w_o, w_pq)]
    sk1_b, sk2_b = sk1.astype(BF16), sk2.astype(BF16)
    eu_b, ev_b = expert_u.astype(BF16), expert_v.astype(BF16)
    one_gain = jnp.ones((HEAD_DIM,), F32)

    def pool_weights(w1, pe):
        hid = w1.shape[1]
        w1x = w1.reshape(CMP_BLOCK // 2, 2, HEAD_DIM, hid).transpose(0, 2, 1, 3).reshape(CMP_BLOCK // 2 * HEAD_DIM, 2 * hid)
        pe8 = jnp.repeat(pe.reshape(CMP_BLOCK // 2, 2, 1, HEAD_DIM), N_KV, axis=2).reshape(CMP_BLOCK // 2, SUBLANE, HEAD_DIM)
        return w1x.astype(BF16), pe8

    w1k_x, pe8_k = pool_weights(w_c1k, pe_k)
    w1v_x, pe8_v = pool_weights(w_c1v, pe_v)

    nc = bp + bs
    ncp = -(-nc // SUBLANE) * SUBLANE
    c_all = jnp.pad(jnp.concatenate([c_prompt, c_sample], axis=0), ((0, ncp - nc), (0, 0)))
    mod = _modulation(c_all, w_ada, b_ada)
    mod_p = mod[:bp].reshape(bp, 6, 1, d)
    mod_s = mod[bp:nc].reshape(bs, 6, 1, d)

    tm_p = min(1024, tp)
    bt_s = min(1024 // ts, bs)

    proj_p = _in_projection(x_prompt, mod_p, g_n1, w_in_p, gain, flag, 1, tm_p)
    proj_s = _in_projection(x_sample, mod_s, g_n1, w_in_p, gain, flag, bt_s, ts)

    cpb = tp // CHUNK
    ya_p, vch_p = _mixer_a(proj_p, ln_v_g, ln_v_b, wm_p, bsb_p, bp, lambda i: i // cpb)
    ya_s, vch_s = _mixer_a(proj_s, ln_v_g, ln_v_b, wm_s, bsb_s, ns_ // CHUNK, lambda i: i)

    kcmp_p = _compress_prompt(proj_p, T_KC, bp, tp, pe_k, w1k_b, w2k_b, g_k[0], True)
    vcmp_p = _compress_prompt(proj_p, T_VC, bp, tp, pe_v, w1v_b, w2v_b, one_gain, False)
    n_phys, page = cache_k_cmp.shape[1], cache_k_cmp.shape[2]
    bpp = page // CMP_BLOCK
    blk_rows = CMP_BLOCK * N_KV
    kcmp_pool = _compress_pool(cache_k_cmp.reshape(n_phys * bpp, blk_rows, HEAD_DIM), pe8_k, w1k_x, w2k_b, g_k[0], True)
    vcmp_pool = _compress_pool(cache_v_cmp.reshape(n_phys * bpp, blk_rows, HEAD_DIM), pe8_v, w1v_x, w2v_b, one_gain, False)
    npg = page_table.shape[1]
    kcmp_s = kcmp_pool.reshape(n_phys, bpp * SUBLANE * HEAD_DIM)[page_table].reshape(bs, npg * bpp, SUBLANE, HEAD_DIM)
    vcmp_s = vcmp_pool.reshape(n_phys, bpp * SUBLANE * HEAD_DIM)[page_table].reshape(bs, npg * bpp, SUBLANE, HEAD_DIM)

    tables, ctab = _bias_tables(rel_bias, ATT_TILE)
    oc_p, sel_p = _cmp_select(proj_p, ctab, kcmp_p, vcmp_p, bp, tp)
    os_p = _prompt_attention(proj_p, rel_bias, tables, bp, tp, "sel", sel_p)
    ow_p = _prompt_attention(proj_p, rel_bias, tables, bp, tp, "win")

    past = npg * page
    yb_s, kwin_s, vwin_s = _sample_attention(
        proj_s, rel_bias, page_table, kcmp_s, vcmp_s,
        cache_k_sel.reshape(n_phys, page * N_KV, HEAD_DIM), cache_v_sel.reshape(n_phys, page * N_KV, HEAD_DIM),
        state_k_win.reshape(bs, -1, HEAD_DIM), state_v_win.reshape(bs, -1, HEAD_DIM), bs, ts)

    t_p = _merge(proj_p, ya_p, [oc_p, os_p, ow_p], wa_b, wb_b)
    t_s = _merge(proj_s, ya_s, [yb_s], wa_b, wb_b)
    x1_p, h2_p = _out_projection(t_p, x_prompt, mod_p, g_n2, wo_b, 1, min(256, tp))
    x1_s, h2_s = _out_projection(t_s, x_sample, mod_s, g_n2, wo_b, min(256 // ts, bs), ts)

    h2 = jnp.concatenate([h2_p, h2_s], axis=0)
    s1t, s2t = _peer_scores(h2, wpq_b, sk1_b, sk2_b)
    cnt1, e1, rk2, e2 = _peer_topk(s1t, s2t)
    peer = _peer_dense(h2, eu_b, ev_b, cnt1, e1, rk2, e2)
    y_p = _final_residual(x1_p, peer, 0, mod_p, 1, min(512, tp))
    y_s = _final_residual(x1_s, peer, np_, mod_s, min(512 // ts, bs), ts)

    def kv_p(k):
        return proj_p[k].reshape(1, bp, tp, N_KV, HEAD_DIM)

    def kv_s(k):
        return proj_s[k].reshape(1, bs, ts, N_KV, HEAD_DIM)

    wb_p = min(WINDOW, tp)
    wlen = state_k_win.shape[2]
    return (y_p, y_s,
            kv_p(T_KC), kv_p(T_VC), kv_p(T_KS), kv_p(T_VS),
            kv_p(T_KW)[:, :, tp - wb_p:], kv_p(T_VW)[:, :, tp - wb_p:],
            vch_p.reshape(1, bp, CHUNK, -1),
            kv_s(T_KC), kv_s(T_VC), kv_s(T_KS), kv_s(T_VS),
            kwin_s.reshape(1, bs, wlen, N_KV, HEAD_DIM), vwin_s.reshape(1, bs, wlen, N_KV, HEAD_DIM),
            vch_s.reshape(1, bs, ts, -1))


def kernel(x_prompt, x_sample, cache_k_cmp, cache_v_cmp, cache_k_sel, cache_v_sel, state_k_win, state_v_win, page_table, c_prompt, c_sample, rel_bias, w_ada, b_ada, g_n1, g_n2, w_in, ln_v_g, ln_v_b, w_s, b_s, g_q, g_k, pe_k, w_c1k, w_c2k, pe_v, w_c1v, w_c2v, w_a, w_b, w_o, w_pq, sk1, sk2, expert_u, expert_v):
    return _forward(x_prompt, x_sample, cache_k_cmp, cache_v_cmp, cache_k_sel, cache_v_sel, state_k_win, state_v_win,
                    page_table, c_prompt, c_sample, rel_bias, w_ada, b_ada, g_n1, g_n2, w_in, ln_v_g, ln_v_b, w_s, b_s,
                    g_q, g_k, pe_k, w_c1k, w_c2k, pe_v, w_c1v, w_c2v, w_a, w_b, w_o, w_pq, sk1, sk2, expert_u, expert_v)
```

```python
import functools
import math

import numpy as np
import jax
import jax.numpy as jnp
from jax import lax
from jax.experimental import pallas as pl
from jax.experimental.pallas import tpu as pltpu

F32 = jnp.float32
BF16 = jnp.bfloat16

N_HEADS = 16
HEAD_DIM = 128
N_KV = 4
GQA = N_HEADS // N_KV
KV_WIDTH = N_KV * HEAD_DIM
CHUNK = 128
A_GROUPS = 8
CMP_BLOCK = 64
SEL_BLOCK = 64
N_SEL = 16
WINDOW = 512
N_BUCKETS = 32
MAX_DISTANCE = 128
N_KEYS = 128
PEER_HEADS = 8
PEER_TOPK = 16
ATTN_SCALE = HEAD_DIM ** -0.5
NEG = -1e30
FORCE_BONUS = 1e4
EPS = 1e-6
LANE = 128
SUBLANE = 8
PROJ_TILE = 512
ATT_TILE = 256
VMEM_LIMIT = 56 * 1024 * 1024

T_U, T_V, T_Q, T_KC, T_VC, T_KS, T_VS, T_KW, T_VW, T_NSA, T_GA, T_GB, N_TILES = 0, 2, 4, 8, 9, 10, 11, 12, 13, 14, 15, 19, 23


def _bucket_thresholds():
    n = np.arange(0, 2 * MAX_DISTANCE)
    nf = np.maximum(n, 1).astype(np.float32)
    half = N_BUCKETS // 2
    large = half + (np.log(nf / half) / math.log(MAX_DISTANCE / half) * (N_BUCKETS - half)).astype(np.int32)
    b = np.where(n < half, n, np.minimum(large, N_BUCKETS - 1))
    assert np.all(np.diff(b) >= 0) and b[-1] == N_BUCKETS - 1
    return [int(np.argmax(b >= k)) for k in range(N_BUCKETS)]


BUCKET_THR = _bucket_thresholds()
FAR_DIST = BUCKET_THR[-1]
assert FAR_DIST <= MAX_DISTANCE


def _cparams(sem, vmem=VMEM_LIMIT):
    return pltpu.CompilerParams(dimension_semantics=sem, vmem_limit_bytes=vmem)


def _gelu(x):
    c = 2.0 * math.sqrt(2.0 / math.pi)
    u = (x * x) * (-0.044715 * c) - c
    return x * (1.0 / (1.0 + jnp.exp(x * u)))


def _sigmoid(x):
    return 1.0 / (1.0 + jnp.exp(-x))


def _dot_nt(a, b):
    return lax.dot_general(a, b, (((1,), (1,)), ((), ())), preferred_element_type=F32)


def _dot_tn(a, b):
    return lax.dot_general(a, b, (((0,), (0,)), ((), ())), preferred_element_type=F32)


def _bias_chain(dist, rbs):
    b = jnp.full(dist.shape, rbs[0], F32)
    for k in range(1, N_BUCKETS):
        b = jnp.where(dist >= BUCKET_THR[k], rbs[k], b)
    return b


def _mod_kernel(c_ref, w_ref, b_ref, o_ref):
    c = c_ref[...]
    a = (c * _sigmoid(c)).astype(BF16)
    o_ref[...] = jnp.dot(a, w_ref[...].astype(BF16), preferred_element_type=F32) + b_ref[...]


def _modulation(c_all, w_ada, b_ada):
    m, d = c_all.shape
    n = w_ada.shape[1]
    tn = 1024
    return pl.pallas_call(
        _mod_kernel,
        grid=(n // tn,),
        in_specs=[pl.BlockSpec((m, d), lambda j: (0, 0)),
                  pl.BlockSpec((d, tn), lambda j: (0, j)),
                  pl.BlockSpec((1, tn), lambda j: (0, j))],
        out_specs=pl.BlockSpec((m, tn), lambda j: (0, j)),
        out_shape=jax.ShapeDtypeStruct((m, n), F32),
        compiler_params=_cparams(("arbitrary",)),
        name="adaln_mod",
    )(c_all, w_ada, b_ada.reshape(1, n))


def _inproj_kernel(x_ref, sc_ref, sh_ref, gn_ref, w_ref, gain_ref, flag_ref, o_ref, h_scr):
    j = pl.program_id(1)

    @pl.when(j == 0)
    def _():
        x = x_ref[...]
        r = lax.rsqrt(jnp.mean(x * x, axis=-1, keepdims=True) + EPS)
        h = (x * r) * gn_ref[...] * (1.0 + sc_ref[...]) + sh_ref[...]
        h_scr[...] = h.reshape(h_scr.shape).astype(BF16)

    y = jnp.dot(h_scr[...], w_ref[...], preferred_element_type=F32)

    @pl.when(j < T_Q)
    def _():
        o_ref[...] = _gelu(y)

    @pl.when((j >= T_Q) & (j < T_NSA))
    def _():
        parts = []
        for hh in range(PROJ_TILE // HEAD_DIM):
            yh = y[:, hh * HEAD_DIM:(hh + 1) * HEAD_DIM]
            parts.append(yh * lax.rsqrt(jnp.mean(yh * yh, axis=-1, keepdims=True) + EPS))
        yn = jnp.concatenate(parts, axis=1) * gain_ref[...]
        o_ref[...] = jnp.where(flag_ref[...] > 0.5, yn, y)

    @pl.when(j >= T_NSA)
    def _():
        o_ref[...] = _sigmoid(y)


def _in_projection(x3, mod4, g_n1, w_in_p, gain, flag, bt, tt):
    nb, tb, d = x3.shape
    tpb = tb // tt
    tm = bt * tt
    n = nb * tb
    grid = (n // tm, N_TILES)
    return pl.pallas_call(
        _inproj_kernel,
        grid=grid,
        in_specs=[pl.BlockSpec((bt, tt, d), lambda i, j: (i // tpb, i % tpb, 0)),
                  pl.BlockSpec((bt, None, 1, d), lambda i, j: (i // tpb, 1, 0, 0)),
                  pl.BlockSpec((bt, None, 1, d), lambda i, j: (i // tpb, 0, 0, 0)),
                  pl.BlockSpec((1, 1, d), lambda i, j: (0, 0, 0)),
                  pl.BlockSpec((d, PROJ_TILE), lambda i, j: (0, j)),
                  pl.BlockSpec((None, 1, PROJ_TILE), lambda i, j: (j, 0, 0)),
                  pl.BlockSpec((None, 1, PROJ_TILE), lambda i, j: (j, 0, 0))],
        out_specs=pl.BlockSpec((None, tm, PROJ_TILE), lambda i, j: (j, i, 0)),
        out_shape=jax.ShapeDtypeStruct((N_TILES, n, PROJ_TILE), F32),
        scratch_shapes=[pltpu.VMEM((tm, d), BF16)],
        compiler_params=_cparams(("parallel", "arbitrary")),
        name="in_projection",
    )(x3, mod4, mod4, g_n1.reshape(1, 1, d), w_in_p, gain, flag)


def _mixa_kernel(u0_ref, u1_ref, v0_ref, v1_ref, lg_ref, lb_ref, wm_ref, bs_ref, ya_ref, vch_ref):
    v = jnp.concatenate([v0_ref[...], v1_ref[...]], axis=1)
    mu = jnp.mean(v, axis=-1, keepdims=True)
    var = jnp.mean(jnp.square(v - mu), axis=-1, keepdims=True)
    vln = ((v - mu) * lax.rsqrt(var + EPS)) * lg_ref[...] + lb_ref[...]
    vch_ref[...] = vln
    u = jnp.concatenate([u0_ref[...], u1_ref[...]], axis=1)
    vb = vln.astype(BF16)
    gd = vln.shape[1] // A_GROUPS
    for g in range(A_GROUPS):
        sl = slice(g * gd, (g + 1) * gd)
        s = jnp.dot(wm_ref[g], vb[:, sl], preferred_element_type=F32) + bs_ref[g]
        ya_ref[:, sl] = (u[:, sl] * s).astype(BF16)


def _mixer_a(proj, ln_g, ln_b, wm, bsb, vch_blocks, vch_map):
    n = proj.shape[1]
    aw = 2 * PROJ_TILE

    def tile(k):
        return pl.BlockSpec((None, CHUNK, PROJ_TILE), lambda i, k=k: (k, i, 0))

    return pl.pallas_call(
        _mixa_kernel,
        grid=(n // CHUNK,),
        in_specs=[tile(T_U), tile(T_U + 1), tile(T_V), tile(T_V + 1),
                  pl.BlockSpec((1, aw), lambda i: (0, 0)),
                  pl.BlockSpec((1, aw), lambda i: (0, 0)),
                  pl.BlockSpec((A_GROUPS, CHUNK, CHUNK), lambda i: (0, 0, 0)),
                  pl.BlockSpec((A_GROUPS, CHUNK, CHUNK), lambda i: (0, 0, 0))],
        out_specs=[pl.BlockSpec((CHUNK, aw), lambda i: (i, 0)),
                   pl.BlockSpec((CHUNK, aw), lambda i: (vch_map(i), 0))],
        out_shape=[jax.ShapeDtypeStruct((n, aw), BF16),
                   jax.ShapeDtypeStruct((vch_blocks * CHUNK, aw), F32)],
        compiler_params=_cparams(("arbitrary",)),
        name="mixer_a",
    )(proj, proj, proj, proj, ln_g.reshape(1, aw), ln_b.reshape(1, aw), wm, bsb)


def _compress_tail(hid, w2_ref, gain_ref, do_rms):
    out = jnp.dot(_gelu(hid).astype(BF16), w2_ref[...], preferred_element_type=F32)
    if do_rms:
        out = out * lax.rsqrt(jnp.mean(out * out, axis=-1, keepdims=True) + EPS) * gain_ref[...]
    return out


def _compress_prompt_kernel(x0_ref, x1_ref, x2_ref, x3_ref, pe_ref, w1_ref, w2_ref, gain_ref, o_ref, lhs_scr, *, nb, do_rms):
    for s_ in range(CMP_BLOCK):
        for g, x_ref in enumerate((x0_ref, x1_ref, x2_ref, x3_ref)):
            rows = x_ref[pl.ds(s_, nb, stride=CMP_BLOCK), :]
            lhs_scr[g * nb:(g + 1) * nb, s_ * HEAD_DIM:(s_ + 1) * HEAD_DIM] = (rows + pe_ref[s_:s_ + 1, :]).astype(BF16)
    hid = jnp.dot(lhs_scr[...], w1_ref[...], preferred_element_type=F32)
    out = _compress_tail(hid, w2_ref, gain_ref, do_rms)
    for g in range(N_KV):
        o_ref[:, g * HEAD_DIM:(g + 1) * HEAD_DIM] = out[g * nb:(g + 1) * nb]


def _compress_prompt(proj, tile, bsz, t, pe, w1b, w2b, gain, do_rms):
    nb = t // CMP_BLOCK
    hid = w1b.shape[1]
    return pl.pallas_call(
        functools.partial(_compress_prompt_kernel, nb=nb, do_rms=do_rms),
        grid=(bsz,),
        in_specs=[pl.BlockSpec((None, t, HEAD_DIM), lambda b, g=g: (tile, b, g)) for g in range(N_KV)] + [
                  pl.BlockSpec((CMP_BLOCK, HEAD_DIM), lambda b: (0, 0)),
                  pl.BlockSpec((CMP_BLOCK * HEAD_DIM, hid), lambda b: (0, 0)),
                  pl.BlockSpec((hid, HEAD_DIM), lambda b: (0, 0)),
                  pl.BlockSpec((1, HEAD_DIM), lambda b: (0, 0))],
        out_specs=pl.BlockSpec((nb, KV_WIDTH), lambda b: (b, 0)),
        out_shape=jax.ShapeDtypeStruct((bsz * nb, KV_WIDTH), F32),
        scratch_shapes=[pltpu.VMEM((N_KV * nb, CMP_BLOCK * HEAD_DIM), BF16)],
        compiler_params=_cparams(("arbitrary",)),
        name="compress_prompt",
    )(proj, proj, proj, proj, pe, w1b, w2b, gain.reshape(1, HEAD_DIM))


def _compress_pool_kernel(x_ref, pe8_ref, w1_ref, w2_ref, gain_ref, o_ref, lhs_scr, *, do_rms):
    tb = x_ref.shape[0]
    m = tb * SUBLANE
    hid = w2_ref.shape[0]
    for j in range(CMP_BLOCK // 2):
        xj = x_ref[:, SUBLANE * j:SUBLANE * (j + 1), :] + pe8_ref[j]
        lhs_scr[:, j * HEAD_DIM:(j + 1) * HEAD_DIM] = xj.reshape(m, HEAD_DIM).astype(BF16)
    acc = jnp.dot(lhs_scr[...], w1_ref[...], preferred_element_type=F32)
    hidv = acc[:, :hid] + pltpu.roll(acc[:, hid:], m - N_KV, 0)
    out = _compress_tail(hidv, w2_ref, gain_ref, do_rms)
    o_ref[...] = out.reshape(tb, SUBLANE, HEAD_DIM)


def _compress_pool(x3, pe8, w1x, w2b, gain, do_rms, tb=64):
    nblk = x3.shape[0]
    tb = min(tb, nblk)
    hid = w2b.shape[0]
    kdim = CMP_BLOCK // 2 * HEAD_DIM
    return pl.pallas_call(
        functools.partial(_compress_pool_kernel, do_rms=do_rms),
        grid=(nblk // tb,),
        in_specs=[pl.BlockSpec((tb, CMP_BLOCK * N_KV, HEAD_DIM), lambda i: (i, 0, 0)),
                  pl.BlockSpec((CMP_BLOCK // 2, SUBLANE, HEAD_DIM), lambda i: (0, 0, 0)),
                  pl.BlockSpec((kdim, 2 * hid), lambda i: (0, 0)),
                  pl.BlockSpec((hid, HEAD_DIM), lambda i: (0, 0)),
                  pl.BlockSpec((1, HEAD_DIM), lambda i: (0, 0))],
        out_specs=pl.BlockSpec((tb, SUBLANE, HEAD_DIM), lambda i: (i, 0, 0)),
        out_shape=jax.ShapeDtypeStruct((nblk, SUBLANE, HEAD_DIM), F32),
        scratch_shapes=[pltpu.VMEM((tb * SUBLANE, kdim), BF16)],
        compiler_params=_cparams(("arbitrary",)),
        name="compress_pool",
    )(x3, pe8, w1x, w2b, gain.reshape(1, HEAD_DIM))


def _bias_table_kernel(rb_ref, o_ref, c_ref, *, ts):
    g = pl.program_id(0)
    i = lax.broadcasted_iota(jnp.int32, (ts, ts), 0)
    j = lax.broadcasted_iota(jnp.int32, (ts, ts), 1)
    for d in range(2):
        dist = d * ts + i - j
        for r in range(GQA):
            rbs = [rb_ref[k, g * GQA + r] for k in range(N_BUCKETS)]
            o_ref[d, r * ts:(r + 1) * ts, :] = _bias_chain(dist, rbs) - rbs[-1]
    ic = lax.broadcasted_iota(jnp.int32, (ts, LANE), 0)
    nc = lax.broadcasted_iota(jnp.int32, (ts, LANE), 1) - LANE // 2
    dist_c = ic - (nc * CMP_BLOCK + CMP_BLOCK - 1)
    for r in range(GQA):
        rbs = [rb_ref[k, g * GQA + r] for k in range(N_BUCKETS)]
        c_ref[r * ts:(r + 1) * ts, :] = _bias_chain(dist_c, rbs)


def _bias_tables(rel_bias, ts):
    return pl.pallas_call(
        functools.partial(_bias_table_kernel, ts=ts),
        grid=(N_KV,),
        in_specs=[pl.BlockSpec(memory_space=pltpu.SMEM)],
        out_specs=[pl.BlockSpec((None, 2, GQA * ts, ts), lambda g: (g, 0, 0, 0)),
                   pl.BlockSpec((None, GQA * ts, LANE), lambda g: (g, 0, 0))],
        out_shape=[jax.ShapeDtypeStruct((N_KV, 2, GQA * ts, ts), F32),
                   jax.ShapeDtypeStruct((N_KV, GQA * ts, LANE), F32)],
        compiler_params=_cparams(("arbitrary",)),
        name="bias_tables",
    )(rel_bias)


def _rank_select(score, n_sel):
    t = score.shape[1]
    ngrp = -(-n_sel // SUBLANE)
    jrow = lax.broadcasted_iota(jnp.int32, (SUBLANE, t), 0)
    sel = []
    for gb in range(ngrp):
        blk = score[gb * SUBLANE:(gb + 1) * SUBLANE, :]
        rank = jnp.zeros((SUBLANE, t), F32)
        for i in range(n_sel):
            row = score[i:i + 1, :]
            if i < gb * SUBLANE:
                beats = row >= blk
            elif i >= (gb + 1) * SUBLANE:
                beats = row > blk
            else:
                beats = (row > blk) | ((jrow > i - gb * SUBLANE) & (row == blk))
            rank = rank + jnp.where(beats, 1.0, 0.0)
        keep = (rank < float(min(N_SEL, n_sel))) & (jrow + gb * SUBLANE < n_sel)
        sel.append(jnp.where(keep, 1.0, 0.0))
    sel.append(jnp.zeros((score.shape[0] - ngrp * SUBLANE, t), F32))
    return jnp.concatenate(sel, axis=0)


def _cmp_kernel(ct_ref, q0_ref, q1_ref, q2_ref, q3_ref, nsa_ref, kc_ref, vc_ref, oc_ref, sel_ref, *, tq, nb, n_sel):
    qt = pl.program_id(1)
    nsa = nsa_ref[...]
    row = lax.broadcasted_iota(jnp.int32, (tq, LANE), 0) + qt * tq
    col = lax.broadcasted_iota(jnp.int32, (tq, LANE), 1)
    dist = row - (col * CMP_BLOCK + CMP_BLOCK - 1)
    valid = (dist >= 0) & (col < nb)
    cur = row // SEL_BLOCK
    forced = (col == 0) | (col == cur) | (col == cur - 1)
    shift = (qt * (tq // CMP_BLOCK) + LANE // 2) % LANE
    pad = jnp.zeros((LANE - nb, HEAD_DIM), F32)
    for g in range(N_KV):
        q = (q0_ref, q1_ref, q2_ref, q3_ref)[g][...]
        qst = jnp.concatenate([q[:, r * HEAD_DIM:(r + 1) * HEAD_DIM] for r in range(GQA)], axis=0).astype(BF16)
        kg = jnp.concatenate([kc_ref[:, g * HEAD_DIM:(g + 1) * HEAD_DIM], pad], axis=0).astype(BF16)
        vg = jnp.concatenate([vc_ref[:, g * HEAD_DIM:(g + 1) * HEAD_DIM], pad], axis=0).astype(BF16)
        s = _dot_nt(qst, kg) * ATTN_SCALE
        ps = []
        imp = jnp.zeros((tq, LANE), F32)
        for r in range(GQA):
            h = g * GQA + r
            b = pltpu.roll(ct_ref[g, r * tq:(r + 1) * tq, :], shift, 1)
            sr = jnp.where(valid, s[r * tq:(r + 1) * tq] + b, NEG)
            m = jnp.max(sr, axis=-1, keepdims=True)
            p = jnp.where(valid, jnp.exp(sr - m), 0.0)
            den = jnp.sum(p, axis=-1, keepdims=True)
            p = p * (1.0 / jnp.maximum(den, 1e-30))
            imp = imp + p
            ps.append(p)
        o = jnp.dot(jnp.concatenate(ps, axis=0).astype(BF16), vg, preferred_element_type=F32)
        for r in range(GQA):
            h = g * GQA + r
            oc_ref[:, h * HEAD_DIM:(h + 1) * HEAD_DIM] = o[r * tq:(r + 1) * tq] * nsa[:, 3 * h:3 * h + 1]
        score = jnp.where(col <= cur, imp + jnp.where(forced, FORCE_BONUS, 0.0), NEG)
        score = jnp.where(col < n_sel, score, -3e38)
        sel_ref[g] = _rank_select(score.T, n_sel).T


def _cmp_select(proj, ctab, kcmp, vcmp, bsz, t, tq=ATT_TILE):
    nb = kcmp.shape[0] // bsz
    n_sel = -(-t // SEL_BLOCK)
    nq = t // tq
    n = proj.shape[1]
    assert nb <= LANE // 2 and SEL_BLOCK == CMP_BLOCK

    def tile(k):
        return pl.BlockSpec((None, tq, PROJ_TILE), lambda b, i, k=k: (k, b * nq + i, 0))

    return pl.pallas_call(
        functools.partial(_cmp_kernel, tq=tq, nb=nb, n_sel=n_sel),
        grid=(bsz, nq),
        in_specs=[pl.BlockSpec((N_KV, GQA * tq, LANE), lambda b, i: (0, 0, 0)),
                  tile(T_Q), tile(T_Q + 1), tile(T_Q + 2), tile(T_Q + 3), tile(T_NSA),
                  pl.BlockSpec((nb, KV_WIDTH), lambda b, i: (b, 0)),
                  pl.BlockSpec((nb, KV_WIDTH), lambda b, i: (b, 0))],
        out_specs=[pl.BlockSpec((tq, N_HEADS * HEAD_DIM), lambda b, i: (b * nq + i, 0)),
                   pl.BlockSpec((None, N_KV, tq, LANE), lambda b, i: (b, 0, i, 0))],
        out_shape=[jax.ShapeDtypeStruct((n, N_HEADS * HEAD_DIM), F32),
                   jax.ShapeDtypeStruct((bsz, N_KV, t, LANE), F32)],
        compiler_params=_cparams(("parallel", "arbitrary")),
        name="cmp_select",
    )(ctab, proj, proj, proj, proj, proj, kcmp, vcmp)


MASK_BIG = 2.0 ** 100
AUG = 2 * HEAD_DIM
ROW_BLOCK = 128


def _attn_kernel(rb_ref, q_ref, nsa_ref, k_ref, v_ref, tb_ref, prev_ref, *rest, mode, tq, branch):
    if mode == "sel":
        sel_ref, o_ref, kb, vb, qa, s_scr, p_scr, m_s, a_s, acc_s = rest
    else:
        o_ref, kb, vb, qa, s_scr, p_scr, m_s, a_s, acc_s = rest
    g = pl.program_id(1)
    qt = pl.program_id(2)
    tk = tq
    rows4 = GQA * tq
    t_all = kb.shape[0]

    @pl.when(qt == 0)
    def _():
        krow = lax.broadcasted_iota(jnp.int32, (t_all, LANE), 0)
        lane = lax.broadcasted_iota(jnp.int32, (t_all, LANE), 1)
        onehot = ((lane < SEL_BLOCK) & (krow // SEL_BLOCK == lane)) | (lane == SEL_BLOCK) | (lane == SEL_BLOCK + 1)
        kb[:, 0:HEAD_DIM] = k_ref[...].astype(BF16)
        kb[:, HEAD_DIM:AUG] = onehot.astype(BF16)
        vb[:, 0:HEAD_DIM] = v_ref[...].astype(BF16)
        vb[:, HEAD_DIM:AUG] = (lane == 0).astype(BF16)

    q = q_ref[...]
    lane_q = lax.broadcasted_iota(jnp.int32, (tq, LANE), 1)
    if mode == "sel":
        selm = jnp.where(lane_q < SEL_BLOCK, (sel_ref[...] - 1.0) * MASK_BIG, 0.0)
    else:
        selm = jnp.zeros((tq, LANE), F32)
    for r in range(GQA):
        b_far = jnp.full((tq, LANE), rb_ref[N_BUCKETS - 1, g * GQA + r], F32)
        b_hi = b_far.astype(BF16).astype(F32)
        ext = jnp.where(lane_q == SEL_BLOCK, b_hi, jnp.where(lane_q == SEL_BLOCK + 1, b_far - b_hi, selm))
        qa[r * tq:(r + 1) * tq, 0:HEAD_DIM] = (q[:, r * HEAD_DIM:(r + 1) * HEAD_DIM] * ATTN_SCALE).astype(BF16)
        qa[r * tq:(r + 1) * tq, HEAD_DIM:AUG] = ext.astype(BF16)
    m_s[...] = jnp.full(m_s.shape, NEG, F32)
    acc_s[...] = jnp.zeros(acc_s.shape, F32)
    nrb = rows4 // ROW_BLOCK
    rowpos = lax.broadcasted_iota(jnp.int32, (ROW_BLOCK, tk), 0)
    colpos = lax.broadcasted_iota(jnp.int32, (ROW_BLOCK, tk), 1)

    def chunk(kt, table, mask_kind):
        k0 = pl.multiple_of(kt * tk, tk)
        s_scr[...] = _dot_nt(qa[...], kb[pl.ds(k0, tk), :])
        for rb in range(nrb):
            rs = slice(rb * ROW_BLOCK, (rb + 1) * ROW_BLOCK)
            s = s_scr[rs, :]
            if table is not None:
                s = s + tb_ref[table, rs, :]
            if mask_kind is not None:
                rp = rowpos + (rb * ROW_BLOCK) % tq
                keep = (rp >= colpos) if mask_kind == "causal" else (colpos > rp)
                s = jnp.where(keep, s, NEG)
            m_old = m_s[rs, :]
            m_new = jnp.maximum(m_old, jnp.max(s, axis=-1, keepdims=True))
            p_scr[rs, :] = jnp.exp(s - jnp.concatenate([m_new] * (tk // LANE), axis=1)).astype(BF16)
            a_s[rs, :] = jnp.exp(m_old - m_new)
            m_s[rs, :] = m_new
        pv = jnp.dot(p_scr[...], vb[pl.ds(k0, tk), :], preferred_element_type=F32)
        a = a_s[...]
        acc_s[...] = jnp.concatenate([a] * (AUG // LANE), axis=1) * acc_s[...] + pv

    chunk(qt, 0, "causal")

    @pl.when(qt >= 1)
    def _():
        chunk(qt - 1, 1, None)

    if mode == "sel":
        def far_body(kt, carry):
            chunk(kt, None, None)
            return carry
        lax.fori_loop(0, jnp.maximum(qt - 1, 0), far_body, 0)
    else:
        @pl.when(qt >= 2)
        def _():
            chunk(qt - 2, None, "window")

    acc = acc_s[...]
    o = acc[:, 0:HEAD_DIM] * (1.0 / jnp.maximum(acc[:, HEAD_DIM:HEAD_DIM + 1], 1e-30))
    nsa = nsa_ref[...]
    lane = lax.broadcasted_iota(jnp.int32, nsa.shape, 1)
    for r in range(GQA):
        gidx = (g * GQA + r) * 3 + branch
        gate = jnp.sum(jnp.where(lane == gidx, nsa, 0.0), axis=-1, keepdims=True)
        hs = slice(r * HEAD_DIM, (r + 1) * HEAD_DIM)
        o_ref[:, hs] = prev_ref[:, hs] + o[r * tq:(r + 1) * tq] * gate


def _prompt_attention(proj, rel_bias, tables, prev, bsz, t, mode, sel=None, tq=ATT_TILE):
    nq = t // tq
    n = proj.shape[1]
    assert t // SEL_BLOCK <= SEL_BLOCK and tq % ROW_BLOCK == 0
    if mode == "sel":
        tk_, tv_, branch = T_KS, T_VS, 1
    else:
        tk_, tv_, branch = T_KW, T_VW, 2
        assert WINDOW == 2 * tq
    in_specs = [pl.BlockSpec(memory_space=pltpu.SMEM),
                pl.BlockSpec((None, tq, PROJ_TILE), lambda b, g, i: (T_Q + g, b * nq + i, 0)),
                pl.BlockSpec((None, tq, PROJ_TILE), lambda b, g, i: (T_NSA, b * nq + i, 0)),
                pl.BlockSpec((None, t, HEAD_DIM), lambda b, g, i: (tk_, b, g)),
                pl.BlockSpec((None, t, HEAD_DIM), lambda b, g, i: (tv_, b, g)),
                pl.BlockSpec((None, 2, GQA * tq, tq), lambda b, g, i: (g, 0, 0, 0)),
                pl.BlockSpec((tq, GQA * HEAD_DIM), lambda b, g, i: (b * nq + i, g))]
    args = [rel_bias, proj, proj, proj, proj, tables, prev]
    if mode == "sel":
        in_specs += [pl.BlockSpec((None, None, tq, LANE), lambda b, g, i: (b, g, i, 0))]
        args += [sel]
    return pl.pallas_call(
        functools.partial(_attn_kernel, mode=mode, tq=tq, branch=branch),
        grid=(bsz, N_KV, nq),
        in_specs=in_specs,
        out_specs=pl.BlockSpec((tq, GQA * HEAD_DIM), lambda b, g, i: (b * nq + i, g)),
        out_shape=jax.ShapeDtypeStruct((n, N_HEADS * HEAD_DIM), F32),
        scratch_shapes=[pltpu.VMEM((t, AUG), BF16), pltpu.VMEM((t, AUG), BF16),
                        pltpu.VMEM((GQA * tq, AUG), BF16),
                        pltpu.VMEM((GQA * tq, tq), F32), pltpu.VMEM((GQA * tq, tq), BF16),
                        pltpu.VMEM((GQA * tq, LANE), F32), pltpu.VMEM((GQA * tq, LANE), F32),
                        pltpu.VMEM((GQA * tq, AUG), F32)],
        compiler_params=_cparams(("parallel", "parallel", "arbitrary")),
        name="attn_" + mode,
    )(*args)


def _masked_softmax(s, mask):
    s = jnp.where(mask, s, NEG)
    m = jnp.max(s, axis=-1, keepdims=True)
    p = jnp.where(mask, jnp.exp(s - m), 0.0)
    den = jnp.sum(p, axis=-1, keepdims=True)
    return p * (1.0 / jnp.maximum(den, 1e-30))


def _near_far_bias(rb_ref, g, dist_near, tdec, width, near):
    rows = []
    for r in range(GQA):
        h = g * GQA + r
        rbs = [rb_ref[k, h] for k in range(N_BUCKETS)]
        nb_ = _bias_chain(dist_near, rbs)
        rows.append(jnp.concatenate([jnp.full((tdec, width - near), rbs[-1], F32), nb_], axis=1))
    return jnp.concatenate(rows, axis=0)


def _sattn_kernel(pt_ref, rb_ref, q0_ref, q1_ref, q2_ref, q3_ref, ksn_ref, vsn_ref, kwn_ref, vwn_ref, nsa_ref,
                  kc_ref, vc_ref, *rest, npg, page, tdec, wlen):
    kpages = rest[:npg]
    vpages = rest[npg:2 * npg]
    skw_ref, svw_ref, e_ref, yb_ref, kwo_ref, vwo_ref, kbuf, vbuf, wkb, wvb = rest[2 * npg:]
    past = npg * page
    lk = past + LANE
    nb = kc_ref.shape[0]
    n_sel = -(-(past + tdec) // SEL_BLOCK)
    rows_w = wlen * N_KV
    near = 2 * LANE
    rq = GQA * tdec
    nsa = nsa_ref[...]

    kwo_ref[0:rows_w - tdec * N_KV, :] = skw_ref[tdec * N_KV:rows_w, :]
    vwo_ref[0:rows_w - tdec * N_KV, :] = svw_ref[tdec * N_KV:rows_w, :]
    for g in range(N_KV):
        kwo_ref[pl.ds(rows_w - tdec * N_KV + g, tdec, stride=N_KV), :] = kwn_ref[:, g * HEAD_DIM:(g + 1) * HEAD_DIM]
        vwo_ref[pl.ds(rows_w - tdec * N_KV + g, tdec, stride=N_KV), :] = vwn_ref[:, g * HEAD_DIM:(g + 1) * HEAD_DIM]

    trow1 = lax.broadcasted_iota(jnp.int32, (tdec, LANE), 0) + past
    col1 = lax.broadcasted_iota(jnp.int32, (tdec, LANE), 1)
    dist_c = trow1 - (col1 * CMP_BLOCK + CMP_BLOCK - 1)
    valid_c = (dist_c >= 0) & (col1 < nb)
    cur = trow1 // SEL_BLOCK
    forced = (col1 == 0) | (col1 == cur) | (col1 == cur - 1)

    trow_s = (lax.broadcasted_iota(jnp.int32, (rq, lk), 0) & (tdec - 1)) + past
    pos_s = lax.broadcasted_iota(jnp.int32, (rq, lk), 1)
    causal_s = pos_s <= trow_s
    dist_sn = (lax.broadcasted_iota(jnp.int32, (tdec, near), 0) + past) - (lax.broadcasted_iota(jnp.int32, (tdec, near), 1) + lk - near)

    wl = wlen + LANE
    qidx_w = (lax.broadcasted_iota(jnp.int32, (rq, wl), 0) & (tdec - 1)) + wlen
    kidx_w = lax.broadcasted_iota(jnp.int32, (rq, wl), 1)
    dist_w = qidx_w - kidx_w
    mask_w = (dist_w >= 0) & (dist_w < WINDOW)
    dist_wn = (lax.broadcasted_iota(jnp.int32, (tdec, near), 0) + wlen) - (lax.broadcasted_iota(jnp.int32, (tdec, near), 1) + wl - near)

    zpad = jnp.zeros((LANE - tdec, HEAD_DIM), F32)
    cpad = jnp.zeros((LANE - nb, HEAD_DIM), F32)
    for g in range(N_KV):
        gs = slice(g * HEAD_DIM, (g + 1) * HEAD_DIM)
        q = (q0_ref, q1_ref, q2_ref, q3_ref)[g][...]
        qst = jnp.concatenate([q[:, r * HEAD_DIM:(r + 1) * HEAD_DIM] for r in range(GQA)], axis=0).astype(BF16)

        kg = jnp.concatenate([kc_ref[:, g, :], cpad], axis=0).astype(BF16)
        vg = jnp.concatenate([vc_ref[:, g, :], cpad], axis=0).astype(BF16)
        s = _dot_nt(qst, kg) * ATTN_SCALE
        ps = []
        imp = jnp.zeros((tdec, LANE), F32)
        for r in range(GQA):
            h = g * GQA + r
            b = _bias_chain(dist_c, [rb_ref[k, h] for k in range(N_BUCKETS)])
            p = _masked_softmax(s[r * tdec:(r + 1) * tdec] + b, valid_c)
            imp = imp + p
            ps.append(p)
        o_c = jnp.dot(jnp.concatenate(ps, axis=0).astype(BF16), vg, preferred_element_type=F32)

        score = jnp.where(col1 <= cur, imp + jnp.where(forced, FORCE_BONUS, 0.0), NEG)
        score = jnp.where(col1 < n_sel, score, -3e38)
        rank = jnp.zeros((tdec, LANE), F32)
        for i in range(n_sel):
            ci = score[:, i:i + 1]
            rank = rank + ((ci > score) | ((ci == score) & (col1 > i))).astype(F32)
        sel = ((rank < float(min(N_SEL, n_sel))) & (col1 < n_sel)).astype(F32)

        for p_ in range(npg):
            kbuf[p_ * page:(p_ + 1) * page, :] = kpages[p_][pl.ds(g, page, stride=N_KV), :].astype(BF16)
            vbuf[p_ * page:(p_ + 1) * page, :] = vpages[p_][pl.ds(g, page, stride=N_KV), :].astype(BF16)
        kbuf[past:lk, :] = jnp.concatenate([ksn_ref[:, gs], zpad], axis=0).astype(BF16)
        vbuf[past:lk, :] = jnp.concatenate([vsn_ref[:, gs], zpad], axis=0).astype(BF16)
        s = _dot_nt(qst, kbuf[...]) * ATTN_SCALE + _near_far_bias(rb_ref, g, dist_sn, tdec, lk, near)
        sel4 = jnp.concatenate([sel] * GQA, axis=0).astype(BF16)
        mask = (jnp.dot(sel4, e_ref[...], preferred_element_type=F32) > 0.5) & causal_s
        o_s = jnp.dot(_masked_softmax(s, mask).astype(BF16), vbuf[...], preferred_element_type=F32)

        wkb[0:wlen, :] = skw_ref[pl.ds(g, wlen, stride=N_KV), :].astype(BF16)
        wvb[0:wlen, :] = svw_ref[pl.ds(g, wlen, stride=N_KV), :].astype(BF16)
        wkb[wlen:wl, :] = jnp.concatenate([kwn_ref[:, gs], zpad], axis=0).astype(BF16)
        wvb[wlen:wl, :] = jnp.concatenate([vwn_ref[:, gs], zpad], axis=0).astype(BF16)
        s = _dot_nt(qst, wkb[...]) * ATTN_SCALE + _near_far_bias(rb_ref, g, dist_wn, tdec, wl, near)
        o_w = jnp.dot(_masked_softmax(s, mask_w).astype(BF16), wvb[...], preferred_element_type=F32)

        for r in range(GQA):
            h = g * GQA + r
            rs = slice(r * tdec, (r + 1) * tdec)
            yb_ref[:, h * HEAD_DIM:(h + 1) * HEAD_DIM] = (nsa[:, 3 * h:3 * h + 1] * o_c[rs]
                                                         + nsa[:, 3 * h + 1:3 * h + 2] * o_s[rs]
                                                         + nsa[:, 3 * h + 2:3 * h + 3] * o_w[rs])


def _sample_attention(proj, rel_bias, page_table, kcmp_g, vcmp_g, ck_sel, cv_sel, skw, svw, emat, bsz, tdec):
    npg = page_table.shape[1]
    page = ck_sel.shape[1] // N_KV
    wlen = skw.shape[1] // N_KV
    nb = kcmp_g.shape[1]
    past = npg * page
    lk = past + LANE
    assert tdec == SUBLANE and FAR_DIST <= LANE and wlen == WINDOW

    def tile(k):
        return pl.BlockSpec((None, tdec, PROJ_TILE), lambda b, pt, k=k: (k, b, 0))

    def pagespec(p_):
        return pl.BlockSpec((None, page * N_KV, HEAD_DIM), lambda b, pt, p_=p_: (pt[b, p_], 0, 0))

    in_specs = ([pl.BlockSpec(memory_space=pltpu.SMEM)]
                + [tile(T_Q + g) for g in range(N_KV)]
                + [tile(T_KS), tile(T_VS), tile(T_KW), tile(T_VW), tile(T_NSA)]
                + [pl.BlockSpec((None, nb, SUBLANE, HEAD_DIM), lambda b, pt: (b, 0, 0, 0))] * 2
                + [pagespec(p_) for p_ in range(npg)] * 2
                + [pl.BlockSpec((None, wlen * N_KV, HEAD_DIM), lambda b, pt: (b, 0, 0))] * 2
                + [pl.BlockSpec((LANE, lk), lambda b, pt: (0, 0))])
    grid_spec = pltpu.PrefetchScalarGridSpec(
        num_scalar_prefetch=1,
        grid=(bsz,),
        in_specs=in_specs,
        out_specs=[pl.BlockSpec((tdec, N_HEADS * HEAD_DIM), lambda b, pt: (b, 0)),
                   pl.BlockSpec((None, wlen * N_KV, HEAD_DIM), lambda b, pt: (b, 0, 0)),
                   pl.BlockSpec((None, wlen * N_KV, HEAD_DIM), lambda b, pt: (b, 0, 0))],
        scratch_shapes=[pltpu.VMEM((lk, HEAD_DIM), BF16), pltpu.VMEM((lk, HEAD_DIM), BF16),
                        pltpu.VMEM((wlen + LANE, HEAD_DIM), BF16), pltpu.VMEM((wlen + LANE, HEAD_DIM), BF16)])
    return pl.pallas_call(
        functools.partial(_sattn_kernel, npg=npg, page=page, tdec=tdec, wlen=wlen),
        grid_spec=grid_spec,
        out_shape=[jax.ShapeDtypeStruct((bsz * tdec, N_HEADS * HEAD_DIM), F32),
                   jax.ShapeDtypeStruct((bsz, wlen * N_KV, HEAD_DIM), F32),
                   jax.ShapeDtypeStruct((bsz, wlen * N_KV, HEAD_DIM), F32)],
        compiler_params=_cparams(("arbitrary",)),
        name="sample_attention",
    )(page_table, rel_bias, *([proj] * 9), kcmp_g, vcmp_g, *([ck_sel] * npg), *([cv_sel] * npg), skw, svw, emat)


def _merge_kernel(ya_ref, yb_ref, wa_ref, wb_ref, ga_ref, gb_ref, t_ref, yb_scr):
    @pl.when(pl.program_id(1) == 0)
    def _():
        yb_scr[...] = yb_ref[...].astype(BF16)

    a = jnp.dot(ya_ref[...], wa_ref[...], preferred_element_type=F32)
    b = jnp.dot(yb_scr[...], wb_ref[...], preferred_element_type=F32)
    t_ref[...] = (ga_ref[...] * a + gb_ref[...] * b).astype(BF16)


def _merge(proj, ya, yb, wa_b, wb_b, tm=1024):
    n, aw = ya.shape
    d = wb_b.shape[0]
    tm = min(tm, n)
    nj = d // PROJ_TILE
    return pl.pallas_call(
        _merge_kernel,
        grid=(n // tm, nj),
        in_specs=[pl.BlockSpec((tm, aw), lambda i, j: (i, 0)),
                  pl.BlockSpec((tm, d), lambda i, j: (i, 0)),
                  pl.BlockSpec((aw, PROJ_TILE), lambda i, j: (0, j)),
                  pl.BlockSpec((d, PROJ_TILE), lambda i, j: (0, j)),
                  pl.BlockSpec((None, tm, PROJ_TILE), lambda i, j: (T_GA + j, i, 0)),
                  pl.BlockSpec((None, tm, PROJ_TILE), lambda i, j: (T_GB + j, i, 0))],
        out_specs=pl.BlockSpec((tm, PROJ_TILE), lambda i, j: (i, j)),
        out_shape=jax.ShapeDtypeStruct((n, d), BF16),
        scratch_shapes=[pltpu.VMEM((tm, d), BF16)],
        compiler_params=_cparams(("parallel", "arbitrary")),
        name="merge",
    )(ya, yb, wa_b, wb_b, proj, proj)


def _outproj_kernel(t_ref, x_ref, gt_ref, sc_ref, sh_ref, gn_ref, wo_ref, x1_ref, h2_ref):
    y = jnp.dot(t_ref[...], wo_ref[...], preferred_element_type=F32)
    x1 = x_ref[...] + gt_ref[...] * y.reshape(x_ref.shape)
    x1_ref[...] = x1
    r = lax.rsqrt(jnp.mean(x1 * x1, axis=-1, keepdims=True) + EPS)
    h2 = (x1 * r) * gn_ref[...] * (1.0 + sc_ref[...]) + sh_ref[...]
    h2_ref[...] = h2.reshape(h2_ref.shape).astype(BF16)


def _out_projection(tmix, x3, mod4, g_n2, wo_b, bt, tt):
    nb, tb, d = x3.shape
    tpb = tb // tt
    tm = bt * tt
    n = nb * tb

    def modspec(k):
        return pl.BlockSpec((bt, None, 1, d), lambda i, k=k: (i // tpb, k, 0, 0))

    return pl.pallas_call(
        _outproj_kernel,
        grid=(n // tm,),
        in_specs=[pl.BlockSpec((tm, d), lambda i: (i, 0)),
                  pl.BlockSpec((bt, tt, d), lambda i: (i // tpb, i % tpb, 0)),
                  modspec(2), modspec(4), modspec(3),
                  pl.BlockSpec((1, 1, d), lambda i: (0, 0, 0)),
                  pl.BlockSpec((d, d), lambda i: (0, 0))],
        out_specs=[pl.BlockSpec((bt, tt, d), lambda i: (i // tpb, i % tpb, 0)),
                   pl.BlockSpec((tm, d), lambda i: (i, 0))],
        out_shape=[jax.ShapeDtypeStruct((nb, tb, d), F32), jax.ShapeDtypeStruct((n, d), BF16)],
        compiler_params=_cparams(("arbitrary",)),
        name="out_projection",
    )(tmix, x3, mod4, mod4, mod4, g_n2.reshape(1, 1, d), wo_b)


def _peer_scores_kernel(h_ref, wpq_ref, sk1_ref, sk2_ref, s1_ref, s2_ref):
    pq = jnp.dot(h_ref[...], wpq_ref[...], preferred_element_type=F32)
    kd = sk1_ref.shape[1]
    for hd in range(PEER_HEADS):
        q1 = pq[:, hd * 2 * kd:hd * 2 * kd + kd].astype(BF16)
        q2 = pq[:, hd * 2 * kd + kd:(hd + 1) * 2 * kd].astype(BF16)
        s1_ref[hd] = _dot_nt(sk1_ref[...], q1)
        s2_ref[hd] = _dot_nt(sk2_ref[...], q2)


def _peer_scores(h2, wpq_b, sk1_b, sk2_b, tm=512):
    n, d = h2.shape
    dq = wpq_b.shape[1]
    nk, kd = sk1_b.shape
    return pl.pallas_call(
        _peer_scores_kernel,
        grid=(n // tm,),
        in_specs=[pl.BlockSpec((tm, d), lambda i: (i, 0)),
                  pl.BlockSpec((d, dq), lambda i: (0, 0)),
                  pl.BlockSpec((nk, kd), lambda i: (0, 0)),
                  pl.BlockSpec((nk, kd), lambda i: (0, 0))],
        out_specs=[pl.BlockSpec((PEER_HEADS, nk, tm), lambda i: (0, 0, i))] * 2,
        out_shape=[jax.ShapeDtypeStruct((PEER_HEADS, nk, n), F32)] * 2,
        compiler_params=_cparams(("arbitrary",)),
        name="peer_scores",
    )(h2, wpq_b, sk1_b, sk2_b)


def _staircase():
    return [(a, b) for a in range(PEER_TOPK) for b in range(PEER_TOPK) if (a + 1) * (b + 1) <= PEER_TOPK]


def _extract_top(s, rows_f):
    vals = []
    rank = jnp.full(s.shape, float(PEER_TOPK), F32)
    for a in range(PEER_TOPK):
        m = jnp.max(s, axis=0, keepdims=True)
        idx = jnp.min(jnp.where(s == m, rows_f, 1e9), axis=0, keepdims=True)
        hit = rows_f == idx
        rank = jnp.where(hit, float(a), rank)
        s = jnp.where(hit, -jnp.inf, s)
        vals.append(m)
    return vals, rank


def _peer_topk_kernel(s1_ref, s2_ref, cnt_ref, e1_ref, rk_ref, e2_ref):
    nk, tn = s1_ref.shape[1], s1_ref.shape[2]
    rows_f = lax.broadcasted_iota(jnp.int32, (nk, tn), 0).astype(F32)
    pairs = _staircase()
    npad = -(-len(pairs) // SUBLANE) * SUBLANE
    prow = lax.broadcasted_iota(jnp.int32, (npad, tn), 0)
    flat_f = jnp.full((npad, tn), 1e9, F32)
    arow_f = jnp.full((npad, tn), -1.0, F32)
    for i, (a, b) in enumerate(pairs):
        flat_f = jnp.where(prow == i, float(a * PEER_TOPK + b), flat_f)
        arow_f = jnp.where(prow == i, float(a), arow_f)

    def body(hd, carry):
        s1 = s1_ref[hd]
        s2 = s2_ref[hd]
        v1, rank1 = _extract_top(s1, rows_f)
        v2, rank2 = _extract_top(s2, rows_f)
        cand = jnp.full((npad, tn), -jnp.inf, F32)
        for i, (a, b) in enumerate(pairs):
            cand = jnp.where(prow == i, v1[a] + v2[b], cand)
        m0 = v1[0] + v2[0]
        c = cand
        selected = jnp.zeros((npad, tn), F32)
        for _ in range(PEER_TOPK):
            m = jnp.max(c, axis=0, keepdims=True)
            idx = jnp.min(jnp.where(c == m, flat_f, 2e9), axis=0, keepdims=True)
            hit = flat_f == idx
            selected = jnp.where(hit, 1.0, selected)
            c = jnp.where(hit, -jnp.inf, c)
        z = jnp.sum(jnp.where(selected > 0.5, jnp.exp(cand - m0), 0.0), axis=0, keepdims=True)
        cnt1 = jnp.zeros((nk, tn), F32)
        for a in range(PEER_TOPK):
            cnt_a = jnp.sum(jnp.where(arow_f == float(a), selected, 0.0), axis=0, keepdims=True)
            cnt1 = jnp.where(rank1 == float(a), cnt_a, cnt1)
        cnt_ref[hd] = cnt1
        e1_ref[hd] = jnp.exp(s1 - v1[0]) * (1.0 / z)
        rk_ref[hd] = rank2
        e2_ref[hd] = jnp.exp(s2 - v2[0])
        return carry

    lax.fori_loop(0, PEER_HEADS, body, 0)


def _peer_topk(s1t, s2t, tn=256):
    nh, nk, n = s1t.shape
    spec = pl.BlockSpec((nh, nk, tn), lambda i: (0, 0, i))
    return pl.pallas_call(
        _peer_topk_kernel,
        grid=(n // tn,),
        in_specs=[spec, spec],
        out_specs=[spec] * 4,
        out_shape=[jax.ShapeDtypeStruct((nh, nk, n), F32)] * 4,
        compiler_params=_cparams(("arbitrary",)),
        name="peer_topk",
    )(s1t, s2t)


PEER_SUB = 256


def _peer_dense_kernel(h_ref, eu_ref, ev_ref, cnt_ref, e1_ref, rk_ref, e2_ref, o_ref, at_scr, wa_scr, *, te):
    e = pl.program_id(1)
    nk = rk_ref.shape[1]
    tm, d = h_ref.shape

    @pl.when(e == 0)
    def _():
        o_ref[...] = jnp.zeros(o_ref.shape, F32)

    n_i1 = PEER_SUB // nk
    nsub = te // PEER_SUB
    assert te // nk == SUBLANE
    i1_base = pl.multiple_of(e * SUBLANE, SUBLANE)
    tok_piece = 2 * LANE
    n_tok = tm // tok_piece
    col_piece = 2 * LANE
    n_col = d // col_piece
    tiles = [(il, tb) for il in range(n_i1) for tb in range(tm // LANE)]

    def pre_activation(sb, k):
        ts_ = slice(k * tok_piece, (k + 1) * tok_piece)
        at_scr[sb, :, ts_] = _dot_nt(eu_ref[sb * PEER_SUB:(sb + 1) * PEER_SUB, :], h_ref[ts_, :])

    def down_projection(sb, k):
        cs_ = slice(k * col_piece, (k + 1) * col_piece)
        o_ref[:, cs_] += _dot_tn(wa_scr[sb], ev_ref[sb * PEER_SUB:(sb + 1) * PEER_SUB, cs_])

    def gate_tile(sb, il, tb):
        j1 = sb * n_i1 + il
        ks = slice(il * nk, (il + 1) * nk)
        cs = slice(tb * LANE, (tb + 1) * LANE)
        w = jnp.zeros((nk, LANE), F32)
        for hd in range(PEER_HEADS):
            c = cnt_ref[hd, pl.ds(i1_base, SUBLANE), cs][j1:j1 + 1, :]
            g1 = e1_ref[hd, pl.ds(i1_base, SUBLANE), cs][j1:j1 + 1, :]
            w = w + jnp.where(rk_ref[hd, :, cs] < c, e2_ref[hd, :, cs] * g1, 0.0)
        wa_scr[sb, ks, cs] = (w * _gelu(at_scr[sb, ks, cs])).astype(BF16)

    for k in range(n_tok):
        pre_activation(0, k)
    nslot = max(len(tiles), n_col)
    for sb in range(nsub):
        for k in range(nslot):
            if sb >= 1 and k < n_col:
                down_projection(sb - 1, k)
            if sb + 1 < nsub and k % (nslot // n_tok) == 0:
                pre_activation(sb + 1, k // (nslot // n_tok))
            if k < len(tiles):
                gate_tile(sb, *tiles[k])
    for k in range(n_col):
        down_projection(nsub - 1, k)


def _peer_dense(h2, eu_b, ev_b, cnt1, e1, rk2, e2, tm=512, te=1024):
    n, d = h2.shape
    ne = eu_b.shape[0]
    nh, nk, _ = cnt1.shape
    res = pl.BlockSpec((nh, nk, tm), lambda i, e: (0, 0, i))
    return pl.pallas_call(
        functools.partial(_peer_dense_kernel, te=te),
        grid=(n // tm, ne // te),
        in_specs=[pl.BlockSpec((tm, d), lambda i, e: (i, 0)),
                  pl.BlockSpec((te, d), lambda i, e: (e, 0)),
                  pl.BlockSpec((te, d), lambda i, e: (e, 0)),
                  res, res, res, res],
        out_specs=pl.BlockSpec((tm, d), lambda i, e: (i, 0)),
        out_shape=jax.ShapeDtypeStruct((n, d), F32),
        scratch_shapes=[pltpu.VMEM((te // PEER_SUB, PEER_SUB, tm), F32), pltpu.VMEM((te // PEER_SUB, PEER_SUB, tm), BF16)],
        compiler_params=_cparams(("parallel", "arbitrary")),
        name="peer_dense",
    )(h2, eu_b, ev_b, cnt1, e1, rk2, e2)


def _final_kernel(x1_ref, p_ref, gt_ref, o_ref):
    o_ref[...] = x1_ref[...] + gt_ref[...] * p_ref[...].reshape(x1_ref.shape)


def _final_residual(x1, peer, row0, mod4, bt, tt):
    nb, tb, d = x1.shape
    tpb = tb // tt
    tm = bt * tt
    n = nb * tb
    off = row0 // tm
    return pl.pallas_call(
        _final_kernel,
        grid=(n // tm,),
        in_specs=[pl.BlockSpec((bt, tt, d), lambda i: (i // tpb, i % tpb, 0)),
                  pl.BlockSpec((tm, d), lambda i: (off + i, 0)),
                  pl.BlockSpec((bt, None, 1, d), lambda i: (i // tpb, 5, 0, 0))],
        out_specs=pl.BlockSpec((bt, tt, d), lambda i: (i // tpb, i % tpb, 0)),
        out_shape=jax.ShapeDtypeStruct((nb, tb, d), F32),
        compiler_params=_cparams(("arbitrary",)),
        name="final_residual",
    )(x1, peer, mod4)


def _block_expand(n_cols, width=LANE):
    j = np.arange(width)[:, None]
    s = np.arange(n_cols)[None, :]
    return (s // SEL_BLOCK == j).astype(np.float32)


def _forward(x_prompt, x_sample, cache_k_cmp, cache_v_cmp, cache_k_sel, cache_v_sel, state_k_win, state_v_win,
             page_table, c_prompt, c_sample, rel_bias, w_ada, b_ada, g_n1, g_n2, w_in, ln_v_g, ln_v_b, w_s, b_s,
             g_q, g_k, pe_k, w_c1k, w_c2k, pe_v, w_c1v, w_c2v, w_a, w_b, w_o, w_pq, sk1, sk2, expert_u, expert_v):
    assert w_ada.shape[0] == 1, "single layer"
    bp, tp, d = x_prompt.shape
    bs, ts, _ = x_sample.shape
    np_, ns_ = bp * tp, bs * ts
    (w_ada, b_ada, g_n1, g_n2, w_in, ln_v_g, ln_v_b, w_s, b_s, g_q, g_k, pe_k, w_c1k, w_c2k, pe_v, w_c1v, w_c2v,
     w_a, w_b, w_o, w_pq, sk1, sk2, expert_u, expert_v) = [a[0] for a in (
         w_ada, b_ada, g_n1, g_n2, w_in, ln_v_g, ln_v_b, w_s, b_s, g_q, g_k, pe_k, w_c1k, w_c2k, pe_v, w_c1v, w_c2v,
         w_a, w_b, w_o, w_pq, sk1, sk2, expert_u, expert_v)]

    n_gate = 3 * N_HEADS
    c0 = T_NSA * PROJ_TILE
    w_in_p = jnp.concatenate([w_in[:, :c0],
                              jnp.pad(w_in[:, c0:c0 + n_gate], ((0, 0), (0, PROJ_TILE - n_gate))),
                              w_in[:, c0 + n_gate:]], axis=1).astype(BF16)
    ones = jnp.ones((PROJ_TILE,), F32)
    zeros = jnp.zeros((PROJ_TILE,), F32)
    rep = PROJ_TILE // HEAD_DIM
    gains = [ones] * N_TILES
    flags = [zeros] * N_TILES
    for k in range(T_Q, T_KC):
        gains[k], flags[k] = jnp.tile(g_q, rep), ones
    gains[T_KS], flags[T_KS] = jnp.tile(g_k[1], rep), ones
    gains[T_KW], flags[T_KW] = jnp.tile(g_k[2], rep), ones
    gain = jnp.stack(gains)[:, None, :]
    flag = jnp.stack(flags)[:, None, :]
    tril = jnp.tril(w_s)
    wm_p = tril.astype(BF16)
    bsb_p = jnp.broadcast_to(b_s[:, :, None], (A_GROUPS, CHUNK, CHUNK))
    nrep = CHUNK // ts
    wm_s = jnp.einsum("ab,gij->gaibj", jnp.eye(nrep, dtype=F32), tril[:, :ts, :ts]).reshape(A_GROUPS, CHUNK, CHUNK).astype(BF16)
    bsb_s = jnp.broadcast_to(jnp.tile(b_s[:, :ts], (1, nrep))[:, :, None], (A_GROUPS, CHUNK, CHUNK))
    w1k_b, w2k_b, w1v_b, w2v_b = [a.astype(BF16) for a in (w_c1k, w_c2k, w_c1v, w_c2v)]
    wa_b, wb_b, wo_b, wpq_b = [a.astype(BF16) for a in (w_a, w_b, w_o, w_pq)]
    sk1_b, sk2_b = sk1.astype(BF16), sk2.astype(BF16)
    eu_b, ev_b = expert_u.astype(BF16), expert_v.astype(BF16)
    one_gain = jnp.ones((HEAD_DIM,), F32)

    def pool_weights(w1, pe):
        hid = w1.shape[1]
        w1x = w1.reshape(CMP_BLOCK // 2, 2, HEAD_DIM, hid).transpose(0, 2, 1, 3).reshape(CMP_BLOCK // 2 * HEAD_DIM, 2 * hid)
        pe8 = jnp.repeat(pe.reshape(CMP_BLOCK // 2, 2, 1, HEAD_DIM), N_KV, axis=2).reshape(CMP_BLOCK // 2, SUBLANE, HEAD_DIM)
        return w1x.astype(BF16), pe8

    w1k_x, pe8_k = pool_weights(w_c1k, pe_k)
    w1v_x, pe8_v = pool_weights(w_c1v, pe_v)

    nc = bp + bs
    ncp = -(-nc // SUBLANE) * SUBLANE
    c_all = jnp.pad(jnp.concatenate([c_prompt, c_sample], axis=0), ((0, ncp - nc), (0, 0)))
    mod = _modulation(c_all, w_ada, b_ada)
    mod_p = mod[:bp].reshape(bp, 6, 1, d)
    mod_s = mod[bp:nc].reshape(bs, 6, 1, d)

    tm_p = min(1024, tp)
    bt_s = min(1024 // ts, bs)

    proj_p = _in_projection(x_prompt, mod_p, g_n1, w_in_p, gain, flag, 1, tm_p)
    proj_s = _in_projection(x_sample, mod_s, g_n1, w_in_p, gain, flag, bt_s, ts)

    cpb = tp // CHUNK
    ya_p, vch_p = _mixer_a(proj_p, ln_v_g, ln_v_b, wm_p, bsb_p, bp, lambda i: i // cpb)
    ya_s, vch_s = _mixer_a(proj_s, ln_v_g, ln_v_b, wm_s, bsb_s, ns_ // CHUNK, lambda i: i)

    kcmp_p = _compress_prompt(proj_p, T_KC, bp, tp, pe_k, w1k_b, w2k_b, g_k[0], True)
    vcmp_p = _compress_prompt(proj_p, T_VC, bp, tp, pe_v, w1v_b, w2v_b, one_gain, False)
    n_phys, page = cache_k_cmp.shape[1], cache_k_cmp.shape[2]
    bpp = page // CMP_BLOCK
    blk_rows = CMP_BLOCK * N_KV
    kcmp_pool = _compress_pool(cache_k_cmp.reshape(n_phys * bpp, blk_rows, HEAD_DIM), pe8_k, w1k_x, w2k_b, g_k[0], True)
    vcmp_pool = _compress_pool(cache_v_cmp.reshape(n_phys * bpp, blk_rows, HEAD_DIM), pe8_v, w1v_x, w2v_b, one_gain, False)
    npg = page_table.shape[1]
    kcmp_s = kcmp_pool.reshape(n_phys, bpp * SUBLANE * HEAD_DIM)[page_table].reshape(bs, npg * bpp, SUBLANE, HEAD_DIM)
    vcmp_s = vcmp_pool.reshape(n_phys, bpp * SUBLANE * HEAD_DIM)[page_table].reshape(bs, npg * bpp, SUBLANE, HEAD_DIM)

    tables, ctab = _bias_tables(rel_bias, ATT_TILE)
    oc_p, sel_p = _cmp_select(proj_p, ctab, kcmp_p, vcmp_p, bp, tp)
    yb_p = _prompt_attention(proj_p, rel_bias, tables, oc_p, bp, tp, "sel", sel_p)
    yb_p = _prompt_attention(proj_p, rel_bias, tables, yb_p, bp, tp, "win")

    past = npg * page
    emat_s = jnp.asarray(_block_expand(past + LANE), BF16)
    yb_s, kwin_s, vwin_s = _sample_attention(
        proj_s, rel_bias, page_table, kcmp_s, vcmp_s,
        cache_k_sel.reshape(n_phys, page * N_KV, HEAD_DIM), cache_v_sel.reshape(n_phys, page * N_KV, HEAD_DIM),
        state_k_win.reshape(bs, -1, HEAD_DIM), state_v_win.reshape(bs, -1, HEAD_DIM), emat_s, bs, ts)

    t_p = _merge(proj_p, ya_p, yb_p, wa_b, wb_b)
    t_s = _merge(proj_s, ya_s, yb_s, wa_b, wb_b)
    x1_p, h2_p = _out_projection(t_p, x_prompt, mod_p, g_n2, wo_b, 1, min(256, tp))
    x1_s, h2_s = _out_projection(t_s, x_sample, mod_s, g_n2, wo_b, min(256 // ts, bs), ts)

    h2 = jnp.concatenate([h2_p, h2_s], axis=0)
    s1t, s2t = _peer_scores(h2, wpq_b, sk1_b, sk2_b)
    cnt1, e1, rk2, e2 = _peer_topk(s1t, s2t)
    peer = _peer_dense(h2, eu_b, ev_b, cnt1, e1, rk2, e2)
    y_p = _final_residual(x1_p, peer, 0, mod_p, 1, min(512, tp))
    y_s = _final_residual(x1_s, peer, np_, mod_s, min(512 // ts, bs), ts)

    def kv_p(k):
        return proj_p[k].reshape(1, bp, tp, N_KV, HEAD_DIM)

    def kv_s(k):
        return proj_s[k].reshape(1, bs, ts, N_KV, HEAD_DIM)

    wb_p = min(WINDOW, tp)
    wlen = state_k_win.shape[2]
    return (y_p, y_s,
            kv_p(T_KC), kv_p(T_VC), kv_p(T_KS), kv_p(T_VS),
            kv_p(T_KW)[:, :, tp - wb_p:], kv_p(T_VW)[:, :, tp - wb_p:],
            vch_p.reshape(1, bp, CHUNK, -1),
            kv_s(T_KC), kv_s(T_VC), kv_s(T_KS), kv_s(T_VS),
            kwin_s.reshape(1, bs, wlen, N_KV, HEAD_DIM), vwin_s.reshape(1, bs, wlen, N_KV, HEAD_DIM),
            vch_s.reshape(1, bs, ts, -1))


def kernel(x_prompt, x_sample, cache_k_cmp, cache_v_cmp, cache_k_sel, cache_v_sel, state_k_win, state_v_win, page_table, c_prompt, c_sample, rel_bias, w_ada, b_ada, g_n1, g_n2, w_in, ln_v_g, ln_v_b, w_s, b_s, g_q, g_k, pe_k, w_c1k, w_c2k, pe_v, w_c1v, w_c2v, w_a, w_b, w_o, w_pq, sk1, sk2, expert_u, expert_v):
    return _forward(x_prompt, x_sample, cache_k_cmp, cache_v_cmp, cache_k_sel, cache_v_sel, state_k_win, state_v_win,
                    page_table, c_prompt, c_sample, rel_bias, w_ada, b_ada, g_n1, g_n2, w_in, ln_v_g, ln_v_b, w_s, b_s,
                    g_q, g_k, pe_k, w_c1k, w_c2k, pe_v, w_c1v, w_c2v, w_a, w_b, w_o, w_pq, sk1, sk2, expert_u, expert_v)
```

```python
import functools
import math

import numpy as np
import jax
import jax.numpy as jnp
from jax import lax
from jax.experimental import pallas as pl
from jax.experimental.pallas import tpu as pltpu

F32 = jnp.float32
BF16 = jnp.bfloat16

N_HEADS = 16
HEAD_DIM = 128
N_KV = 4
GQA = N_HEADS // N_KV
KV_WIDTH = N_KV * HEAD_DIM
CHUNK = 128
A_GROUPS = 8
CMP_BLOCK = 64
SEL_BLOCK = 64
N_SEL = 16
WINDOW = 512
N_BUCKETS = 32
MAX_DISTANCE = 128
N_KEYS = 128
PEER_HEADS = 8
PEER_TOPK = 16
ATTN_SCALE = HEAD_DIM ** -0.5
NEG = -1e30
FORCE_BONUS = 1e4
EPS = 1e-6
LANE = 128
SUBLANE = 8
PROJ_TILE = 512
ATT_TILE = 256
VMEM_LIMIT = 56 * 1024 * 1024

T_U, T_V, T_Q, T_KC, T_VC, T_KS, T_VS, T_KW, T_VW, T_NSA, T_GA, T_GB, N_TILES = 0, 2, 4, 8, 9, 10, 11, 12, 13, 14, 15, 19, 23


def _bucket_thresholds():
    n = np.arange(0, 2 * MAX_DISTANCE)
    nf = np.maximum(n, 1).astype(np.float32)
    half = N_BUCKETS // 2
    large = half + (np.log(nf / half) / math.log(MAX_DISTANCE / half) * (N_BUCKETS - half)).astype(np.int32)
    b = np.where(n < half, n, np.minimum(large, N_BUCKETS - 1))
    assert np.all(np.diff(b) >= 0) and b[-1] == N_BUCKETS - 1
    return [int(np.argmax(b >= k)) for k in range(N_BUCKETS)]


BUCKET_THR = _bucket_thresholds()
FAR_DIST = BUCKET_THR[-1]
assert FAR_DIST <= MAX_DISTANCE


def _cparams(sem, vmem=VMEM_LIMIT):
    return pltpu.CompilerParams(dimension_semantics=sem, vmem_limit_bytes=vmem)


def _gelu(x):
    c = 2.0 * math.sqrt(2.0 / math.pi)
    u = (x * x) * (-0.044715 * c) - c
    return x * (1.0 / (1.0 + jnp.exp(x * u)))


def _sigmoid(x):
    return 1.0 / (1.0 + jnp.exp(-x))


def _dot_nt(a, b):
    return lax.dot_general(a, b, (((1,), (1,)), ((), ())), preferred_element_type=F32)


def _dot_tn(a, b):
    return lax.dot_general(a, b, (((0,), (0,)), ((), ())), preferred_element_type=F32)


def _bias_chain(dist, rbs):
    b = jnp.full(dist.shape, rbs[0], F32)
    for k in range(1, N_BUCKETS):
        b = jnp.where(dist >= BUCKET_THR[k], rbs[k], b)
    return b


def _mod_kernel(c_ref, w_ref, b_ref, o_ref):
    c = c_ref[...]
    a = (c * _sigmoid(c)).astype(BF16)
    o_ref[...] = jnp.dot(a, w_ref[...].astype(BF16), preferred_element_type=F32) + b_ref[...]


def _modulation(c_all, w_ada, b_ada):
    m, d = c_all.shape
    n = w_ada.shape[1]
    tn = 1024
    return pl.pallas_call(
        _mod_kernel,
        grid=(n // tn,),
        in_specs=[pl.BlockSpec((m, d), lambda j: (0, 0)),
                  pl.BlockSpec((d, tn), lambda j: (0, j)),
                  pl.BlockSpec((1, tn), lambda j: (0, j))],
        out_specs=pl.BlockSpec((m, tn), lambda j: (0, j)),
        out_shape=jax.ShapeDtypeStruct((m, n), F32),
        compiler_params=_cparams(("arbitrary",)),
        name="adaln_mod",
    )(c_all, w_ada, b_ada.reshape(1, n))


def _inproj_kernel(x_ref, sc_ref, sh_ref, gn_ref, w_ref, gain_ref, flag_ref, o_ref, h_scr):
    j = pl.program_id(1)

    @pl.when(j == 0)
    def _():
        x = x_ref[...]
        r = lax.rsqrt(jnp.mean(x * x, axis=-1, keepdims=True) + EPS)
        h = (x * r) * gn_ref[...] * (1.0 + sc_ref[...]) + sh_ref[...]
        h_scr[...] = h.reshape(h_scr.shape).astype(BF16)

    y = jnp.dot(h_scr[...], w_ref[...], preferred_element_type=F32)

    @pl.when(j < T_Q)
    def _():
        o_ref[...] = _gelu(y)

    @pl.when((j >= T_Q) & (j < T_NSA))
    def _():
        parts = []
        for hh in range(PROJ_TILE // HEAD_DIM):
            yh = y[:, hh * HEAD_DIM:(hh + 1) * HEAD_DIM]
            parts.append(yh * lax.rsqrt(jnp.mean(yh * yh, axis=-1, keepdims=True) + EPS))
        yn = jnp.concatenate(parts, axis=1) * gain_ref[...]
        o_ref[...] = jnp.where(flag_ref[...] > 0.5, yn, y)

    @pl.when(j >= T_NSA)
    def _():
        o_ref[...] = _sigmoid(y)


def _in_projection(x3, mod4, g_n1, w_in_p, gain, flag, bt, tt):
    nb, tb, d = x3.shape
    tpb = tb // tt
    tm = bt * tt
    n = nb * tb
    grid = (n // tm, N_TILES)
    return pl.pallas_call(
        _inproj_kernel,
        grid=grid,
        in_specs=[pl.BlockSpec((bt, tt, d), lambda i, j: (i // tpb, i % tpb, 0)),
                  pl.BlockSpec((bt, None, 1, d), lambda i, j: (i // tpb, 1, 0, 0)),
                  pl.BlockSpec((bt, None, 1, d), lambda i, j: (i // tpb, 0, 0, 0)),
                  pl.BlockSpec((1, 1, d), lambda i, j: (0, 0, 0)),
                  pl.BlockSpec((d, PROJ_TILE), lambda i, j: (0, j)),
                  pl.BlockSpec((None, 1, PROJ_TILE), lambda i, j: (j, 0, 0)),
                  pl.BlockSpec((None, 1, PROJ_TILE), lambda i, j: (j, 0, 0))],
        out_specs=pl.BlockSpec((None, tm, PROJ_TILE), lambda i, j: (j, i, 0)),
        out_shape=jax.ShapeDtypeStruct((N_TILES, n, PROJ_TILE), F32),
        scratch_shapes=[pltpu.VMEM((tm, d), BF16)],
        compiler_params=_cparams(("parallel", "arbitrary")),
        name="in_projection",
    )(x3, mod4, mod4, g_n1.reshape(1, 1, d), w_in_p, gain, flag)


def _mixa_kernel(u0_ref, u1_ref, v0_ref, v1_ref, lg_ref, lb_ref, wm_ref, bs_ref, ya_ref, vch_ref):
    v = jnp.concatenate([v0_ref[...], v1_ref[...]], axis=1)
    mu = jnp.mean(v, axis=-1, keepdims=True)
    var = jnp.mean(jnp.square(v - mu), axis=-1, keepdims=True)
    vln = ((v - mu) * lax.rsqrt(var + EPS)) * lg_ref[...] + lb_ref[...]
    vch_ref[...] = vln
    u = jnp.concatenate([u0_ref[...], u1_ref[...]], axis=1)
    vb = vln.astype(BF16)
    gd = vln.shape[1] // A_GROUPS
    for g in range(A_GROUPS):
        sl = slice(g * gd, (g + 1) * gd)
        s = jnp.dot(wm_ref[g], vb[:, sl], preferred_element_type=F32) + bs_ref[g]
        ya_ref[:, sl] = (u[:, sl] * s).astype(BF16)


def _mixer_a(proj, ln_g, ln_b, wm, bsb, vch_blocks, vch_map):
    n = proj.shape[1]
    aw = 2 * PROJ_TILE

    def tile(k):
        return pl.BlockSpec((None, CHUNK, PROJ_TILE), lambda i, k=k: (k, i, 0))

    return pl.pallas_call(
        _mixa_kernel,
        grid=(n // CHUNK,),
        in_specs=[tile(T_U), tile(T_U + 1), tile(T_V), tile(T_V + 1),
                  pl.BlockSpec((1, aw), lambda i: (0, 0)),
                  pl.BlockSpec((1, aw), lambda i: (0, 0)),
                  pl.BlockSpec((A_GROUPS, CHUNK, CHUNK), lambda i: (0, 0, 0)),
                  pl.BlockSpec((A_GROUPS, CHUNK, CHUNK), lambda i: (0, 0, 0))],
        out_specs=[pl.BlockSpec((CHUNK, aw), lambda i: (i, 0)),
                   pl.BlockSpec((CHUNK, aw), lambda i: (vch_map(i), 0))],
        out_shape=[jax.ShapeDtypeStruct((n, aw), BF16),
                   jax.ShapeDtypeStruct((vch_blocks * CHUNK, aw), F32)],
        compiler_params=_cparams(("arbitrary",)),
        name="mixer_a",
    )(proj, proj, proj, proj, ln_g.reshape(1, aw), ln_b.reshape(1, aw), wm, bsb)


def _compress_tail(hid, w2_ref, gain_ref, do_rms):
    out = jnp.dot(_gelu(hid).astype(BF16), w2_ref[...], preferred_element_type=F32)
    if do_rms:
        out = out * lax.rsqrt(jnp.mean(out * out, axis=-1, keepdims=True) + EPS) * gain_ref[...]
    return out


def _compress_prompt_kernel(x0_ref, x1_ref, x2_ref, x3_ref, pe_ref, w1_ref, w2_ref, gain_ref, o_ref, lhs_scr, *, nb, do_rms):
    for s_ in range(CMP_BLOCK):
        for g, x_ref in enumerate((x0_ref, x1_ref, x2_ref, x3_ref)):
            rows = x_ref[pl.ds(s_, nb, stride=CMP_BLOCK), :]
            lhs_scr[g * nb:(g + 1) * nb, s_ * HEAD_DIM:(s_ + 1) * HEAD_DIM] = (rows + pe_ref[s_:s_ + 1, :]).astype(BF16)
    hid = jnp.dot(lhs_scr[...], w1_ref[...], preferred_element_type=F32)
    out = _compress_tail(hid, w2_ref, gain_ref, do_rms)
    for g in range(N_KV):
        o_ref[:, g * HEAD_DIM:(g + 1) * HEAD_DIM] = out[g * nb:(g + 1) * nb]


def _compress_prompt(proj, tile, bsz, t, pe, w1b, w2b, gain, do_rms):
    nb = t // CMP_BLOCK
    hid = w1b.shape[1]
    return pl.pallas_call(
        functools.partial(_compress_prompt_kernel, nb=nb, do_rms=do_rms),
        grid=(bsz,),
        in_specs=[pl.BlockSpec((None, t, HEAD_DIM), lambda b, g=g: (tile, b, g)) for g in range(N_KV)] + [
                  pl.BlockSpec((CMP_BLOCK, HEAD_DIM), lambda b: (0, 0)),
                  pl.BlockSpec((CMP_BLOCK * HEAD_DIM, hid), lambda b: (0, 0)),
                  pl.BlockSpec((hid, HEAD_DIM), lambda b: (0, 0)),
                  pl.BlockSpec((1, HEAD_DIM), lambda b: (0, 0))],
        out_specs=pl.BlockSpec((nb, KV_WIDTH), lambda b: (b, 0)),
        out_shape=jax.ShapeDtypeStruct((bsz * nb, KV_WIDTH), F32),
        scratch_shapes=[pltpu.VMEM((N_KV * nb, CMP_BLOCK * HEAD_DIM), BF16)],
        compiler_params=_cparams(("arbitrary",)),
        name="compress_prompt",
    )(proj, proj, proj, proj, pe, w1b, w2b, gain.reshape(1, HEAD_DIM))


def _compress_pool_kernel(x_ref, pe8_ref, w1_ref, w2_ref, gain_ref, o_ref, lhs_scr, *, do_rms):
    tb = x_ref.shape[0]
    m = tb * SUBLANE
    hid = w2_ref.shape[0]
    for j in range(CMP_BLOCK // 2):
        xj = x_ref[:, SUBLANE * j:SUBLANE * (j + 1), :] + pe8_ref[j]
        lhs_scr[:, j * HEAD_DIM:(j + 1) * HEAD_DIM] = xj.reshape(m, HEAD_DIM).astype(BF16)
    acc = jnp.dot(lhs_scr[...], w1_ref[...], preferred_element_type=F32)
    hidv = acc[:, :hid] + pltpu.roll(acc[:, hid:], m - N_KV, 0)
    out = _compress_tail(hidv, w2_ref, gain_ref, do_rms)
    o_ref[...] = out.reshape(tb, SUBLANE, HEAD_DIM)


def _compress_pool(x3, pe8, w1x, w2b, gain, do_rms, tb=64):
    nblk = x3.shape[0]
    tb = min(tb, nblk)
    hid = w2b.shape[0]
    kdim = CMP_BLOCK // 2 * HEAD_DIM
    return pl.pallas_call(
        functools.partial(_compress_pool_kernel, do_rms=do_rms),
        grid=(nblk // tb,),
        in_specs=[pl.BlockSpec((tb, CMP_BLOCK * N_KV, HEAD_DIM), lambda i: (i, 0, 0)),
                  pl.BlockSpec((CMP_BLOCK // 2, SUBLANE, HEAD_DIM), lambda i: (0, 0, 0)),
                  pl.BlockSpec((kdim, 2 * hid), lambda i: (0, 0)),
                  pl.BlockSpec((hid, HEAD_DIM), lambda i: (0, 0)),
                  pl.BlockSpec((1, HEAD_DIM), lambda i: (0, 0))],
        out_specs=pl.BlockSpec((tb, SUBLANE, HEAD_DIM), lambda i: (i, 0, 0)),
        out_shape=jax.ShapeDtypeStruct((nblk, SUBLANE, HEAD_DIM), F32),
        scratch_shapes=[pltpu.VMEM((tb * SUBLANE, kdim), BF16)],
        compiler_params=_cparams(("arbitrary",)),
        name="compress_pool",
    )(x3, pe8, w1x, w2b, gain.reshape(1, HEAD_DIM))


def _bias_table_kernel(rb_ref, o_ref, c_ref, *, ts):
    g = pl.program_id(0)
    i = lax.broadcasted_iota(jnp.int32, (ts, ts), 0)
    j = lax.broadcasted_iota(jnp.int32, (ts, ts), 1)
    for d in range(2):
        dist = d * ts + i - j
        for r in range(GQA):
            rbs = [rb_ref[k, g * GQA + r] for k in range(N_BUCKETS)]
            o_ref[d, r * ts:(r + 1) * ts, :] = _bias_chain(dist, rbs) - rbs[-1]
    ic = lax.broadcasted_iota(jnp.int32, (ts, LANE), 0)
    nc = lax.broadcasted_iota(jnp.int32, (ts, LANE), 1) - LANE // 2
    dist_c = ic - (nc * CMP_BLOCK + CMP_BLOCK - 1)
    for r in range(GQA):
        rbs = [rb_ref[k, g * GQA + r] for k in range(N_BUCKETS)]
        c_ref[r * ts:(r + 1) * ts, :] = _bias_chain(dist_c, rbs)


def _bias_tables(rel_bias, ts):
    return pl.pallas_call(
        functools.partial(_bias_table_kernel, ts=ts),
        grid=(N_KV,),
        in_specs=[pl.BlockSpec(memory_space=pltpu.SMEM)],
        out_specs=[pl.BlockSpec((None, 2, GQA * ts, ts), lambda g: (g, 0, 0, 0)),
                   pl.BlockSpec((None, GQA * ts, LANE), lambda g: (g, 0, 0))],
        out_shape=[jax.ShapeDtypeStruct((N_KV, 2, GQA * ts, ts), F32),
                   jax.ShapeDtypeStruct((N_KV, GQA * ts, LANE), F32)],
        compiler_params=_cparams(("arbitrary",)),
        name="bias_tables",
    )(rel_bias)


def _rank_select(score, n_sel):
    t = score.shape[1]
    ngrp = -(-n_sel // SUBLANE)
    jrow = lax.broadcasted_iota(jnp.int32, (SUBLANE, t), 0)
    sel = []
    for gb in range(ngrp):
        blk = score[gb * SUBLANE:(gb + 1) * SUBLANE, :]
        rank = jnp.zeros((SUBLANE, t), F32)
        for i in range(n_sel):
            row = score[i:i + 1, :]
            if i < gb * SUBLANE:
                beats = row >= blk
            elif i >= (gb + 1) * SUBLANE:
                beats = row > blk
            else:
                beats = (row > blk) | ((jrow > i - gb * SUBLANE) & (row == blk))
            rank = rank + jnp.where(beats, 1.0, 0.0)
        keep = (rank < float(min(N_SEL, n_sel))) & (jrow + gb * SUBLANE < n_sel)
        sel.append(jnp.where(keep, 1.0, 0.0))
    sel.append(jnp.zeros((score.shape[0] - ngrp * SUBLANE, t), F32))
    return jnp.concatenate(sel, axis=0)


def _cmp_kernel(ct_ref, q0_ref, q1_ref, q2_ref, q3_ref, nsa_ref, kc_ref, vc_ref, oc_ref, sel_ref, *, tq, nb, n_sel):
    qt = pl.program_id(1)
    nsa = nsa_ref[...]
    row = lax.broadcasted_iota(jnp.int32, (tq, LANE), 0) + qt * tq
    col = lax.broadcasted_iota(jnp.int32, (tq, LANE), 1)
    dist = row - (col * CMP_BLOCK + CMP_BLOCK - 1)
    valid = (dist >= 0) & (col < nb)
    cur = row // SEL_BLOCK
    forced = (col == 0) | (col == cur) | (col == cur - 1)
    shift = (qt * (tq // CMP_BLOCK) + LANE // 2) % LANE
    pad = jnp.zeros((LANE - nb, HEAD_DIM), F32)
    for g in range(N_KV):
        q = (q0_ref, q1_ref, q2_ref, q3_ref)[g][...]
        qst = jnp.concatenate([q[:, r * HEAD_DIM:(r + 1) * HEAD_DIM] for r in range(GQA)], axis=0).astype(BF16)
        kg = jnp.concatenate([kc_ref[:, g * HEAD_DIM:(g + 1) * HEAD_DIM], pad], axis=0).astype(BF16)
        vg = jnp.concatenate([vc_ref[:, g * HEAD_DIM:(g + 1) * HEAD_DIM], pad], axis=0).astype(BF16)
        s = _dot_nt(qst, kg) * ATTN_SCALE
        ps = []
        imp = jnp.zeros((tq, LANE), F32)
        for r in range(GQA):
            h = g * GQA + r
            b = pltpu.roll(ct_ref[g, r * tq:(r + 1) * tq, :], shift, 1)
            sr = jnp.where(valid, s[r * tq:(r + 1) * tq] + b, NEG)
            m = jnp.max(sr, axis=-1, keepdims=True)
            p = jnp.where(valid, jnp.exp(sr - m), 0.0)
            den = jnp.sum(p, axis=-1, keepdims=True)
            p = p * (1.0 / jnp.maximum(den, 1e-30))
            imp = imp + p
            ps.append(p)
        o = jnp.dot(jnp.concatenate(ps, axis=0).astype(BF16), vg, preferred_element_type=F32)
        for r in range(GQA):
            h = g * GQA + r
            oc_ref[:, h * HEAD_DIM:(h + 1) * HEAD_DIM] = o[r * tq:(r + 1) * tq] * nsa[:, 3 * h:3 * h + 1]
        score = jnp.where(col <= cur, imp + jnp.where(forced, FORCE_BONUS, 0.0), NEG)
        score = jnp.where(col < n_sel, score, -3e38)
        sel_ref[g] = _rank_select(score.T, n_sel).T


def _cmp_select(proj, ctab, kcmp, vcmp, bsz, t, tq=ATT_TILE):
    nb = kcmp.shape[0] // bsz
    n_sel = -(-t // SEL_BLOCK)
    nq = t // tq
    n = proj.shape[1]
    assert nb <= LANE // 2 and SEL_BLOCK == CMP_BLOCK

    def tile(k):
        return pl.BlockSpec((None, tq, PROJ_TILE), lambda b, i, k=k: (k, b * nq + i, 0))

    return pl.pallas_call(
        functools.partial(_cmp_kernel, tq=tq, nb=nb, n_sel=n_sel),
        grid=(bsz, nq),
        in_specs=[pl.BlockSpec((N_KV, GQA * tq, LANE), lambda b, i: (0, 0, 0)),
                  tile(T_Q), tile(T_Q + 1), tile(T_Q + 2), tile(T_Q + 3), tile(T_NSA),
                  pl.BlockSpec((nb, KV_WIDTH), lambda b, i: (b, 0)),
                  pl.BlockSpec((nb, KV_WIDTH), lambda b, i: (b, 0))],
        out_specs=[pl.BlockSpec((tq, N_HEADS * HEAD_DIM), lambda b, i: (b * nq + i, 0)),
                   pl.BlockSpec((None, N_KV, tq, LANE), lambda b, i: (b, 0, i, 0))],
        out_shape=[jax.ShapeDtypeStruct((n, N_HEADS * HEAD_DIM), F32),
                   jax.ShapeDtypeStruct((bsz, N_KV, t, LANE), F32)],
        compiler_params=_cparams(("parallel", "arbitrary")),
        name="cmp_select",
    )(ctab, proj, proj, proj, proj, proj, kcmp, vcmp)


MASK_BIG = 2.0 ** 100
AUG = 2 * HEAD_DIM
ROW_BLOCK = 128


def _attn_kernel(rb_ref, q_ref, nsa_ref, k_ref, v_ref, tb_ref, prev_ref, *rest, mode, tq, branch):
    if mode == "sel":
        sel_ref, o_ref, kb, vb, qa, s_scr, p_scr, m_s, a_s, acc_s = rest
    else:
        o_ref, kb, vb, qa, s_scr, p_scr, m_s, a_s, acc_s = rest
    g = pl.program_id(1)
    qt = pl.program_id(2)
    tk = tq
    rows4 = GQA * tq
    t_all = kb.shape[0]

    @pl.when(qt == 0)
    def _():
        krow = lax.broadcasted_iota(jnp.int32, (t_all, LANE), 0)
        lane = lax.broadcasted_iota(jnp.int32, (t_all, LANE), 1)
        onehot = ((lane < SEL_BLOCK) & (krow // SEL_BLOCK == lane)) | (lane == SEL_BLOCK) | (lane == SEL_BLOCK + 1)
        kb[:, 0:HEAD_DIM] = k_ref[...].astype(BF16)
        kb[:, HEAD_DIM:AUG] = onehot.astype(BF16)
        vb[:, 0:HEAD_DIM] = v_ref[...].astype(BF16)
        vb[:, HEAD_DIM:AUG] = (lane == 0).astype(BF16)

    q = q_ref[...]
    lane_q = lax.broadcasted_iota(jnp.int32, (tq, LANE), 1)
    if mode == "sel":
        selm = jnp.where(lane_q < SEL_BLOCK, (sel_ref[...] - 1.0) * MASK_BIG, 0.0)
    else:
        selm = jnp.zeros((tq, LANE), F32)
    for r in range(GQA):
        b_far = jnp.full((tq, LANE), rb_ref[N_BUCKETS - 1, g * GQA + r], F32)
        b_hi = b_far.astype(BF16).astype(F32)
        ext = jnp.where(lane_q == SEL_BLOCK, b_hi, jnp.where(lane_q == SEL_BLOCK + 1, b_far - b_hi, selm))
        qa[r * tq:(r + 1) * tq, 0:HEAD_DIM] = (q[:, r * HEAD_DIM:(r + 1) * HEAD_DIM] * ATTN_SCALE).astype(BF16)
        qa[r * tq:(r + 1) * tq, HEAD_DIM:AUG] = ext.astype(BF16)
    m_s[...] = jnp.full(m_s.shape, NEG, F32)
    acc_s[...] = jnp.zeros(acc_s.shape, F32)
    nrb = rows4 // ROW_BLOCK
    rowpos = lax.broadcasted_iota(jnp.int32, (ROW_BLOCK, tk), 0)
    colpos = lax.broadcasted_iota(jnp.int32, (ROW_BLOCK, tk), 1)

    def chunk(kt, table, mask_kind):
        k0 = pl.multiple_of(kt * tk, tk)
        s_scr[...] = _dot_nt(qa[...], kb[pl.ds(k0, tk), :])
        for rb in range(nrb):
            rs = slice(rb * ROW_BLOCK, (rb + 1) * ROW_BLOCK)
            s = s_scr[rs, :]
            if table is not None:
                s = s + tb_ref[table, rs, :]
            if mask_kind is not None:
                rp = rowpos + (rb * ROW_BLOCK) % tq
                keep = (rp >= colpos) if mask_kind == "causal" else (colpos > rp)
                s = jnp.where(keep, s, NEG)
            m_old = m_s[rs, :]
            m_new = jnp.maximum(m_old, jnp.max(s, axis=-1, keepdims=True))
            p_scr[rs, :] = jnp.exp(s - jnp.concatenate([m_new] * (tk // LANE), axis=1)).astype(BF16)
            a_s[rs, :] = jnp.exp(m_old - m_new)
            m_s[rs, :] = m_new
        pv = jnp.dot(p_scr[...], vb[pl.ds(k0, tk), :], preferred_element_type=F32)
        a = a_s[...]
        acc_s[...] = jnp.concatenate([a] * (AUG // LANE), axis=1) * acc_s[...] + pv

    chunk(qt, 0, "causal")

    @pl.when(qt >= 1)
    def _():
        chunk(qt - 1, 1, None)

    if mode == "sel":
        def far_body(kt, carry):
            chunk(kt, None, None)
            return carry
        lax.fori_loop(0, jnp.maximum(qt - 1, 0), far_body, 0)
    else:
        @pl.when(qt >= 2)
        def _():
            chunk(qt - 2, None, "window")

    acc = acc_s[...]
    o = acc[:, 0:HEAD_DIM] * (1.0 / jnp.maximum(acc[:, HEAD_DIM:HEAD_DIM + 1], 1e-30))
    nsa = nsa_ref[...]
    lane = lax.broadcasted_iota(jnp.int32, nsa.shape, 1)
    for r in range(GQA):
        gidx = (g * GQA + r) * 3 + branch
        gate = jnp.sum(jnp.where(lane == gidx, nsa, 0.0), axis=-1, keepdims=True)
        hs = slice(r * HEAD_DIM, (r + 1) * HEAD_DIM)
        o_ref[:, hs] = prev_ref[:, hs] + o[r * tq:(r + 1) * tq] * gate


def _prompt_attention(proj, rel_bias, tables, prev, bsz, t, mode, sel=None, tq=ATT_TILE):
    nq = t // tq
    n = proj.shape[1]
    assert t // SEL_BLOCK <= SEL_BLOCK and tq % ROW_BLOCK == 0
    if mode == "sel":
        tk_, tv_, branch = T_KS, T_VS, 1
    else:
        tk_, tv_, branch = T_KW, T_VW, 2
        assert WINDOW == 2 * tq
    in_specs = [pl.BlockSpec(memory_space=pltpu.SMEM),
                pl.BlockSpec((None, tq, PROJ_TILE), lambda b, g, i: (T_Q + g, b * nq + i, 0)),
                pl.BlockSpec((None, tq, PROJ_TILE), lambda b, g, i: (T_NSA, b * nq + i, 0)),
                pl.BlockSpec((None, t, HEAD_DIM), lambda b, g, i: (tk_, b, g)),
                pl.BlockSpec((None, t, HEAD_DIM), lambda b, g, i: (tv_, b, g)),
                pl.BlockSpec((None, 2, GQA * tq, tq), lambda b, g, i: (g, 0, 0, 0)),
                pl.BlockSpec((tq, GQA * HEAD_DIM), lambda b, g, i: (b * nq + i, g))]
    args = [rel_bias, proj, proj, proj, proj, tables, prev]
    if mode == "sel":
        in_specs += [pl.BlockSpec((None, None, tq, LANE), lambda b, g, i: (b, g, i, 0))]
        args += [sel]
    return pl.pallas_call(
        functools.partial(_attn_kernel, mode=mode, tq=tq, branch=branch),
        grid=(bsz, N_KV, nq),
        in_specs=in_specs,
        out_specs=pl.BlockSpec((tq, GQA * HEAD_DIM), lambda b, g, i: (b * nq + i, g)),
        out_shape=jax.ShapeDtypeStruct((n, N_HEADS * HEAD_DIM), F32),
        scratch_shapes=[pltpu.VMEM((t, AUG), BF16), pltpu.VMEM((t, AUG), BF16),
                        pltpu.VMEM((GQA * tq, AUG), BF16),
                        pltpu.VMEM((GQA * tq, tq), F32), pltpu.VMEM((GQA * tq, tq), BF16),
                        pltpu.VMEM((GQA * tq, LANE), F32), pltpu.VMEM((GQA * tq, LANE), F32),
                        pltpu.VMEM((GQA * tq, AUG), F32)],
        compiler_params=_cparams(("parallel", "parallel", "arbitrary")),
        name="attn_" + mode,
    )(*args)


def _masked_softmax(s, mask):
    s = jnp.where(mask, s, NEG)
    m = jnp.max(s, axis=-1, keepdims=True)
    p = jnp.where(mask, jnp.exp(s - m), 0.0)
    den = jnp.sum(p, axis=-1, keepdims=True)
    return p * (1.0 / jnp.maximum(den, 1e-30))


def _near_far_bias(rb_ref, g, dist_near, tdec, width, near):
    rows = []
    for r in range(GQA):
        h = g * GQA + r
        rbs = [rb_ref[k, h] for k in range(N_BUCKETS)]
        nb_ = _bias_chain(dist_near, rbs)
        rows.append(jnp.concatenate([jnp.full((tdec, width - near), rbs[-1], F32), nb_], axis=1))
    return jnp.concatenate(rows, axis=0)


def _sattn_kernel(pt_ref, rb_ref, q0_ref, q1_ref, q2_ref, q3_ref, ksn_ref, vsn_ref, kwn_ref, vwn_ref, nsa_ref,
                  kc_ref, vc_ref, *rest, npg, page, tdec, wlen):
    kpages = rest[:npg]
    vpages = rest[npg:2 * npg]
    skw_ref, svw_ref, e_ref, yb_ref, kwo_ref, vwo_ref, kbuf, vbuf, wkb, wvb = rest[2 * npg:]
    past = npg * page
    lk = past + LANE
    nb = kc_ref.shape[0]
    n_sel = -(-(past + tdec) // SEL_BLOCK)
    rows_w = wlen * N_KV
    near = 2 * LANE
    rq = GQA * tdec
    nsa = nsa_ref[...]

    kwo_ref[0:rows_w - tdec * N_KV, :] = skw_ref[tdec * N_KV:rows_w, :]
    vwo_ref[0:rows_w - tdec * N_KV, :] = svw_ref[tdec * N_KV:rows_w, :]
    for g in range(N_KV):
        kwo_ref[pl.ds(rows_w - tdec * N_KV + g, tdec, stride=N_KV), :] = kwn_ref[:, g * HEAD_DIM:(g + 1) * HEAD_DIM]
        vwo_ref[pl.ds(rows_w - tdec * N_KV + g, tdec, stride=N_KV), :] = vwn_ref[:, g * HEAD_DIM:(g + 1) * HEAD_DIM]

    trow1 = lax.broadcasted_iota(jnp.int32, (tdec, LANE), 0) + past
    col1 = lax.broadcasted_iota(jnp.int32, (tdec, LANE), 1)
    dist_c = trow1 - (col1 * CMP_BLOCK + CMP_BLOCK - 1)
    valid_c = (dist_c >= 0) & (col1 < nb)
    cur = trow1 // SEL_BLOCK
    forced = (col1 == 0) | (col1 == cur) | (col1 == cur - 1)

    trow_s = (lax.broadcasted_iota(jnp.int32, (rq, lk), 0) & (tdec - 1)) + past
    pos_s = lax.broadcasted_iota(jnp.int32, (rq, lk), 1)
    causal_s = pos_s <= trow_s
    dist_sn = (lax.broadcasted_iota(jnp.int32, (tdec, near), 0) + past) - (lax.broadcasted_iota(jnp.int32, (tdec, near), 1) + lk - near)

    wl = wlen + LANE
    qidx_w = (lax.broadcasted_iota(jnp.int32, (rq, wl), 0) & (tdec - 1)) + wlen
    kidx_w = lax.broadcasted_iota(jnp.int32, (rq, wl), 1)
    dist_w = qidx_w - kidx_w
    mask_w = (dist_w >= 0) & (dist_w < WINDOW)
    dist_wn = (lax.broadcasted_iota(jnp.int32, (tdec, near), 0) + wlen) - (lax.broadcasted_iota(jnp.int32, (tdec, near), 1) + wl - near)

    zpad = jnp.zeros((LANE - tdec, HEAD_DIM), F32)
    cpad = jnp.zeros((LANE - nb, HEAD_DIM), F32)
    for g in range(N_KV):
        gs = slice(g * HEAD_DIM, (g + 1) * HEAD_DIM)
        q = (q0_ref, q1_ref, q2_ref, q3_ref)[g][...]
        qst = jnp.concatenate([q[:, r * HEAD_DIM:(r + 1) * HEAD_DIM] for r in range(GQA)], axis=0).astype(BF16)

        kg = jnp.concatenate([kc_ref[:, g, :], cpad], axis=0).astype(BF16)
        vg = jnp.concatenate([vc_ref[:, g, :], cpad], axis=0).astype(BF16)
        s = _dot_nt(qst, kg) * ATTN_SCALE
        ps = []
        imp = jnp.zeros((tdec, LANE), F32)
        for r in range(GQA):
            h = g * GQA + r
            b = _bias_chain(dist_c, [rb_ref[k, h] for k in range(N_BUCKETS)])
            p = _masked_softmax(s[r * tdec:(r + 1) * tdec] + b, valid_c)
            imp = imp + p
            ps.append(p)
        o_c = jnp.dot(jnp.concatenate(ps, axis=0).astype(BF16), vg, preferred_element_type=F32)

        score = jnp.where(col1 <= cur, imp + jnp.where(forced, FORCE_BONUS, 0.0), NEG)
        score = jnp.where(col1 < n_sel, score, -3e38)
        rank = jnp.zeros((tdec, LANE), F32)
        for i in range(n_sel):
            ci = score[:, i:i + 1]
            rank = rank + ((ci > score) | ((ci == score) & (col1 > i))).astype(F32)
        sel = ((rank < float(min(N_SEL, n_sel))) & (col1 < n_sel)).astype(F32)

        for p_ in range(npg):
            kbuf[p_ * page:(p_ + 1) * page, :] = kpages[p_][pl.ds(g, page, stride=N_KV), :].astype(BF16)
            vbuf[p_ * page:(p_ + 1) * page, :] = vpages[p_][pl.ds(g, page, stride=N_KV), :].astype(BF16)
        kbuf[past:lk, :] = jnp.concatenate([ksn_ref[:, gs], zpad], axis=0).astype(BF16)
        vbuf[past:lk, :] = jnp.concatenate([vsn_ref[:, gs], zpad], axis=0).astype(BF16)
        s = _dot_nt(qst, kbuf[...]) * ATTN_SCALE + _near_far_bias(rb_ref, g, dist_sn, tdec, lk, near)
        sel4 = jnp.concatenate([sel] * GQA, axis=0).astype(BF16)
        mask = (jnp.dot(sel4, e_ref[...], preferred_element_type=F32) > 0.5) & causal_s
        o_s = jnp.dot(_masked_softmax(s, mask).astype(BF16), vbuf[...], preferred_element_type=F32)

        wkb[0:wlen, :] = skw_ref[pl.ds(g, wlen, stride=N_KV), :].astype(BF16)
        wvb[0:wlen, :] = svw_ref[pl.ds(g, wlen, stride=N_KV), :].astype(BF16)
        wkb[wlen:wl, :] = jnp.concatenate([kwn_ref[:, gs], zpad], axis=0).astype(BF16)
        wvb[wlen:wl, :] = jnp.concatenate([vwn_ref[:, gs], zpad], axis=0).astype(BF16)
        s = _dot_nt(qst, wkb[...]) * ATTN_SCALE + _near_far_bias(rb_ref, g, dist_wn, tdec, wl, near)
        o_w = jnp.dot(_masked_softmax(s, mask_w).astype(BF16), wvb[...], preferred_element_type=F32)

        for r in range(GQA):
            h = g * GQA + r
            rs = slice(r * tdec, (r + 1) * tdec)
            yb_ref[:, h * HEAD_DIM:(h + 1) * HEAD_DIM] = (nsa[:, 3 * h:3 * h + 1] * o_c[rs]
                                                         + nsa[:, 3 * h + 1:3 * h + 2] * o_s[rs]
                                                         + nsa[:, 3 * h + 2:3 * h + 3] * o_w[rs])


def _sample_attention(proj, rel_bias, page_table, kcmp_g, vcmp_g, ck_sel, cv_sel, skw, svw, emat, bsz, tdec):
    npg = page_table.shape[1]
    page = ck_sel.shape[1] // N_KV
    wlen = skw.shape[1] // N_KV
    nb = kcmp_g.shape[1]
    past = npg * page
    lk = past + LANE
    assert tdec == SUBLANE and FAR_DIST <= LANE and wlen == WINDOW

    def tile(k):
        return pl.BlockSpec((None, tdec, PROJ_TILE), lambda b, pt, k=k: (k, b, 0))

    def pagespec(p_):
        return pl.BlockSpec((None, page * N_KV, HEAD_DIM), lambda b, pt, p_=p_: (pt[b, p_], 0, 0))

    in_specs = ([pl.BlockSpec(memory_space=pltpu.SMEM)]
                + [tile(T_Q + g) for g in range(N_KV)]
                + [tile(T_KS), tile(T_VS), tile(T_KW), tile(T_VW), tile(T_NSA)]
                + [pl.BlockSpec((None, nb, SUBLANE, HEAD_DIM), lambda b, pt: (b, 0, 0, 0))] * 2
                + [pagespec(p_) for p_ in range(npg)] * 2
                + [pl.BlockSpec((None, wlen * N_KV, HEAD_DIM), lambda b, pt: (b, 0, 0))] * 2
                + [pl.BlockSpec((LANE, lk), lambda b, pt: (0, 0))])
    grid_spec = pltpu.PrefetchScalarGridSpec(
        num_scalar_prefetch=1,
        grid=(bsz,),
        in_specs=in_specs,
        out_specs=[pl.BlockSpec((tdec, N_HEADS * HEAD_DIM), lambda b, pt: (b, 0)),
                   pl.BlockSpec((None, wlen * N_KV, HEAD_DIM), lambda b, pt: (b, 0, 0)),
                   pl.BlockSpec((None, wlen * N_KV, HEAD_DIM), lambda b, pt: (b, 0, 0))],
        scratch_shapes=[pltpu.VMEM((lk, HEAD_DIM), BF16), pltpu.VMEM((lk, HEAD_DIM), BF16),
                        pltpu.VMEM((wlen + LANE, HEAD_DIM), BF16), pltpu.VMEM((wlen + LANE, HEAD_DIM), BF16)])
    return pl.pallas_call(
        functools.partial(_sattn_kernel, npg=npg, page=page, tdec=tdec, wlen=wlen),
        grid_spec=grid_spec,
        out_shape=[jax.ShapeDtypeStruct((bsz * tdec, N_HEADS * HEAD_DIM), F32),
                   jax.ShapeDtypeStruct((bsz, wlen * N_KV, HEAD_DIM), F32),
                   jax.ShapeDtypeStruct((bsz, wlen * N_KV, HEAD_DIM), F32)],
        compiler_params=_cparams(("arbitrary",)),
        name="sample_attention",
    )(page_table, rel_bias, *([proj] * 9), kcmp_g, vcmp_g, *([ck_sel] * npg), *([cv_sel] * npg), skw, svw, emat)


def _merge_kernel(ya_ref, yb_ref, wa_ref, wb_ref, ga_ref, gb_ref, t_ref, yb_scr):
    @pl.when(pl.program_id(1) == 0)
    def _():
        yb_scr[...] = yb_ref[...].astype(BF16)

    a = jnp.dot(ya_ref[...], wa_ref[...], preferred_element_type=F32)
    b = jnp.dot(yb_scr[...], wb_ref[...], preferred_element_type=F32)
    t_ref[...] = (ga_ref[...] * a + gb_ref[...] * b).astype(BF16)


def _merge(proj, ya, yb, wa_b, wb_b, tm=1024):
    n, aw = ya.shape
    d = wb_b.shape[0]
    tm = min(tm, n)
    nj = d // PROJ_TILE
    return pl.pallas_call(
        _merge_kernel,
        grid=(n // tm, nj),
        in_specs=[pl.BlockSpec((tm, aw), lambda i, j: (i, 0)),
                  pl.BlockSpec((tm, d), lambda i, j: (i, 0)),
                  pl.BlockSpec((aw, PROJ_TILE), lambda i, j: (0, j)),
                  pl.BlockSpec((d, PROJ_TILE), lambda i, j: (0, j)),
                  pl.BlockSpec((None, tm, PROJ_TILE), lambda i, j: (T_GA + j, i, 0)),
                  pl.BlockSpec((None, tm, PROJ_TILE), lambda i, j: (T_GB + j, i, 0))],
        out_specs=pl.BlockSpec((tm, PROJ_TILE), lambda i, j: (i, j)),
        out_shape=jax.ShapeDtypeStruct((n, d), BF16),
        scratch_shapes=[pltpu.VMEM((tm, d), BF16)],
        compiler_params=_cparams(("parallel", "arbitrary")),
        name="merge",
    )(ya, yb, wa_b, wb_b, proj, proj)


def _outproj_kernel(t_ref, x_ref, gt_ref, sc_ref, sh_ref, gn_ref, wo_ref, x1_ref, h2_ref):
    y = jnp.dot(t_ref[...], wo_ref[...], preferred_element_type=F32)
    x1 = x_ref[...] + gt_ref[...] * y.reshape(x_ref.shape)
    x1_ref[...] = x1
    r = lax.rsqrt(jnp.mean(x1 * x1, axis=-1, keepdims=True) + EPS)
    h2 = (x1 * r) * gn_ref[...] * (1.0 + sc_ref[...]) + sh_ref[...]
    h2_ref[...] = h2.reshape(h2_ref.shape).astype(BF16)


def _out_projection(tmix, x3, mod4, g_n2, wo_b, bt, tt):
    nb, tb, d = x3.shape
    tpb = tb // tt
    tm = bt * tt
    n = nb * tb

    def modspec(k):
        return pl.BlockSpec((bt, None, 1, d), lambda i, k=k: (i // tpb, k, 0, 0))

    return pl.pallas_call(
        _outproj_kernel,
        grid=(n // tm,),
        in_specs=[pl.BlockSpec((tm, d), lambda i: (i, 0)),
                  pl.BlockSpec((bt, tt, d), lambda i: (i // tpb, i % tpb, 0)),
                  modspec(2), modspec(4), modspec(3),
                  pl.BlockSpec((1, 1, d), lambda i: (0, 0, 0)),
                  pl.BlockSpec((d, d), lambda i: (0, 0))],
        out_specs=[pl.BlockSpec((bt, tt, d), lambda i: (i // tpb, i % tpb, 0)),
                   pl.BlockSpec((tm, d), lambda i: (i, 0))],
        out_shape=[jax.ShapeDtypeStruct((nb, tb, d), F32), jax.ShapeDtypeStruct((n, d), BF16)],
        compiler_params=_cparams(("arbitrary",)),
        name="out_projection",
    )(tmix, x3, mod4, mod4, mod4, g_n2.reshape(1, 1, d), wo_b)


def _peer_scores_kernel(h_ref, wpq_ref, sk1_ref, sk2_ref, s1_ref, s2_ref):
    pq = jnp.dot(h_ref[...], wpq_ref[...], preferred_element_type=F32)
    kd = sk1_ref.shape[1]
    for hd in range(PEER_HEADS):
        q1 = pq[:, hd * 2 * kd:hd * 2 * kd + kd].astype(BF16)
        q2 = pq[:, hd * 2 * kd + kd:(hd + 1) * 2 * kd].astype(BF16)
        s1_ref[hd] = _dot_nt(sk1_ref[...], q1)
        s2_ref[hd] = _dot_nt(sk2_ref[...], q2)


def _peer_scores(h2, wpq_b, sk1_b, sk2_b, tm=512):
    n, d = h2.shape
    dq = wpq_b.shape[1]
    nk, kd = sk1_b.shape
    return pl.pallas_call(
        _peer_scores_kernel,
        grid=(n // tm,),
        in_specs=[pl.BlockSpec((tm, d), lambda i: (i, 0)),
                  pl.BlockSpec((d, dq), lambda i: (0, 0)),
                  pl.BlockSpec((nk, kd), lambda i: (0, 0)),
                  pl.BlockSpec((nk, kd), lambda i: (0, 0))],
        out_specs=[pl.BlockSpec((PEER_HEADS, nk, tm), lambda i: (0, 0, i))] * 2,
        out_shape=[jax.ShapeDtypeStruct((PEER_HEADS, nk, n), F32)] * 2,
        compiler_params=_cparams(("arbitrary",)),
        name="peer_scores",
    )(h2, wpq_b, sk1_b, sk2_b)


def _staircase():
    return [(a, b) for a in range(PEER_TOPK) for b in range(PEER_TOPK) if (a + 1) * (b + 1) <= PEER_TOPK]


def _extract_top(s, rows_f, exact):
    vals = []
    rank = jnp.full(s.shape, float(PEER_TOPK), F32)
    for a in range(PEER_TOPK):
        m = jnp.max(s, axis=0, keepdims=True)
        hit = s == m
        if exact:
            hit = rows_f == jnp.min(jnp.where(hit, rows_f, 1e9), axis=0, keepdims=True)
        rank = jnp.where(hit, float(a), rank)
        s = jnp.where(hit, -jnp.inf, s)
        vals.append(m)
    return vals, rank


def _peer_topk_kernel(s1_ref, s2_ref, cnt_ref, e1_ref, rk_ref, e2_ref, *flag_ref, exact):
    nk, tn = s1_ref.shape[1], s1_ref.shape[2]
    rows_f = lax.broadcasted_iota(jnp.int32, (nk, tn), 0).astype(F32)
    pairs = _staircase()
    npad = -(-len(pairs) // SUBLANE) * SUBLANE
    prow = lax.broadcasted_iota(jnp.int32, (npad, tn), 0)
    flat_f = jnp.full((npad, tn), 1e9, F32)
    arow_f = jnp.full((npad, tn), -1.0, F32)
    for i, (a, b) in enumerate(pairs):
        flat_f = jnp.where(prow == i, float(a * PEER_TOPK + b), flat_f)
        arow_f = jnp.where(prow == i, float(a), arow_f)
    k_f = float(PEER_TOPK)

    def body(hd, tie):
        s1 = s1_ref[hd]
        s2 = s2_ref[hd]
        v1, rank1 = _extract_top(s1, rows_f, exact)
        v2, rank2 = _extract_top(s2, rows_f, exact)
        cand = jnp.full((npad, tn), -jnp.inf, F32)
        for i, (a, b) in enumerate(pairs):
            cand = jnp.where(prow == i, v1[a] + v2[b], cand)
        m0 = v1[0] + v2[0]
        c = cand
        selected = jnp.zeros((npad, tn), F32)
        for _ in range(PEER_TOPK):
            m = jnp.max(c, axis=0, keepdims=True)
            hit = c == m
            if exact:
                hit = flat_f == jnp.min(jnp.where(hit, flat_f, 2e9), axis=0, keepdims=True)
            selected = jnp.where(hit, 1.0, selected)
            c = jnp.where(hit, -jnp.inf, c)
        if not exact:
            n1 = jnp.sum(jnp.where(rank1 < k_f, 1.0, 0.0), axis=0, keepdims=True)
            n2 = jnp.sum(jnp.where(rank2 < k_f, 1.0, 0.0), axis=0, keepdims=True)
            n3 = jnp.sum(selected, axis=0, keepdims=True)
            bad = (n1 != k_f) | (n2 != k_f) | (n3 != k_f)
            tie = jnp.maximum(tie, jnp.where(bad, 1.0, 0.0))
        z = jnp.sum(jnp.where(selected > 0.5, jnp.exp(cand - m0), 0.0), axis=0, keepdims=True)
        cnt1 = jnp.zeros((nk, tn), F32)
        for a in range(PEER_TOPK):
            cnt_a = jnp.sum(jnp.where(arow_f == float(a), selected, 0.0), axis=0, keepdims=True)
            cnt1 = jnp.where(rank1 == float(a), cnt_a, cnt1)
        cnt_ref[hd] = cnt1
        e1_ref[hd] = jnp.exp(s1 - v1[0]) * (1.0 / z)
        rk_ref[hd] = rank2
        e2_ref[hd] = jnp.exp(s2 - v2[0])
        return tie

    tie = lax.fori_loop(0, PEER_HEADS, body, jnp.zeros((1, tn), F32))
    if not exact:
        flag_ref[0][...] = jnp.broadcast_to(tie, flag_ref[0].shape)


def _peer_topk_call(s1t, s2t, exact, tn=256):
    nh, nk, n = s1t.shape
    spec = pl.BlockSpec((nh, nk, tn), lambda i: (0, 0, i))
    out_specs = [spec] * 4
    out_shape = [jax.ShapeDtypeStruct((nh, nk, n), F32)] * 4
    if not exact:
        out_specs = out_specs + [pl.BlockSpec((SUBLANE, tn), lambda i: (0, i))]
        out_shape = out_shape + [jax.ShapeDtypeStruct((SUBLANE, n), F32)]
    return pl.pallas_call(
        functools.partial(_peer_topk_kernel, exact=exact),
        grid=(n // tn,),
        in_specs=[spec, spec],
        out_specs=out_specs,
        out_shape=out_shape,
        compiler_params=_cparams(("arbitrary",)),
        name="peer_topk_exact" if exact else "peer_topk",
    )(s1t, s2t)


def _peer_topk(s1t, s2t):
    cnt1, e1, rk2, e2, tie = _peer_topk_call(s1t, s2t, False)
    return lax.cond(jnp.max(tie) > 0.0,
                    lambda: tuple(_peer_topk_call(s1t, s2t, True)),
                    lambda: (cnt1, e1, rk2, e2))


PEER_SUB = 256


def _peer_dense_kernel(h_ref, eu_ref, ev_ref, cnt_ref, e1_ref, rk_ref, e2_ref, o_ref, at_scr, wa_scr, *, te):
    e = pl.program_id(1)
    nk = rk_ref.shape[1]
    tm, d = h_ref.shape

    @pl.when(e == 0)
    def _():
        o_ref[...] = jnp.zeros(o_ref.shape, F32)

    n_i1 = PEER_SUB // nk
    nsub = te // PEER_SUB
    assert te // nk == SUBLANE
    i1_base = pl.multiple_of(e * SUBLANE, SUBLANE)
    tok_piece = 2 * LANE
    n_tok = tm // tok_piece
    col_piece = 2 * LANE
    n_col = d // col_piece
    tiles = [(il, tb) for il in range(n_i1) for tb in range(tm // LANE)]

    def pre_activation(sb, k):
        ts_ = slice(k * tok_piece, (k + 1) * tok_piece)
        at_scr[sb, :, ts_] = _dot_nt(eu_ref[sb * PEER_SUB:(sb + 1) * PEER_SUB, :], h_ref[ts_, :])

    def down_projection(sb, k):
        cs_ = slice(k * col_piece, (k + 1) * col_piece)
        o_ref[:, cs_] += _dot_tn(wa_scr[sb], ev_ref[sb * PEER_SUB:(sb + 1) * PEER_SUB, cs_])

    def gate_tile(sb, il, tb):
        j1 = sb * n_i1 + il
        ks = slice(il * nk, (il + 1) * nk)
        cs = slice(tb * LANE, (tb + 1) * LANE)
        w = jnp.zeros((nk, LANE), F32)
        for hd in range(PEER_HEADS):
            c = cnt_ref[hd, pl.ds(i1_base, SUBLANE), cs][j1:j1 + 1, :]
            g1 = e1_ref[hd, pl.ds(i1_base, SUBLANE), cs][j1:j1 + 1, :]
            w = w + jnp.where(rk_ref[hd, :, cs] < c, e2_ref[hd, :, cs] * g1, 0.0)
        wa_scr[sb, ks, cs] = (w * _gelu(at_scr[sb, ks, cs])).astype(BF16)

    for k in range(n_tok):
        pre_activation(0, k)
    nslot = max(len(tiles), n_col)
    for sb in range(nsub):
        for k in range(nslot):
            if sb >= 1 and k < n_col:
                down_projection(sb - 1, k)
            if sb + 1 < nsub and k % (nslot // n_tok) == 0:
                pre_activation(sb + 1, k // (nslot // n_tok))
            if k < len(tiles):
                gate_tile(sb, *tiles[k])
    for k in range(n_col):
        down_projection(nsub - 1, k)


def _peer_dense(h2, eu_b, ev_b, cnt1, e1, rk2, e2, tm=512, te=1024):
    n, d = h2.shape
    ne = eu_b.shape[0]
    nh, nk, _ = cnt1.shape
    res = pl.BlockSpec((nh, nk, tm), lambda i, e: (0, 0, i))
    return pl.pallas_call(
        functools.partial(_peer_dense_kernel, te=te),
        grid=(n // tm, ne // te),
        in_specs=[pl.BlockSpec((tm, d), lambda i, e: (i, 0)),
                  pl.BlockSpec((te, d), lambda i, e: (e, 0)),
                  pl.BlockSpec((te, d), lambda i, e: (e, 0)),
                  res, res, res, res],
        out_specs=pl.BlockSpec((tm, d), lambda i, e: (i, 0)),
        out_shape=jax.ShapeDtypeStruct((n, d), F32),
        scratch_shapes=[pltpu.VMEM((te // PEER_SUB, PEER_SUB, tm), F32), pltpu.VMEM((te // PEER_SUB, PEER_SUB, tm), BF16)],
        compiler_params=_cparams(("parallel", "arbitrary")),
        name="peer_dense",
    )(h2, eu_b, ev_b, cnt1, e1, rk2, e2)


def _final_kernel(x1_ref, p_ref, gt_ref, o_ref):
    o_ref[...] = x1_ref[...] + gt_ref[...] * p_ref[...].reshape(x1_ref.shape)


def _final_residual(x1, peer, row0, mod4, bt, tt):
    nb, tb, d = x1.shape
    tpb = tb // tt
    tm = bt * tt
    n = nb * tb
    off = row0 // tm
    return pl.pallas_call(
        _final_kernel,
        grid=(n // tm,),
        in_specs=[pl.BlockSpec((bt, tt, d), lambda i: (i // tpb, i % tpb, 0)),
                  pl.BlockSpec((tm, d), lambda i: (off + i, 0)),
                  pl.BlockSpec((bt, None, 1, d), lambda i: (i // tpb, 5, 0, 0))],
        out_specs=pl.BlockSpec((bt, tt, d), lambda i: (i // tpb, i % tpb, 0)),
        out_shape=jax.ShapeDtypeStruct((nb, tb, d), F32),
        compiler_params=_cparams(("arbitrary",)),
        name="final_residual",
    )(x1, peer, mod4)


def _block_expand(n_cols, width=LANE):
    j = np.arange(width)[:, None]
    s = np.arange(n_cols)[None, :]
    return (s // SEL_BLOCK == j).astype(np.float32)


def _forward(x_prompt, x_sample, cache_k_cmp, cache_v_cmp, cache_k_sel, cache_v_sel, state_k_win, state_v_win,
             page_table, c_prompt, c_sample, rel_bias, w_ada, b_ada, g_n1, g_n2, w_in, ln_v_g, ln_v_b, w_s, b_s,
             g_q, g_k, pe_k, w_c1k, w_c2k, pe_v, w_c1v, w_c2v, w_a, w_b, w_o, w_pq, sk1, sk2, expert_u, expert_v):
    assert w_ada.shape[0] == 1, "single layer"
    bp, tp, d = x_prompt.shape
    bs, ts, _ = x_sample.shape
    np_, ns_ = bp * tp, bs * ts
    (w_ada, b_ada, g_n1, g_n2, w_in, ln_v_g, ln_v_b, w_s, b_s, g_q, g_k, pe_k, w_c1k, w_c2k, pe_v, w_c1v, w_c2v,
     w_a, w_b, w_o, w_pq, sk1, sk2, expert_u, expert_v) = [a[0] for a in (
         w_ada, b_ada, g_n1, g_n2, w_in, ln_v_g, ln_v_b, w_s, b_s, g_q, g_k, pe_k, w_c1k, w_c2k, pe_v, w_c1v, w_c2v,
         w_a, w_b, w_o, w_pq, sk1, sk2, expert_u, expert_v)]

    n_gate = 3 * N_HEADS
    c0 = T_NSA * PROJ_TILE
    w_in_p = jnp.concatenate([w_in[:, :c0],
                              jnp.pad(w_in[:, c0:c0 + n_gate], ((0, 0), (0, PROJ_TILE - n_gate))),
                              w_in[:, c0 + n_gate:]], axis=1).astype(BF16)
    ones = jnp.ones((PROJ_TILE,), F32)
    zeros = jnp.zeros((PROJ_TILE,), F32)
    rep = PROJ_TILE // HEAD_DIM
    gains = [ones] * N_TILES
    flags = [zeros] * N_TILES
    for k in range(T_Q, T_KC):
        gains[k], flags[k] = jnp.tile(g_q, rep), ones
    gains[T_KS], flags[T_KS] = jnp.tile(g_k[1], rep), ones
    gains[T_KW], flags[T_KW] = jnp.tile(g_k[2], rep), ones
    gain = jnp.stack(gains)[:, None, :]
    flag = jnp.stack(flags)[:, None, :]
    tril = jnp.tril(w_s)
    wm_p = tril.astype(BF16)
    bsb_p = jnp.broadcast_to(b_s[:, :, None], (A_GROUPS, CHUNK, CHUNK))
    nrep = CHUNK // ts
    wm_s = jnp.einsum("ab,gij->gaibj", jnp.eye(nrep, dtype=F32), tril[:, :ts, :ts]).reshape(A_GROUPS, CHUNK, CHUNK).astype(BF16)
    bsb_s = jnp.broadcast_to(jnp.tile(b_s[:, :ts], (1, nrep))[:, :, None], (A_GROUPS, CHUNK, CHUNK))
    w1k_b, w2k_b, w1v_b, w2v_b = [a.astype(BF16) for a in (w_c1k, w_c2k, w_c1v, w_c2v)]
    wa_b, wb_b, wo_b, wpq_b = [a.astype(BF16) for a in (w_a, w_b, w_o, w_pq)]
    sk1_b, sk2_b = sk1.astype(BF16), sk2.astype(BF16)
    eu_b, ev_b = expert_u.astype(BF16), expert_v.astype(BF16)
    one_gain = jnp.ones((HEAD_DIM,), F32)

    def pool_weights(w1, pe):
        hid = w1.shape[1]
        w1x = w1.reshape(CMP_BLOCK // 2, 2, HEAD_DIM, hid).transpose(0, 2, 1, 3).reshape(CMP_BLOCK // 2 * HEAD_DIM, 2 * hid)
        pe8 = jnp.repeat(pe.reshape(CMP_BLOCK // 2, 2, 1, HEAD_DIM), N_KV, axis=2).reshape(CMP_BLOCK // 2, SUBLANE, HEAD_DIM)
        return w1x.astype(BF16), pe8

    w1k_x, pe8_k = pool_weights(w_c1k, pe_k)
    w1v_x, pe8_v = pool_weights(w_c1v, pe_v)

    nc = bp + bs
    ncp = -(-nc // SUBLANE) * SUBLANE
    c_all = jnp.pad(jnp.concatenate([c_prompt, c_sample], axis=0), ((0, ncp - nc), (0, 0)))
    mod = _modulation(c_all, w_ada, b_ada)
    mod_p = mod[:bp].reshape(bp, 6, 1, d)
    mod_s = mod[bp:nc].reshape(bs, 6, 1, d)

    tm_p = min(1024, tp)
    bt_s = min(1024 // ts, bs)

    proj_p = _in_projection(x_prompt, mod_p, g_n1, w_in_p, gain, flag, 1, tm_p)
    proj_s = _in_projection(x_sample, mod_s, g_n1, w_in_p, gain, flag, bt_s, ts)

    cpb = tp // CHUNK
    ya_p, vch_p = _mixer_a(proj_p, ln_v_g, ln_v_b, wm_p, bsb_p, bp, lambda i: i // cpb)
    ya_s, vch_s = _mixer_a(proj_s, ln_v_g, ln_v_b, wm_s, bsb_s, ns_ // CHUNK, lambda i: i)

    kcmp_p = _compress_prompt(proj_p, T_KC, bp, tp, pe_k, w1k_b, w2k_b, g_k[0], True)
    vcmp_p = _compress_prompt(proj_p, T_VC, bp, tp, pe_v, w1v_b, w2v_b, one_gain, False)
    n_phys, page = cache_k_cmp.shape[1], cache_k_cmp.shape[2]
    bpp = page // CMP_BLOCK
    blk_rows = CMP_BLOCK * N_KV
    kcmp_pool = _compress_pool(cache_k_cmp.reshape(n_phys * bpp, blk_rows, HEAD_DIM), pe8_k, w1k_x, w2k_b, g_k[0], True)
    vcmp_pool = _compress_pool(cache_v_cmp.reshape(n_phys * bpp, blk_rows, HEAD_DIM), pe8_v, w1v_x, w2v_b, one_gain, False)
    npg = page_table.shape[1]
    kcmp_s = kcmp_pool.reshape(n_phys, bpp * SUBLANE * HEAD_DIM)[page_table].reshape(bs, npg * bpp, SUBLANE, HEAD_DIM)
    vcmp_s = vcmp_pool.reshape(n_phys, bpp * SUBLANE * HEAD_DIM)[page_table].reshape(bs, npg * bpp, SUBLANE, HEAD_DIM)

    tables, ctab = _bias_tables(rel_bias, ATT_TILE)
    oc_p, sel_p = _cmp_select(proj_p, ctab, kcmp_p, vcmp_p, bp, tp)
    yb_p = _prompt_attention(proj_p, rel_bias, tables, oc_p, bp, tp, "sel", sel_p)
    yb_p = _prompt_attention(proj_p, rel_bias, tables, yb_p, bp, tp, "win")

    past = npg * page
    emat_s = jnp.asarray(_block_expand(past + LANE), BF16)
    yb_s, kwin_s, vwin_s = _sample_attention(
        proj_s, rel_bias, page_table, kcmp_s, vcmp_s,
        cache_k_sel.reshape(n_phys, page * N_KV, HEAD_DIM), cache_v_sel.reshape(n_phys, page * N_KV, HEAD_DIM),
        state_k_win.reshape(bs, -1, HEAD_DIM), state_v_win.reshape(bs, -1, HEAD_DIM), emat_s, bs, ts)

    t_p = _merge(proj_p, ya_p, yb_p, wa_b, wb_b)
    t_s = _merge(proj_s, ya_s, yb_s, wa_b, wb_b)
    x1_p, h2_p = _out_projection(t_p, x_prompt, mod_p, g_n2, wo_b, 1, min(256, tp))
    x1_s, h2_s = _out_projection(t_s, x_sample, mod_s, g_n2, wo_b, min(256 // ts, bs), ts)

    h2 = jnp.concatenate([h2_p, h2_s], axis=0)
    s1t, s2t = _peer_scores(h2, wpq_b, sk1_b, sk2_b)
    cnt1, e1, rk2, e2 = _peer_topk(s1t, s2t)
    peer = _peer_dense(h2, eu_b, ev_b, cnt1, e1, rk2, e2)
    y_p = _final_residual(x1_p, peer, 0, mod_p, 1, min(512, tp))
    y_s = _final_residual(x1_s, peer, np_, mod_s, min(512 // ts, bs), ts)

    def kv_p(k):
        return proj_p[k].reshape(1, bp, tp, N_KV, HEAD_DIM)

    def kv_s(k):
        return proj_s[k].reshape(1, bs, ts, N_KV, HEAD_DIM)

    wb_p = min(WINDOW, tp)
    wlen = state_k_win.shape[2]
    return (y_p, y_s,
            kv_p(T_KC), kv_p(T_VC), kv_p(T_KS), kv_p(T_VS),
            kv_p(T_KW)[:, :, tp - wb_p:], kv_p(T_VW)[:, :, tp - wb_p:],
            vch_p.reshape(1, bp, CHUNK, -1),
            kv_s(T_KC), kv_s(T_VC), kv_s(T_KS), kv_s(T_VS),
            kwin_s.reshape(1, bs, wlen, N_KV, HEAD_DIM), vwin_s.reshape(1, bs, wlen, N_KV, HEAD_DIM),
            vch_s.reshape(1, bs, ts, -1))


def kernel(x_prompt, x_sample, cache_k_cmp, cache_v_cmp, cache_k_sel, cache_v_sel, state_k_win, state_v_win, page_table, c_prompt, c_sample, rel_bias, w_ada, b_ada, g_n1, g_n2, w_in, ln_v_g, ln_v_b, w_s, b_s, g_q, g_k, pe_k, w_c1k, w_c2k, pe_v, w_c1v, w_c2v, w_a, w_b, w_o, w_pq, sk1, sk2, expert_u, expert_v):
    return _forward(x_prompt, x_sample, cache_k_cmp, cache_v_cmp, cache_k_sel, cache_v_sel, state_k_win, state_v_win,
                    page_table, c_prompt, c_sample, rel_bias, w_ada, b_ada, g_n1, g_n2, w_in, ln_v_g, ln_v_b, w_s, b_s,
                    g_q, g_k, pe_k, w_c1k, w_c2k, pe_v, w_c1v, w_c2v, w_a, w_b, w_o, w_pq, sk1, sk2, expert_u, expert_v)
```

```python
import functools
import math

import numpy as np
import jax
import jax.numpy as jnp
from jax import lax
from jax.experimental import pallas as pl
from jax.experimental.pallas import tpu as pltpu

F32 = jnp.float32
BF16 = jnp.bfloat16

N_HEADS = 16
HEAD_DIM = 128
N_KV = 4
GQA = N_HEADS // N_KV
KV_WIDTH = N_KV * HEAD_DIM
CHUNK = 128
A_GROUPS = 8
CMP_BLOCK = 64
SEL_BLOCK = 64
N_SEL = 16
WINDOW = 512
N_BUCKETS = 32
MAX_DISTANCE = 128
N_KEYS = 128
PEER_HEADS = 8
PEER_TOPK = 16
ATTN_SCALE = HEAD_DIM ** -0.5
NEG = -1e30
FORCE_BONUS = 1e4
EPS = 1e-6
LANE = 128
SUBLANE = 8
PROJ_TILE = 512
ATT_TILE = 256
VMEM_LIMIT = 56 * 1024 * 1024

T_U, T_V, T_Q, T_KC, T_VC, T_KS, T_VS, T_KW, T_VW, T_NSA, T_GA, T_GB, N_TILES = 0, 2, 4, 8, 9, 10, 11, 12, 13, 14, 15, 19, 23


def _bucket_thresholds():
    n = np.arange(0, 2 * MAX_DISTANCE)
    nf = np.maximum(n, 1).astype(np.float32)
    half = N_BUCKETS // 2
    large = half + (np.log(nf / half) / math.log(MAX_DISTANCE / half) * (N_BUCKETS - half)).astype(np.int32)
    b = np.where(n < half, n, np.minimum(large, N_BUCKETS - 1))
    assert np.all(np.diff(b) >= 0) and b[-1] == N_BUCKETS - 1
    return [int(np.argmax(b >= k)) for k in range(N_BUCKETS)]


BUCKET_THR = _bucket_thresholds()
FAR_DIST = BUCKET_THR[-1]
assert FAR_DIST <= MAX_DISTANCE


def _cparams(sem, vmem=VMEM_LIMIT):
    return pltpu.CompilerParams(dimension_semantics=sem, vmem_limit_bytes=vmem)


def _gelu(x):
    c = 2.0 * math.sqrt(2.0 / math.pi)
    u = (x * x) * (-0.044715 * c) - c
    return x * (1.0 / (1.0 + jnp.exp(x * u)))


def _sigmoid(x):
    return 1.0 / (1.0 + jnp.exp(-x))


def _dot_nt(a, b):
    return lax.dot_general(a, b, (((1,), (1,)), ((), ())), preferred_element_type=F32)


def _dot_tn(a, b):
    return lax.dot_general(a, b, (((0,), (0,)), ((), ())), preferred_element_type=F32)


def _bias_chain(dist, rbs):
    b = jnp.full(dist.shape, rbs[0], F32)
    for k in range(1, N_BUCKETS):
        b = jnp.where(dist >= BUCKET_THR[k], rbs[k], b)
    return b


def _mod_kernel(c_ref, w_ref, b_ref, o_ref):
    c = c_ref[...]
    a = (c * _sigmoid(c)).astype(BF16)
    o_ref[...] = jnp.dot(a, w_ref[...].astype(BF16), preferred_element_type=F32) + b_ref[...]


def _modulation(c_all, w_ada, b_ada):
    m, d = c_all.shape
    n = w_ada.shape[1]
    tn = 1024
    return pl.pallas_call(
        _mod_kernel,
        grid=(n // tn,),
        in_specs=[pl.BlockSpec((m, d), lambda j: (0, 0)),
                  pl.BlockSpec((d, tn), lambda j: (0, j)),
                  pl.BlockSpec((1, tn), lambda j: (0, j))],
        out_specs=pl.BlockSpec((m, tn), lambda j: (0, j)),
        out_shape=jax.ShapeDtypeStruct((m, n), F32),
        compiler_params=_cparams(("arbitrary",)),
        name="adaln_mod",
    )(c_all, w_ada, b_ada.reshape(1, n))


def _inproj_kernel(x_ref, sc_ref, sh_ref, gn_ref, w_ref, gain_ref, flag_ref, o_ref, h_scr):
    j = pl.program_id(1)

    @pl.when(j == 0)
    def _():
        x = x_ref[...]
        r = lax.rsqrt(jnp.mean(x * x, axis=-1, keepdims=True) + EPS)
        h = (x * r) * gn_ref[...] * (1.0 + sc_ref[...]) + sh_ref[...]
        h_scr[...] = h.reshape(h_scr.shape).astype(BF16)

    y = jnp.dot(h_scr[...], w_ref[...], preferred_element_type=F32)

    @pl.when(j < T_Q)
    def _():
        o_ref[...] = _gelu(y)

    @pl.when((j >= T_Q) & (j < T_NSA))
    def _():
        parts = []
        for hh in range(PROJ_TILE // HEAD_DIM):
            yh = y[:, hh * HEAD_DIM:(hh + 1) * HEAD_DIM]
            parts.append(yh * lax.rsqrt(jnp.mean(yh * yh, axis=-1, keepdims=True) + EPS))
        yn = jnp.concatenate(parts, axis=1) * gain_ref[...]
        o_ref[...] = jnp.where(flag_ref[...] > 0.5, yn, y)

    @pl.when(j >= T_NSA)
    def _():
        o_ref[...] = _sigmoid(y)


def _in_projection(x3, mod4, g_n1, w_in_p, gain, flag, bt, tt):
    nb, tb, d = x3.shape
    tpb = tb // tt
    tm = bt * tt
    n = nb * tb
    grid = (n // tm, N_TILES)
    return pl.pallas_call(
        _inproj_kernel,
        grid=grid,
        in_specs=[pl.BlockSpec((bt, tt, d), lambda i, j: (i // tpb, i % tpb, 0)),
                  pl.BlockSpec((bt, None, 1, d), lambda i, j: (i // tpb, 1, 0, 0)),
                  pl.BlockSpec((bt, None, 1, d), lambda i, j: (i // tpb, 0, 0, 0)),
                  pl.BlockSpec((1, 1, d), lambda i, j: (0, 0, 0)),
                  pl.BlockSpec((d, PROJ_TILE), lambda i, j: (0, j)),
                  pl.BlockSpec((None, 1, PROJ_TILE), lambda i, j: (j, 0, 0)),
                  pl.BlockSpec((None, 1, PROJ_TILE), lambda i, j: (j, 0, 0))],
        out_specs=pl.BlockSpec((None, tm, PROJ_TILE), lambda i, j: (j, i, 0)),
        out_shape=jax.ShapeDtypeStruct((N_TILES, n, PROJ_TILE), F32),
        scratch_shapes=[pltpu.VMEM((tm, d), BF16)],
        compiler_params=_cparams(("parallel", "arbitrary")),
        name="in_projection",
    )(x3, mod4, mod4, g_n1.reshape(1, 1, d), w_in_p, gain, flag)


def _mixa_kernel(u0_ref, u1_ref, v0_ref, v1_ref, lg_ref, lb_ref, wm_ref, bs_ref, ya_ref, vch_ref):
    v = jnp.concatenate([v0_ref[...], v1_ref[...]], axis=1)
    mu = jnp.mean(v, axis=-1, keepdims=True)
    var = jnp.mean(jnp.square(v - mu), axis=-1, keepdims=True)
    vln = ((v - mu) * lax.rsqrt(var + EPS)) * lg_ref[...] + lb_ref[...]
    vch_ref[...] = vln
    u = jnp.concatenate([u0_ref[...], u1_ref[...]], axis=1)
    vb = vln.astype(BF16)
    gd = vln.shape[1] // A_GROUPS
    for g in range(A_GROUPS):
        sl = slice(g * gd, (g + 1) * gd)
        s = jnp.dot(wm_ref[g], vb[:, sl], preferred_element_type=F32) + bs_ref[g]
        ya_ref[:, sl] = (u[:, sl] * s).astype(BF16)


def _mixer_a(proj, ln_g, ln_b, wm, bsb, vch_blocks, vch_map):
    n = proj.shape[1]
    aw = 2 * PROJ_TILE

    def tile(k):
        return pl.BlockSpec((None, CHUNK, PROJ_TILE), lambda i, k=k: (k, i, 0))

    return pl.pallas_call(
        _mixa_kernel,
        grid=(n // CHUNK,),
        in_specs=[tile(T_U), tile(T_U + 1), tile(T_V), tile(T_V + 1),
                  pl.BlockSpec((1, aw), lambda i: (0, 0)),
                  pl.BlockSpec((1, aw), lambda i: (0, 0)),
                  pl.BlockSpec((A_GROUPS, CHUNK, CHUNK), lambda i: (0, 0, 0)),
                  pl.BlockSpec((A_GROUPS, CHUNK, CHUNK), lambda i: (0, 0, 0))],
        out_specs=[pl.BlockSpec((CHUNK, aw), lambda i: (i, 0)),
                   pl.BlockSpec((CHUNK, aw), lambda i: (vch_map(i), 0))],
        out_shape=[jax.ShapeDtypeStruct((n, aw), BF16),
                   jax.ShapeDtypeStruct((vch_blocks * CHUNK, aw), F32)],
        compiler_params=_cparams(("arbitrary",)),
        name="mixer_a",
    )(proj, proj, proj, proj, ln_g.reshape(1, aw), ln_b.reshape(1, aw), wm, bsb)


def _compress_tail(hid, w2_ref, gain_ref, do_rms):
    out = jnp.dot(_gelu(hid).astype(BF16), w2_ref[...], preferred_element_type=F32)
    if do_rms:
        out = out * lax.rsqrt(jnp.mean(out * out, axis=-1, keepdims=True) + EPS) * gain_ref[...]
    return out


def _compress_prompt_kernel(x0_ref, x1_ref, x2_ref, x3_ref, pe_ref, w1_ref, w2_ref, gain_ref, o_ref, lhs_scr, *, nb, do_rms):
    for s_ in range(CMP_BLOCK):
        for g, x_ref in enumerate((x0_ref, x1_ref, x2_ref, x3_ref)):
            rows = x_ref[pl.ds(s_, nb, stride=CMP_BLOCK), :]
            lhs_scr[g * nb:(g + 1) * nb, s_ * HEAD_DIM:(s_ + 1) * HEAD_DIM] = (rows + pe_ref[s_:s_ + 1, :]).astype(BF16)
    hid = jnp.dot(lhs_scr[...], w1_ref[...], preferred_element_type=F32)
    out = _compress_tail(hid, w2_ref, gain_ref, do_rms)
    for g in range(N_KV):
        o_ref[:, g * HEAD_DIM:(g + 1) * HEAD_DIM] = out[g * nb:(g + 1) * nb]


def _compress_prompt(proj, tile, bsz, t, pe, w1b, w2b, gain, do_rms):
    nb = t // CMP_BLOCK
    hid = w1b.shape[1]
    return pl.pallas_call(
        functools.partial(_compress_prompt_kernel, nb=nb, do_rms=do_rms),
        grid=(bsz,),
        in_specs=[pl.BlockSpec((None, t, HEAD_DIM), lambda b, g=g: (tile, b, g)) for g in range(N_KV)] + [
                  pl.BlockSpec((CMP_BLOCK, HEAD_DIM), lambda b: (0, 0)),
                  pl.BlockSpec((CMP_BLOCK * HEAD_DIM, hid), lambda b: (0, 0)),
                  pl.BlockSpec((hid, HEAD_DIM), lambda b: (0, 0)),
                  pl.BlockSpec((1, HEAD_DIM), lambda b: (0, 0))],
        out_specs=pl.BlockSpec((nb, KV_WIDTH), lambda b: (b, 0)),
        out_shape=jax.ShapeDtypeStruct((bsz * nb, KV_WIDTH), F32),
        scratch_shapes=[pltpu.VMEM((N_KV * nb, CMP_BLOCK * HEAD_DIM), BF16)],
        compiler_params=_cparams(("arbitrary",)),
        name="compress_prompt",
    )(proj, proj, proj, proj, pe, w1b, w2b, gain.reshape(1, HEAD_DIM))


def _compress_pool_kernel(x_ref, pe8_ref, w1_ref, w2_ref, gain_ref, o_ref, lhs_scr, *, do_rms):
    tb = x_ref.shape[0]
    m = tb * SUBLANE
    hid = w2_ref.shape[0]
    for j in range(CMP_BLOCK // 2):
        xj = x_ref[:, SUBLANE * j:SUBLANE * (j + 1), :] + pe8_ref[j]
        lhs_scr[:, j * HEAD_DIM:(j + 1) * HEAD_DIM] = xj.reshape(m, HEAD_DIM).astype(BF16)
    acc = jnp.dot(lhs_scr[...], w1_ref[...], preferred_element_type=F32)
    hidv = acc[:, :hid] + pltpu.roll(acc[:, hid:], m - N_KV, 0)
    out = _compress_tail(hidv, w2_ref, gain_ref, do_rms)
    o_ref[...] = out.reshape(tb, SUBLANE, HEAD_DIM)


def _compress_pool(x3, pe8, w1x, w2b, gain, do_rms, tb=64):
    nblk = x3.shape[0]
    tb = min(tb, nblk)
    hid = w2b.shape[0]
    kdim = CMP_BLOCK // 2 * HEAD_DIM
    return pl.pallas_call(
        functools.partial(_compress_pool_kernel, do_rms=do_rms),
        grid=(nblk // tb,),
        in_specs=[pl.BlockSpec((tb, CMP_BLOCK * N_KV, HEAD_DIM), lambda i: (i, 0, 0)),
                  pl.BlockSpec((CMP_BLOCK // 2, SUBLANE, HEAD_DIM), lambda i: (0, 0, 0)),
                  pl.BlockSpec((kdim, 2 * hid), lambda i: (0, 0)),
                  pl.BlockSpec((hid, HEAD_DIM), lambda i: (0, 0)),
                  pl.BlockSpec((1, HEAD_DIM), lambda i: (0, 0))],
        out_specs=pl.BlockSpec((tb, SUBLANE, HEAD_DIM), lambda i: (i, 0, 0)),
        out_shape=jax.ShapeDtypeStruct((nblk, SUBLANE, HEAD_DIM), F32),
        scratch_shapes=[pltpu.VMEM((tb * SUBLANE, kdim), BF16)],
        compiler_params=_cparams(("arbitrary",)),
        name="compress_pool",
    )(x3, pe8, w1x, w2b, gain.reshape(1, HEAD_DIM))


def _bias_table_kernel(rb_ref, o_ref, c_ref, *, ts):
    g = pl.program_id(0)
    i = lax.broadcasted_iota(jnp.int32, (ts, ts), 0)
    j = lax.broadcasted_iota(jnp.int32, (ts, ts), 1)
    for d in range(2):
        dist = d * ts + i - j
        for r in range(GQA):
            rbs = [rb_ref[k, g * GQA + r] for k in range(N_BUCKETS)]
            o_ref[d, r * ts:(r + 1) * ts, :] = _bias_chain(dist, rbs) - rbs[-1]
    ic = lax.broadcasted_iota(jnp.int32, (ts, LANE), 0)
    nc = lax.broadcasted_iota(jnp.int32, (ts, LANE), 1) - LANE // 2
    dist_c = ic - (nc * CMP_BLOCK + CMP_BLOCK - 1)
    for r in range(GQA):
        rbs = [rb_ref[k, g * GQA + r] for k in range(N_BUCKETS)]
        c_ref[r * ts:(r + 1) * ts, :] = _bias_chain(dist_c, rbs)


def _bias_tables(rel_bias, ts):
    return pl.pallas_call(
        functools.partial(_bias_table_kernel, ts=ts),
        grid=(N_KV,),
        in_specs=[pl.BlockSpec(memory_space=pltpu.SMEM)],
        out_specs=[pl.BlockSpec((None, 2, GQA * ts, ts), lambda g: (g, 0, 0, 0)),
                   pl.BlockSpec((None, GQA * ts, LANE), lambda g: (g, 0, 0))],
        out_shape=[jax.ShapeDtypeStruct((N_KV, 2, GQA * ts, ts), F32),
                   jax.ShapeDtypeStruct((N_KV, GQA * ts, LANE), F32)],
        compiler_params=_cparams(("arbitrary",)),
        name="bias_tables",
    )(rel_bias)


def _rank_select(score, n_sel):
    t = score.shape[1]
    ngrp = -(-n_sel // SUBLANE)
    jrow = lax.broadcasted_iota(jnp.int32, (SUBLANE, t), 0)
    sel = []
    for gb in range(ngrp):
        blk = score[gb * SUBLANE:(gb + 1) * SUBLANE, :]
        rank = jnp.zeros((SUBLANE, t), F32)
        for i in range(n_sel):
            row = score[i:i + 1, :]
            if i < gb * SUBLANE:
                beats = row >= blk
            elif i >= (gb + 1) * SUBLANE:
                beats = row > blk
            else:
                beats = (row > blk) | ((jrow > i - gb * SUBLANE) & (row == blk))
            rank = rank + jnp.where(beats, 1.0, 0.0)
        keep = (rank < float(min(N_SEL, n_sel))) & (jrow + gb * SUBLANE < n_sel)
        sel.append(jnp.where(keep, 1.0, 0.0))
    sel.append(jnp.zeros((score.shape[0] - ngrp * SUBLANE, t), F32))
    return jnp.concatenate(sel, axis=0)


def _cmp_kernel(ct_ref, q0_ref, q1_ref, q2_ref, q3_ref, nsa_ref, kc_ref, vc_ref, oc_ref, sel_ref, *, tq, nb, n_sel):
    qt = pl.program_id(1)
    nsa = nsa_ref[...]
    row = lax.broadcasted_iota(jnp.int32, (tq, LANE), 0) + qt * tq
    col = lax.broadcasted_iota(jnp.int32, (tq, LANE), 1)
    dist = row - (col * CMP_BLOCK + CMP_BLOCK - 1)
    valid = (dist >= 0) & (col < nb)
    cur = row // SEL_BLOCK
    forced = (col == 0) | (col == cur) | (col == cur - 1)
    shift = (qt * (tq // CMP_BLOCK) + LANE // 2) % LANE
    pad = jnp.zeros((LANE - nb, HEAD_DIM), F32)
    for g in range(N_KV):
        q = (q0_ref, q1_ref, q2_ref, q3_ref)[g][...]
        qst = jnp.concatenate([q[:, r * HEAD_DIM:(r + 1) * HEAD_DIM] for r in range(GQA)], axis=0).astype(BF16)
        kg = jnp.concatenate([kc_ref[:, g * HEAD_DIM:(g + 1) * HEAD_DIM], pad], axis=0).astype(BF16)
        vg = jnp.concatenate([vc_ref[:, g * HEAD_DIM:(g + 1) * HEAD_DIM], pad], axis=0).astype(BF16)
        s = _dot_nt(qst, kg) * ATTN_SCALE
        ps = []
        imp = jnp.zeros((tq, LANE), F32)
        for r in range(GQA):
            h = g * GQA + r
            b = pltpu.roll(ct_ref[g, r * tq:(r + 1) * tq, :], shift, 1)
            sr = jnp.where(valid, s[r * tq:(r + 1) * tq] + b, NEG)
            m = jnp.max(sr, axis=-1, keepdims=True)
            p = jnp.where(valid, jnp.exp(sr - m), 0.0)
            den = jnp.sum(p, axis=-1, keepdims=True)
            p = p * (1.0 / jnp.maximum(den, 1e-30))
            imp = imp + p
            ps.append(p)
        o = jnp.dot(jnp.concatenate(ps, axis=0).astype(BF16), vg, preferred_element_type=F32)
        for r in range(GQA):
            h = g * GQA + r
            oc_ref[:, h * HEAD_DIM:(h + 1) * HEAD_DIM] = o[r * tq:(r + 1) * tq] * nsa[:, 3 * h:3 * h + 1]
        score = jnp.where(col <= cur, imp + jnp.where(forced, FORCE_BONUS, 0.0), NEG)
        score = jnp.where(col < n_sel, score, -3e38)
        sel_ref[g] = _rank_select(score.T, n_sel).T


def _cmp_select(proj, ctab, kcmp, vcmp, bsz, t, tq=ATT_TILE):
    nb = kcmp.shape[0] // bsz
    n_sel = -(-t // SEL_BLOCK)
    nq = t // tq
    n = proj.shape[1]
    assert nb <= LANE // 2 and SEL_BLOCK == CMP_BLOCK

    def tile(k):
        return pl.BlockSpec((None, tq, PROJ_TILE), lambda b, i, k=k: (k, b * nq + i, 0))

    return pl.pallas_call(
        functools.partial(_cmp_kernel, tq=tq, nb=nb, n_sel=n_sel),
        grid=(bsz, nq),
        in_specs=[pl.BlockSpec((N_KV, GQA * tq, LANE), lambda b, i: (0, 0, 0)),
                  tile(T_Q), tile(T_Q + 1), tile(T_Q + 2), tile(T_Q + 3), tile(T_NSA),
                  pl.BlockSpec((nb, KV_WIDTH), lambda b, i: (b, 0)),
                  pl.BlockSpec((nb, KV_WIDTH), lambda b, i: (b, 0))],
        out_specs=[pl.BlockSpec((tq, N_HEADS * HEAD_DIM), lambda b, i: (b * nq + i, 0)),
                   pl.BlockSpec((None, N_KV, tq, LANE), lambda b, i: (b, 0, i, 0))],
        out_shape=[jax.ShapeDtypeStruct((n, N_HEADS * HEAD_DIM), F32),
                   jax.ShapeDtypeStruct((bsz, N_KV, t, LANE), F32)],
        compiler_params=_cparams(("parallel", "arbitrary")),
        name="cmp_select",
    )(ctab, proj, proj, proj, proj, proj, kcmp, vcmp)


MASK_BIG = 2.0 ** 100
AUG = 2 * HEAD_DIM
ROW_BLOCK = 128


def _attn_kernel(rb_ref, q_ref, nsa_ref, k_ref, v_ref, tb_ref, prev_ref, *rest, mode, tq, branch):
    if mode == "sel":
        sel_ref, o_ref, kb, vb, qa, s_scr, p_scr, m_s, a_s, acc_s = rest
    else:
        o_ref, kb, vb, qa, s_scr, p_scr, m_s, a_s, acc_s = rest
    g = pl.program_id(1)
    qt = pl.program_id(2)
    tk = tq
    rows4 = GQA * tq
    t_all = kb.shape[0]

    @pl.when(qt == 0)
    def _():
        krow = lax.broadcasted_iota(jnp.int32, (t_all, LANE), 0)
        lane = lax.broadcasted_iota(jnp.int32, (t_all, LANE), 1)
        onehot = ((lane < SEL_BLOCK) & (krow // SEL_BLOCK == lane)) | (lane == SEL_BLOCK) | (lane == SEL_BLOCK + 1)
        kb[:, 0:HEAD_DIM] = k_ref[...].astype(BF16)
        kb[:, HEAD_DIM:AUG] = onehot.astype(BF16)
        vb[:, 0:HEAD_DIM] = v_ref[...].astype(BF16)
        vb[:, HEAD_DIM:AUG] = (lane == 0).astype(BF16)

    q = q_ref[...]
    lane_q = lax.broadcasted_iota(jnp.int32, (tq, LANE), 1)
    if mode == "sel":
        selm = jnp.where(lane_q < SEL_BLOCK, (sel_ref[...] - 1.0) * MASK_BIG, 0.0)
    else:
        selm = jnp.zeros((tq, LANE), F32)
    for r in range(GQA):
        b_far = jnp.full((tq, LANE), rb_ref[N_BUCKETS - 1, g * GQA + r], F32)
        b_hi = b_far.astype(BF16).astype(F32)
        ext = jnp.where(lane_q == SEL_BLOCK, b_hi, jnp.where(lane_q == SEL_BLOCK + 1, b_far - b_hi, selm))
        qa[r * tq:(r + 1) * tq, 0:HEAD_DIM] = (q[:, r * HEAD_DIM:(r + 1) * HEAD_DIM] * ATTN_SCALE).astype(BF16)
        qa[r * tq:(r + 1) * tq, HEAD_DIM:AUG] = ext.astype(BF16)
    m_s[...] = jnp.full(m_s.shape, NEG, F32)
    acc_s[...] = jnp.zeros(acc_s.shape, F32)
    nrb = rows4 // ROW_BLOCK
    rowpos = lax.broadcasted_iota(jnp.int32, (ROW_BLOCK, tk), 0)
    colpos = lax.broadcasted_iota(jnp.int32, (ROW_BLOCK, tk), 1)

    def chunk(kt, table, mask_kind):
        k0 = pl.multiple_of(kt * tk, tk)
        s_scr[...] = _dot_nt(qa[...], kb[pl.ds(k0, tk), :])
        for rb in range(nrb):
            rs = slice(rb * ROW_BLOCK, (rb + 1) * ROW_BLOCK)
            s = s_scr[rs, :]
            if table is not None:
                s = s + tb_ref[table, rs, :]
            if mask_kind is not None:
                rp = rowpos + (rb * ROW_BLOCK) % tq
                keep = (rp >= colpos) if mask_kind == "causal" else (colpos > rp)
                s = jnp.where(keep, s, NEG)
            m_old = m_s[rs, :]
            m_new = jnp.maximum(m_old, jnp.max(s, axis=-1, keepdims=True))
            p_scr[rs, :] = jnp.exp(s - jnp.concatenate([m_new] * (tk // LANE), axis=1)).astype(BF16)
            a_s[rs, :] = jnp.exp(m_old - m_new)
            m_s[rs, :] = m_new
        pv = jnp.dot(p_scr[...], vb[pl.ds(k0, tk), :], preferred_element_type=F32)
        a = a_s[...]
        acc_s[...] = jnp.concatenate([a] * (AUG // LANE), axis=1) * acc_s[...] + pv

    chunk(qt, 0, "causal")

    @pl.when(qt >= 1)
    def _():
        chunk(qt - 1, 1, None)

    if mode == "sel":
        def far_body(kt, carry):
            chunk(kt, None, None)
            return carry
        lax.fori_loop(0, jnp.maximum(qt - 1, 0), far_body, 0)
    else:
        @pl.when(qt >= 2)
        def _():
            chunk(qt - 2, None, "window")

    acc = acc_s[...]
    o = acc[:, 0:HEAD_DIM] * (1.0 / jnp.maximum(acc[:, HEAD_DIM:HEAD_DIM + 1], 1e-30))
    nsa = nsa_ref[...]
    lane = lax.broadcasted_iota(jnp.int32, nsa.shape, 1)
    for r in range(GQA):
        gidx = (g * GQA + r) * 3 + branch
        gate = jnp.sum(jnp.where(lane == gidx, nsa, 0.0), axis=-1, keepdims=True)
        hs = slice(r * HEAD_DIM, (r + 1) * HEAD_DIM)
        o_ref[:, hs] = prev_ref[:, hs] + o[r * tq:(r + 1) * tq] * gate


def _prompt_attention(proj, rel_bias, tables, prev, bsz, t, mode, sel=None, tq=ATT_TILE):
    nq = t // tq
    n = proj.shape[1]
    assert t // SEL_BLOCK <= SEL_BLOCK and tq % ROW_BLOCK == 0
    if mode == "sel":
        tk_, tv_, branch = T_KS, T_VS, 1
    else:
        tk_, tv_, branch = T_KW, T_VW, 2
        assert WINDOW == 2 * tq
    in_specs = [pl.BlockSpec(memory_space=pltpu.SMEM),
                pl.BlockSpec((None, tq, PROJ_TILE), lambda b, g, i: (T_Q + g, b * nq + i, 0)),
                pl.BlockSpec((None, tq, PROJ_TILE), lambda b, g, i: (T_NSA, b * nq + i, 0)),
                pl.BlockSpec((None, t, HEAD_DIM), lambda b, g, i: (tk_, b, g)),
                pl.BlockSpec((None, t, HEAD_DIM), lambda b, g, i: (tv_, b, g)),
                pl.BlockSpec((None, 2, GQA * tq, tq), lambda b, g, i: (g, 0, 0, 0)),
                pl.BlockSpec((tq, GQA * HEAD_DIM), lambda b, g, i: (b * nq + i, g))]
    args = [rel_bias, proj, proj, proj, proj, tables, prev]
    if mode == "sel":
        in_specs += [pl.BlockSpec((None, None, tq, LANE), lambda b, g, i: (b, g, i, 0))]
        args += [sel]
    return pl.pallas_call(
        functools.partial(_attn_kernel, mode=mode, tq=tq, branch=branch),
        grid=(bsz, N_KV, nq),
        in_specs=in_specs,
        out_specs=pl.BlockSpec((tq, GQA * HEAD_DIM), lambda b, g, i: (b * nq + i, g)),
        out_shape=jax.ShapeDtypeStruct((n, N_HEADS * HEAD_DIM), F32),
        scratch_shapes=[pltpu.VMEM((t, AUG), BF16), pltpu.VMEM((t, AUG), BF16),
                        pltpu.VMEM((GQA * tq, AUG), BF16),
                        pltpu.VMEM((GQA * tq, tq), F32), pltpu.VMEM((GQA * tq, tq), BF16),
                        pltpu.VMEM((GQA * tq, LANE), F32), pltpu.VMEM((GQA * tq, LANE), F32),
                        pltpu.VMEM((GQA * tq, AUG), F32)],
        compiler_params=_cparams(("parallel", "parallel", "arbitrary")),
        name="attn_" + mode,
    )(*args)


def _masked_softmax(s, mask):
    s = jnp.where(mask, s, NEG)
    m = jnp.max(s, axis=-1, keepdims=True)
    p = jnp.where(mask, jnp.exp(s - m), 0.0)
    den = jnp.sum(p, axis=-1, keepdims=True)
    return p * (1.0 / jnp.maximum(den, 1e-30))


def _near_far_bias(rb_ref, g, dist_near, tdec, width, near):
    rows = []
    for r in range(GQA):
        h = g * GQA + r
        rbs = [rb_ref[k, h] for k in range(N_BUCKETS)]
        nb_ = _bias_chain(dist_near, rbs)
        rows.append(jnp.concatenate([jnp.full((tdec, width - near), rbs[-1], F32), nb_], axis=1))
    return jnp.concatenate(rows, axis=0)


def _sattn_kernel(pt_ref, rb_ref, q0_ref, q1_ref, q2_ref, q3_ref, ksn_ref, vsn_ref, kwn_ref, vwn_ref, nsa_ref,
                  kc_ref, vc_ref, *rest, npg, page, tdec, wlen):
    kpages = rest[:npg]
    vpages = rest[npg:2 * npg]
    skw_ref, svw_ref, e_ref, yb_ref, kwo_ref, vwo_ref, kbuf, vbuf, wkb, wvb = rest[2 * npg:]
    past = npg * page
    lk = past + LANE
    nb = kc_ref.shape[0]
    n_sel = -(-(past + tdec) // SEL_BLOCK)
    rows_w = wlen * N_KV
    near = 2 * LANE
    rq = GQA * tdec
    nsa = nsa_ref[...]

    kwo_ref[0:rows_w - tdec * N_KV, :] = skw_ref[tdec * N_KV:rows_w, :]
    vwo_ref[0:rows_w - tdec * N_KV, :] = svw_ref[tdec * N_KV:rows_w, :]
    for g in range(N_KV):
        kwo_ref[pl.ds(rows_w - tdec * N_KV + g, tdec, stride=N_KV), :] = kwn_ref[:, g * HEAD_DIM:(g + 1) * HEAD_DIM]
        vwo_ref[pl.ds(rows_w - tdec * N_KV + g, tdec, stride=N_KV), :] = vwn_ref[:, g * HEAD_DIM:(g + 1) * HEAD_DIM]

    trow1 = lax.broadcasted_iota(jnp.int32, (tdec, LANE), 0) + past
    col1 = lax.broadcasted_iota(jnp.int32, (tdec, LANE), 1)
    dist_c = trow1 - (col1 * CMP_BLOCK + CMP_BLOCK - 1)
    valid_c = (dist_c >= 0) & (col1 < nb)
    cur = trow1 // SEL_BLOCK
    forced = (col1 == 0) | (col1 == cur) | (col1 == cur - 1)

    trow_s = (lax.broadcasted_iota(jnp.int32, (rq, lk), 0) & (tdec - 1)) + past
    pos_s = lax.broadcasted_iota(jnp.int32, (rq, lk), 1)
    causal_s = pos_s <= trow_s
    dist_sn = (lax.broadcasted_iota(jnp.int32, (tdec, near), 0) + past) - (lax.broadcasted_iota(jnp.int32, (tdec, near), 1) + lk - near)

    wl = wlen + LANE
    qidx_w = (lax.broadcasted_iota(jnp.int32, (rq, wl), 0) & (tdec - 1)) + wlen
    kidx_w = lax.broadcasted_iota(jnp.int32, (rq, wl), 1)
    dist_w = qidx_w - kidx_w
    mask_w = (dist_w >= 0) & (dist_w < WINDOW)
    dist_wn = (lax.broadcasted_iota(jnp.int32, (tdec, near), 0) + wlen) - (lax.broadcasted_iota(jnp.int32, (tdec, near), 1) + wl - near)

    zpad = jnp.zeros((LANE - tdec, HEAD_DIM), F32)
    cpad = jnp.zeros((LANE - nb, HEAD_DIM), F32)
    for g in range(N_KV):
        gs = slice(g * HEAD_DIM, (g + 1) * HEAD_DIM)
        q = (q0_ref, q1_ref, q2_ref, q3_ref)[g][...]
        qst = jnp.concatenate([q[:, r * HEAD_DIM:(r + 1) * HEAD_DIM] for r in range(GQA)], axis=0).astype(BF16)

        kg = jnp.concatenate([kc_ref[:, g, :], cpad], axis=0).astype(BF16)
        vg = jnp.concatenate([vc_ref[:, g, :], cpad], axis=0).astype(BF16)
        s = _dot_nt(qst, kg) * ATTN_SCALE
        ps = []
        imp = jnp.zeros((tdec, LANE), F32)
        for r in range(GQA):
            h = g * GQA + r
            b = _bias_chain(dist_c, [rb_ref[k, h] for k in range(N_BUCKETS)])
            p = _masked_softmax(s[r * tdec:(r + 1) * tdec] + b, valid_c)
            imp = imp + p
            ps.append(p)
        o_c = jnp.dot(jnp.concatenate(ps, axis=0).astype(BF16), vg, preferred_element_type=F32)

        score = jnp.where(col1 <= cur, imp + jnp.where(forced, FORCE_BONUS, 0.0), NEG)
        score = jnp.where(col1 < n_sel, score, -3e38)
        rank = jnp.zeros((tdec, LANE), F32)
        for i in range(n_sel):
            ci = score[:, i:i + 1]
            rank = rank + ((ci > score) | ((ci == score) & (col1 > i))).astype(F32)
        sel = ((rank < float(min(N_SEL, n_sel))) & (col1 < n_sel)).astype(F32)

        for p_ in range(npg):
            kbuf[p_ * page:(p_ + 1) * page, :] = kpages[p_][pl.ds(g, page, stride=N_KV), :].astype(BF16)
            vbuf[p_ * page:(p_ + 1) * page, :] = vpages[p_][pl.ds(g, page, stride=N_KV), :].astype(BF16)
        kbuf[past:lk, :] = jnp.concatenate([ksn_ref[:, gs], zpad], axis=0).astype(BF16)
        vbuf[past:lk, :] = jnp.concatenate([vsn_ref[:, gs], zpad], axis=0).astype(BF16)
        s = _dot_nt(qst, kbuf[...]) * ATTN_SCALE + _near_far_bias(rb_ref, g, dist_sn, tdec, lk, near)
        sel4 = jnp.concatenate([sel] * GQA, axis=0).astype(BF16)
        mask = (jnp.dot(sel4, e_ref[...], preferred_element_type=F32) > 0.5) & causal_s
        o_s = jnp.dot(_masked_softmax(s, mask).astype(BF16), vbuf[...], preferred_element_type=F32)

        wkb[0:wlen, :] = skw_ref[pl.ds(g, wlen, stride=N_KV), :].astype(BF16)
        wvb[0:wlen, :] = svw_ref[pl.ds(g, wlen, stride=N_KV), :].astype(BF16)
        wkb[wlen:wl, :] = jnp.concatenate([kwn_ref[:, gs], zpad], axis=0).astype(BF16)
        wvb[wlen:wl, :] = jnp.concatenate([vwn_ref[:, gs], zpad], axis=0).astype(BF16)
        s = _dot_nt(qst, wkb[...]) * ATTN_SCALE + _near_far_bias(rb_ref, g, dist_wn, tdec, wl, near)
        o_w = jnp.dot(_masked_softmax(s, mask_w).astype(BF16), wvb[...], preferred_element_type=F32)

        for r in range(GQA):
            h = g * GQA + r
            rs = slice(r * tdec, (r + 1) * tdec)
            yb_ref[:, h * HEAD_DIM:(h + 1) * HEAD_DIM] = (nsa[:, 3 * h:3 * h + 1] * o_c[rs]
                                                         + nsa[:, 3 * h + 1:3 * h + 2] * o_s[rs]
                                                         + nsa[:, 3 * h + 2:3 * h + 3] * o_w[rs])


def _sample_attention(proj, rel_bias, page_table, kcmp_g, vcmp_g, ck_sel, cv_sel, skw, svw, emat, bsz, tdec):
    npg = page_table.shape[1]
    page = ck_sel.shape[1] // N_KV
    wlen = skw.shape[1] // N_KV
    nb = kcmp_g.shape[1]
    past = npg * page
    lk = past + LANE
    assert tdec == SUBLANE and FAR_DIST <= LANE and wlen == WINDOW

    def tile(k):
        return pl.BlockSpec((None, tdec, PROJ_TILE), lambda b, pt, k=k: (k, b, 0))

    def pagespec(p_):
        return pl.BlockSpec((None, page * N_KV, HEAD_DIM), lambda b, pt, p_=p_: (pt[b, p_], 0, 0))

    in_specs = ([pl.BlockSpec(memory_space=pltpu.SMEM)]
                + [tile(T_Q + g) for g in range(N_KV)]
                + [tile(T_KS), tile(T_VS), tile(T_KW), tile(T_VW), tile(T_NSA)]
                + [pl.BlockSpec((None, nb, SUBLANE, HEAD_DIM), lambda b, pt: (b, 0, 0, 0))] * 2
                + [pagespec(p_) for p_ in range(npg)] * 2
                + [pl.BlockSpec((None, wlen * N_KV, HEAD_DIM), lambda b, pt: (b, 0, 0))] * 2
                + [pl.BlockSpec((LANE, lk), lambda b, pt: (0, 0))])
    grid_spec = pltpu.PrefetchScalarGridSpec(
        num_scalar_prefetch=1,
        grid=(bsz,),
        in_specs=in_specs,
        out_specs=[pl.BlockSpec((tdec, N_HEADS * HEAD_DIM), lambda b, pt: (b, 0)),
                   pl.BlockSpec((None, wlen * N_KV, HEAD_DIM), lambda b, pt: (b, 0, 0)),
                   pl.BlockSpec((None, wlen * N_KV, HEAD_DIM), lambda b, pt: (b, 0, 0))],
        scratch_shapes=[pltpu.VMEM((lk, HEAD_DIM), BF16), pltpu.VMEM((lk, HEAD_DIM), BF16),
                        pltpu.VMEM((wlen + LANE, HEAD_DIM), BF16), pltpu.VMEM((wlen + LANE, HEAD_DIM), BF16)])
    return pl.pallas_call(
        functools.partial(_sattn_kernel, npg=npg, page=page, tdec=tdec, wlen=wlen),
        grid_spec=grid_spec,
        out_shape=[jax.ShapeDtypeStruct((bsz * tdec, N_HEADS * HEAD_DIM), F32),
                   jax.ShapeDtypeStruct((bsz, wlen * N_KV, HEAD_DIM), F32),
                   jax.ShapeDtypeStruct((bsz, wlen * N_KV, HEAD_DIM), F32)],
        compiler_params=_cparams(("arbitrary",)),
        name="sample_attention",
    )(page_table, rel_bias, *([proj] * 9), kcmp_g, vcmp_g, *([ck_sel] * npg), *([cv_sel] * npg), skw, svw, emat)


def _merge_kernel(ya_ref, yb_ref, wa_ref, wb_ref, ga_ref, gb_ref, t_ref, yb_scr):
    @pl.when(pl.program_id(1) == 0)
    def _():
        yb_scr[...] = yb_ref[...].astype(BF16)

    a = jnp.dot(ya_ref[...], wa_ref[...], preferred_element_type=F32)
    b = jnp.dot(yb_scr[...], wb_ref[...], preferred_element_type=F32)
    t_ref[...] = (ga_ref[...] * a + gb_ref[...] * b).astype(BF16)


def _merge(proj, ya, yb, wa_b, wb_b, tm=1024):
    n, aw = ya.shape
    d = wb_b.shape[0]
    tm = min(tm, n)
    nj = d // PROJ_TILE
    return pl.pallas_call(
        _merge_kernel,
        grid=(n // tm, nj),
        in_specs=[pl.BlockSpec((tm, aw), lambda i, j: (i, 0)),
                  pl.BlockSpec((tm, d), lambda i, j: (i, 0)),
                  pl.BlockSpec((aw, PROJ_TILE), lambda i, j: (0, j)),
                  pl.BlockSpec((d, PROJ_TILE), lambda i, j: (0, j)),
                  pl.BlockSpec((None, tm, PROJ_TILE), lambda i, j: (T_GA + j, i, 0)),
                  pl.BlockSpec((None, tm, PROJ_TILE), lambda i, j: (T_GB + j, i, 0))],
        out_specs=pl.BlockSpec((tm, PROJ_TILE), lambda i, j: (i, j)),
        out_shape=jax.ShapeDtypeStruct((n, d), BF16),
        scratch_shapes=[pltpu.VMEM((tm, d), BF16)],
        compiler_params=_cparams(("parallel", "arbitrary")),
        name="merge",
    )(ya, yb, wa_b, wb_b, proj, proj)


def _outproj_kernel(t_ref, x_ref, gt_ref, sc_ref, sh_ref, gn_ref, wo_ref, x1_ref, h2_ref):
    y = jnp.dot(t_ref[...], wo_ref[...], preferred_element_type=F32)
    x1 = x_ref[...] + gt_ref[...] * y.reshape(x_ref.shape)
    x1_ref[...] = x1
    r = lax.rsqrt(jnp.mean(x1 * x1, axis=-1, keepdims=True) + EPS)
    h2 = (x1 * r) * gn_ref[...] * (1.0 + sc_ref[...]) + sh_ref[...]
    h2_ref[...] = h2.reshape(h2_ref.shape).astype(BF16)


def _out_projection(tmix, x3, mod4, g_n2, wo_b, bt, tt):
    nb, tb, d = x3.shape
    tpb = tb // tt
    tm = bt * tt
    n = nb * tb

    def modspec(k):
        return pl.BlockSpec((bt, None, 1, d), lambda i, k=k: (i // tpb, k, 0, 0))

    return pl.pallas_call(
        _outproj_kernel,
        grid=(n // tm,),
        in_specs=[pl.BlockSpec((tm, d), lambda i: (i, 0)),
                  pl.BlockSpec((bt, tt, d), lambda i: (i // tpb, i % tpb, 0)),
                  modspec(2), modspec(4), modspec(3),
                  pl.BlockSpec((1, 1, d), lambda i: (0, 0, 0)),
                  pl.BlockSpec((d, d), lambda i: (0, 0))],
        out_specs=[pl.BlockSpec((bt, tt, d), lambda i: (i // tpb, i % tpb, 0)),
                   pl.BlockSpec((tm, d), lambda i: (i, 0))],
        out_shape=[jax.ShapeDtypeStruct((nb, tb, d), F32), jax.ShapeDtypeStruct((n, d), BF16)],
        compiler_params=_cparams(("arbitrary",)),
        name="out_projection",
    )(tmix, x3, mod4, mod4, mod4, g_n2.reshape(1, 1, d), wo_b)


def _peer_scores_kernel(h_ref, wpq_ref, sk1_ref, sk2_ref, s1_ref, s2_ref):
    pq = jnp.dot(h_ref[...], wpq_ref[...], preferred_element_type=F32)
    kd = sk1_ref.shape[1]
    for hd in range(PEER_HEADS):
        q1 = pq[:, hd * 2 * kd:hd * 2 * kd + kd].astype(BF16)
        q2 = pq[:, hd * 2 * kd + kd:(hd + 1) * 2 * kd].astype(BF16)
        s1_ref[hd] = _dot_nt(sk1_ref[...], q1)
        s2_ref[hd] = _dot_nt(sk2_ref[...], q2)


def _peer_scores(h2, wpq_b, sk1_b, sk2_b, tm=512):
    n, d = h2.shape
    dq = wpq_b.shape[1]
    nk, kd = sk1_b.shape
    return pl.pallas_call(
        _peer_scores_kernel,
        grid=(n // tm,),
        in_specs=[pl.BlockSpec((tm, d), lambda i: (i, 0)),
                  pl.BlockSpec((d, dq), lambda i: (0, 0)),
                  pl.BlockSpec((nk, kd), lambda i: (0, 0)),
                  pl.BlockSpec((nk, kd), lambda i: (0, 0))],
        out_specs=[pl.BlockSpec((PEER_HEADS, nk, tm), lambda i: (0, 0, i))] * 2,
        out_shape=[jax.ShapeDtypeStruct((PEER_HEADS, nk, n), F32)] * 2,
        compiler_params=_cparams(("arbitrary",)),
        name="peer_scores",
    )(h2, wpq_b, sk1_b, sk2_b)


def _staircase():
    return [(a, b) for a in range(PEER_TOPK) for b in range(PEER_TOPK) if (a + 1) * (b + 1) <= PEER_TOPK]


def _extract_top(s, rows_f, exact):
    vals = []
    rank = jnp.full(s.shape, float(PEER_TOPK), F32)
    for a in range(PEER_TOPK):
        m = jnp.max(s, axis=0, keepdims=True)
        hit = s == m
        if exact:
            hit = rows_f == jnp.min(jnp.where(hit, rows_f, 1e9), axis=0, keepdims=True)
        rank = jnp.where(hit, float(a), rank)
        s = jnp.where(hit, -jnp.inf, s)
        vals.append(m)
    return vals, rank


def _peer_topk_kernel(s1_ref, s2_ref, cnt_ref, e1_ref, rk_ref, e2_ref):
    nk, tn = s1_ref.shape[1], s1_ref.shape[2]
    rows_f = lax.broadcasted_iota(jnp.int32, (nk, tn), 0).astype(F32)
    pairs = _staircase()
    npad = -(-len(pairs) // SUBLANE) * SUBLANE
    prow = lax.broadcasted_iota(jnp.int32, (npad, tn), 0)
    flat_f = jnp.full((npad, tn), 1e9, F32)
    arow_f = jnp.full((npad, tn), -1.0, F32)
    for i, (a, b) in enumerate(pairs):
        flat_f = jnp.where(prow == i, float(a * PEER_TOPK + b), flat_f)
        arow_f = jnp.where(prow == i, float(a), arow_f)
    k_f = float(PEER_TOPK)

    def one_head(hd, exact):
        s1 = s1_ref[hd]
        s2 = s2_ref[hd]
        v1, rank1 = _extract_top(s1, rows_f, exact)
        v2, rank2 = _extract_top(s2, rows_f, exact)
        cand = jnp.full((npad, tn), -jnp.inf, F32)
        for i, (a, b) in enumerate(pairs):
            cand = jnp.where(prow == i, v1[a] + v2[b], cand)
        m0 = v1[0] + v2[0]
        c = cand
        selected = jnp.zeros((npad, tn), F32)
        for _ in range(PEER_TOPK):
            m = jnp.max(c, axis=0, keepdims=True)
            hit = c == m
            if exact:
                hit = flat_f == jnp.min(jnp.where(hit, flat_f, 2e9), axis=0, keepdims=True)
            selected = jnp.where(hit, 1.0, selected)
            c = jnp.where(hit, -jnp.inf, c)
        z = jnp.sum(jnp.where(selected > 0.5, jnp.exp(cand - m0), 0.0), axis=0, keepdims=True)
        cnt1 = jnp.zeros((nk, tn), F32)
        for a in range(PEER_TOPK):
            cnt_a = jnp.sum(jnp.where(arow_f == float(a), selected, 0.0), axis=0, keepdims=True)
            cnt1 = jnp.where(rank1 == float(a), cnt_a, cnt1)
        cnt_ref[hd] = cnt1
        e1_ref[hd] = jnp.exp(s1 - v1[0]) * (1.0 / z)
        rk_ref[hd] = rank2
        e2_ref[hd] = jnp.exp(s2 - v2[0])
        if exact:
            return None
        n1 = jnp.sum(jnp.where(rank1 < k_f, 1.0, 0.0), axis=0, keepdims=True)
        n2 = jnp.sum(jnp.where(rank2 < k_f, 1.0, 0.0), axis=0, keepdims=True)
        n3 = jnp.sum(selected, axis=0, keepdims=True)
        return jnp.where((n1 != k_f) | (n2 != k_f) | (n3 != k_f), 1.0, 0.0)

    def body(hd, carry):
        tie = one_head(hd, False)

        @pl.when(jnp.max(tie) > 0.0)
        def _():
            one_head(hd, True)

        return carry

    lax.fori_loop(0, PEER_HEADS, body, 0)


def _peer_topk(s1t, s2t, tn=256):
    nh, nk, n = s1t.shape
    spec = pl.BlockSpec((nh, nk, tn), lambda i: (0, 0, i))
    return pl.pallas_call(
        _peer_topk_kernel,
        grid=(n // tn,),
        in_specs=[spec, spec],
        out_specs=[spec] * 4,
        out_shape=[jax.ShapeDtypeStruct((nh, nk, n), F32)] * 4,
        compiler_params=_cparams(("arbitrary",)),
        name="peer_topk",
    )(s1t, s2t)


PEER_SUB = 256


def _peer_dense_kernel(h_ref, eu_ref, ev_ref, cnt_ref, e1_ref, rk_ref, e2_ref, o_ref, at_scr, wa_scr, *, te):
    e = pl.program_id(1)
    nk = rk_ref.shape[1]
    tm, d = h_ref.shape

    @pl.when(e == 0)
    def _():
        o_ref[...] = jnp.zeros(o_ref.shape, F32)

    n_i1 = PEER_SUB // nk
    nsub = te // PEER_SUB
    assert te // nk == SUBLANE
    i1_base = pl.multiple_of(e * SUBLANE, SUBLANE)
    tok_piece = 2 * LANE
    n_tok = tm // tok_piece
    col_piece = 2 * LANE
    n_col = d // col_piece
    tiles = [(il, tb) for il in range(n_i1) for tb in range(tm // LANE)]

    def pre_activation(sb, k):
        ts_ = slice(k * tok_piece, (k + 1) * tok_piece)
        at_scr[sb, :, ts_] = _dot_nt(eu_ref[sb * PEER_SUB:(sb + 1) * PEER_SUB, :], h_ref[ts_, :])

    def down_projection(sb, k):
        cs_ = slice(k * col_piece, (k + 1) * col_piece)
        o_ref[:, cs_] += _dot_tn(wa_scr[sb], ev_ref[sb * PEER_SUB:(sb + 1) * PEER_SUB, cs_])

    def gate_tile(sb, il, tb):
        j1 = sb * n_i1 + il
        ks = slice(il * nk, (il + 1) * nk)
        cs = slice(tb * LANE, (tb + 1) * LANE)
        w = jnp.zeros((nk, LANE), F32)
        for hd in range(PEER_HEADS):
            c = cnt_ref[hd, pl.ds(i1_base, SUBLANE), cs][j1:j1 + 1, :]
            g1 = e1_ref[hd, pl.ds(i1_base, SUBLANE), cs][j1:j1 + 1, :]
            w = w + jnp.where(rk_ref[hd, :, cs] < c, e2_ref[hd, :, cs] * g1, 0.0)
        wa_scr[sb, ks, cs] = (w * _gelu(at_scr[sb, ks, cs])).astype(BF16)

    for k in range(n_tok):
        pre_activation(0, k)
    nslot = max(len(tiles), n_col)
    for sb in range(nsub):
        for k in range(nslot):
            if sb >= 1 and k < n_col:
                down_projection(sb - 1, k)
            if sb + 1 < nsub and k % (nslot // n_tok) == 0:
                pre_activation(sb + 1, k // (nslot // n_tok))
            if k < len(tiles):
                gate_tile(sb, *tiles[k])
    for k in range(n_col):
        down_projection(nsub - 1, k)


def _peer_dense(h2, eu_b, ev_b, cnt1, e1, rk2, e2, tm=512, te=1024):
    n, d = h2.shape
    ne = eu_b.shape[0]
    nh, nk, _ = cnt1.shape
    res = pl.BlockSpec((nh, nk, tm), lambda i, e: (0, 0, i))
    return pl.pallas_call(
        functools.partial(_peer_dense_kernel, te=te),
        grid=(n // tm, ne // te),
        in_specs=[pl.BlockSpec((tm, d), lambda i, e: (i, 0)),
                  pl.BlockSpec((te, d), lambda i, e: (e, 0)),
                  pl.BlockSpec((te, d), lambda i, e: (e, 0)),
                  res, res, res, res],
        out_specs=pl.BlockSpec((tm, d), lambda i, e: (i, 0)),
        out_shape=jax.ShapeDtypeStruct((n, d), F32),
        scratch_shapes=[pltpu.VMEM((te // PEER_SUB, PEER_SUB, tm), F32), pltpu.VMEM((te // PEER_SUB, PEER_SUB, tm), BF16)],
        compiler_params=_cparams(("parallel", "arbitrary")),
        name="peer_dense",
    )(h2, eu_b, ev_b, cnt1, e1, rk2, e2)


def _final_kernel(x1_ref, p_ref, gt_ref, o_ref):
    o_ref[...] = x1_ref[...] + gt_ref[...] * p_ref[...].reshape(x1_ref.shape)


def _final_residual(x1, peer, row0, mod4, bt, tt):
    nb, tb, d = x1.shape
    tpb = tb // tt
    tm = bt * tt
    n = nb * tb
    off = row0 // tm
    return pl.pallas_call(
        _final_kernel,
        grid=(n // tm,),
        in_specs=[pl.BlockSpec((bt, tt, d), lambda i: (i // tpb, i % tpb, 0)),
                  pl.BlockSpec((tm, d), lambda i: (off + i, 0)),
                  pl.BlockSpec((bt, None, 1, d), lambda i: (i // tpb, 5, 0, 0))],
        out_specs=pl.BlockSpec((bt, tt, d), lambda i: (i // tpb, i % tpb, 0)),
        out_shape=jax.ShapeDtypeStruct((nb, tb, d), F32),
        compiler_params=_cparams(("arbitrary",)),
        name="final_residual",
    )(x1, peer, mod4)


def _block_expand(n_cols, width=LANE):
    j = np.arange(width)[:, None]
    s = np.arange(n_cols)[None, :]
    return (s // SEL_BLOCK == j).astype(np.float32)


def _forward(x_prompt, x_sample, cache_k_cmp, cache_v_cmp, cache_k_sel, cache_v_sel, state_k_win, state_v_win,
             page_table, c_prompt, c_sample, rel_bias, w_ada, b_ada, g_n1, g_n2, w_in, ln_v_g, ln_v_b, w_s, b_s,
             g_q, g_k, pe_k, w_c1k, w_c2k, pe_v, w_c1v, w_c2v, w_a, w_b, w_o, w_pq, sk1, sk2, expert_u, expert_v):
    assert w_ada.shape[0] == 1, "single layer"
    bp, tp, d = x_prompt.shape
    bs, ts, _ = x_sample.shape
    np_, ns_ = bp * tp, bs * ts
    (w_ada, b_ada, g_n1, g_n2, w_in, ln_v_g, ln_v_b, w_s, b_s, g_q, g_k, pe_k, w_c1k, w_c2k, pe_v, w_c1v, w_c2v,
     w_a, w_b, w_o, w_pq, sk1, sk2, expert_u, expert_v) = [a[0] for a in (
         w_ada, b_ada, g_n1, g_n2, w_in, ln_v_g, ln_v_b, w_s, b_s, g_q, g_k, pe_k, w_c1k, w_c2k, pe_v, w_c1v, w_c2v,
         w_a, w_b, w_o, w_pq, sk1, sk2, expert_u, expert_v)]

    n_gate = 3 * N_HEADS
    c0 = T_NSA * PROJ_TILE
    w_in_p = jnp.concatenate([w_in[:, :c0],
                              jnp.pad(w_in[:, c0:c0 + n_gate], ((0, 0), (0, PROJ_TILE - n_gate))),
                              w_in[:, c0 + n_gate:]], axis=1).astype(BF16)
    ones = jnp.ones((PROJ_TILE,), F32)
    zeros = jnp.zeros((PROJ_TILE,), F32)
    rep = PROJ_TILE // HEAD_DIM
    gains = [ones] * N_TILES
    flags = [zeros] * N_TILES
    for k in range(T_Q, T_KC):
        gains[k], flags[k] = jnp.tile(g_q, rep), ones
    gains[T_KS], flags[T_KS] = jnp.tile(g_k[1], rep), ones
    gains[T_KW], flags[T_KW] = jnp.tile(g_k[2], rep), ones
    gain = jnp.stack(gains)[:, None, :]
    flag = jnp.stack(flags)[:, None, :]
    tril = jnp.tril(w_s)
    wm_p = tril.astype(BF16)
    bsb_p = jnp.broadcast_to(b_s[:, :, None], (A_GROUPS, CHUNK, CHUNK))
    nrep = CHUNK // ts
    wm_s = jnp.einsum("ab,gij->gaibj", jnp.eye(nrep, dtype=F32), tril[:, :ts, :ts]).reshape(A_GROUPS, CHUNK, CHUNK).astype(BF16)
    bsb_s = jnp.broadcast_to(jnp.tile(b_s[:, :ts], (1, nrep))[:, :, None], (A_GROUPS, CHUNK, CHUNK))
    w1k_b, w2k_b, w1v_b, w2v_b = [a.astype(BF16) for a in (w_c1k, w_c2k, w_c1v, w_c2v)]
    wa_b, wb_b, wo_b, wpq_b = [a.astype(BF16) for a in (w_a, w_b, w_o, w_pq)]
    sk1_b, sk2_b = sk1.astype(BF16), sk2.astype(BF16)
    eu_b, ev_b = expert_u.astype(BF16), expert_v.astype(BF16)
    one_gain = jnp.ones((HEAD_DIM,), F32)

    def pool_weights(w1, pe):
        hid = w1.shape[1]
        w1x = w1.reshape(CMP_BLOCK // 2, 2, HEAD_DIM, hid).transpose(0, 2, 1, 3).reshape(CMP_BLOCK // 2 * HEAD_DIM, 2 * hid)
        pe8 = jnp.repeat(pe.reshape(CMP_BLOCK // 2, 2, 1, HEAD_DIM), N_KV, axis=2).reshape(CMP_BLOCK // 2, SUBLANE, HEAD_DIM)
        return w1x.astype(BF16), pe8

    w1k_x, pe8_k = pool_weights(w_c1k, pe_k)
    w1v_x, pe8_v = pool_weights(w_c1v, pe_v)

    nc = bp + bs
    ncp = -(-nc // SUBLANE) * SUBLANE
    c_all = jnp.pad(jnp.concatenate([c_prompt, c_sample], axis=0), ((0, ncp - nc), (0, 0)))
    mod = _modulation(c_all, w_ada, b_ada)
    mod_p = mod[:bp].reshape(bp, 6, 1, d)
    mod_s = mod[bp:nc].reshape(bs, 6, 1, d)

    tm_p = min(1024, tp)
    bt_s = min(1024 // ts, bs)

    proj_p = _in_projection(x_prompt, mod_p, g_n1, w_in_p, gain, flag, 1, tm_p)
    proj_s = _in_projection(x_sample, mod_s, g_n1, w_in_p, gain, flag, bt_s, ts)

    cpb = tp // CHUNK
    ya_p, vch_p = _mixer_a(proj_p, ln_v_g, ln_v_b, wm_p, bsb_p, bp, lambda i: i // cpb)
    ya_s, vch_s = _mixer_a(proj_s, ln_v_g, ln_v_b, wm_s, bsb_s, ns_ // CHUNK, lambda i: i)

    kcmp_p = _compress_prompt(proj_p, T_KC, bp, tp, pe_k, w1k_b, w2k_b, g_k[0], True)
    vcmp_p = _compress_prompt(proj_p, T_VC, bp, tp, pe_v, w1v_b, w2v_b, one_gain, False)
    n_phys, page = cache_k_cmp.shape[1], cache_k_cmp.shape[2]
    bpp = page // CMP_BLOCK
    blk_rows = CMP_BLOCK * N_KV
    kcmp_pool = _compress_pool(cache_k_cmp.reshape(n_phys * bpp, blk_rows, HEAD_DIM), pe8_k, w1k_x, w2k_b, g_k[0], True)
    vcmp_pool = _compress_pool(cache_v_cmp.reshape(n_phys * bpp, blk_rows, HEAD_DIM), pe8_v, w1v_x, w2v_b, one_gain, False)
    npg = page_table.shape[1]
    kcmp_s = kcmp_pool.reshape(n_phys, bpp * SUBLANE * HEAD_DIM)[page_table].reshape(bs, npg * bpp, SUBLANE, HEAD_DIM)
    vcmp_s = vcmp_pool.reshape(n_phys, bpp * SUBLANE * HEAD_DIM)[page_table].reshape(bs, npg * bpp, SUBLANE, HEAD_DIM)

    tables, ctab = _bias_tables(rel_bias, ATT_TILE)
    oc_p, sel_p = _cmp_select(proj_p, ctab, kcmp_p, vcmp_p, bp, tp)
    yb_p = _prompt_attention(proj_p, rel_bias, tables, oc_p, bp, tp, "sel", sel_p)
    yb_p = _prompt_attention(proj_p, rel_bias, tables, yb_p, bp, tp, "win")

    past = npg * page
    emat_s = jnp.asarray(_block_expand(past + LANE), BF16)
    yb_s, kwin_s, vwin_s = _sample_attention(
        proj_s, rel_bias, page_table, kcmp_s, vcmp_s,
        cache_k_sel.reshape(n_phys, page * N_KV, HEAD_DIM), cache_v_sel.reshape(n_phys, page * N_KV, HEAD_DIM),
        state_k_win.reshape(bs, -1, HEAD_DIM), state_v_win.reshape(bs, -1, HEAD_DIM), emat_s, bs, ts)

    t_p = _merge(proj_p, ya_p, yb_p, wa_b, wb_b)
    t_s = _merge(proj_s, ya_s, yb_s, wa_b, wb_b)
    x1_p, h2_p = _out_projection(t_p, x_prompt, mod_p, g_n2, wo_b, 1, min(256, tp))
    x1_s, h2_s = _out_projection(t_s, x_sample, mod_s, g_n2, wo_b, min(256 // ts, bs), ts)

    h2 = jnp.concatenate([h2_p, h2_s], axis=0)
    s1t, s2t = _peer_scores(h2, wpq_b, sk1_b, sk2_b)
    cnt1, e1, rk2, e2 = _peer_topk(s1t, s2t)
    peer = _peer_dense(h2, eu_b, ev_b, cnt1, e1, rk2, e2)
    y_p = _final_residual(x1_p, peer, 0, mod_p, 1, min(512, tp))
    y_s = _final_residual(x1_s, peer, np_, mod_s, min(512 // ts, bs), ts)

    def kv_p(k):
        return proj_p[k].reshape(1, bp, tp, N_KV, HEAD_DIM)

    def kv_s(k):
        return proj_s[k].reshape(1, bs, ts, N_KV, HEAD_DIM)

    wb_p = min(WINDOW, tp)
    wlen = state_k_win.shape[2]
    return (y_p, y_s,
            kv_p(T_KC), kv_p(T_VC), kv_p(T_KS), kv_p(T_VS),
            kv_p(T_KW)[:, :, tp - wb_p:], kv_p(T_VW)[:, :, tp - wb_p:],
            vch_p.reshape(1, bp, CHUNK, -1),
            kv_s(T_KC), kv_s(T_VC), kv_s(T_KS), kv_s(T_VS),
            kwin_s.reshape(1, bs, wlen, N_KV, HEAD_DIM), vwin_s.reshape(1, bs, wlen, N_KV, HEAD_DIM),
            vch_s.reshape(1, bs, ts, -1))


def kernel(x_prompt, x_sample, cache_k_cmp, cache_v_cmp, cache_k_sel, cache_v_sel, state_k_win, state_v_win, page_table, c_prompt, c_sample, rel_bias, w_ada, b_ada, g_n1, g_n2, w_in, ln_v_g, ln_v_b, w_s, b_s, g_q, g_k, pe_k, w_c1k, w_c2k, pe_v, w_c1v, w_c2v, w_a, w_b, w_o, w_pq, sk1, sk2, expert_u, expert_v):
    return _forward(x_prompt, x_sample, cache_k_cmp, cache_v_cmp, cache_k_sel, cache_v_sel, state_k_win, state_v_win,
                    page_table, c_prompt, c_sample, rel_bias, w_ada, b_ada, g_n1, g_n2, w_in, ln_v_g, ln_v_b, w_s, b_s,
                    g_q, g_k, pe_k, w_c1k, w_c2k, pe_v, w_c1v, w_c2v, w_a, w_b, w_o, w_pq, sk1, sk2, expert_u, expert_v)
```

```python
import functools
import math

import numpy as np
import jax
import jax.numpy as jnp
from jax import lax
from jax.experimental import pallas as pl
from jax.experimental.pallas import tpu as pltpu

F32 = jnp.float32
BF16 = jnp.bfloat16

N_HEADS = 16
HEAD_DIM = 128
N_KV = 4
GQA = N_HEADS // N_KV
KV_WIDTH = N_KV * HEAD_DIM
CHUNK = 128
A_GROUPS = 8
CMP_BLOCK = 64
SEL_BLOCK = 64
N_SEL = 16
WINDOW = 512
N_BUCKETS = 32
MAX_DISTANCE = 128
N_KEYS = 128
PEER_HEADS = 8
PEER_TOPK = 16
ATTN_SCALE = HEAD_DIM ** -0.5
NEG = -1e30
FORCE_BONUS = 1e4
EPS = 1e-6
LANE = 128
SUBLANE = 8
PROJ_TILE = 512
ATT_TILE = 256
VMEM_LIMIT = 56 * 1024 * 1024

T_U, T_V, T_Q, T_KC, T_VC, T_KS, T_VS, T_KW, T_VW, T_NSA, T_GA, T_GB, N_TILES = 0, 2, 4, 8, 9, 10, 11, 12, 13, 14, 15, 19, 23


def _bucket_thresholds():
    n = np.arange(0, 2 * MAX_DISTANCE)
    nf = np.maximum(n, 1).astype(np.float32)
    half = N_BUCKETS // 2
    large = half + (np.log(nf / half) / math.log(MAX_DISTANCE / half) * (N_BUCKETS - half)).astype(np.int32)
    b = np.where(n < half, n, np.minimum(large, N_BUCKETS - 1))
    assert np.all(np.diff(b) >= 0) and b[-1] == N_BUCKETS - 1
    return [int(np.argmax(b >= k)) for k in range(N_BUCKETS)]


BUCKET_THR = _bucket_thresholds()
FAR_DIST = BUCKET_THR[-1]
assert FAR_DIST <= MAX_DISTANCE


def _cparams(sem, vmem=VMEM_LIMIT):
    return pltpu.CompilerParams(dimension_semantics=sem, vmem_limit_bytes=vmem)


def _gelu(x):
    c = 2.0 * math.sqrt(2.0 / math.pi)
    u = (x * x) * (-0.044715 * c) - c
    return x * (1.0 / (1.0 + jnp.exp(x * u)))


def _sigmoid(x):
    return 1.0 / (1.0 + jnp.exp(-x))


def _dot_nt(a, b):
    return lax.dot_general(a, b, (((1,), (1,)), ((), ())), preferred_element_type=F32)


def _dot_tn(a, b):
    return lax.dot_general(a, b, (((0,), (0,)), ((), ())), preferred_element_type=F32)


def _bias_chain(dist, rbs):
    b = jnp.full(dist.shape, rbs[0], F32)
    for k in range(1, N_BUCKETS):
        b = jnp.where(dist >= BUCKET_THR[k], rbs[k], b)
    return b


def _mod_kernel(c_ref, w_ref, b_ref, o_ref):
    c = c_ref[...]
    a = (c * _sigmoid(c)).astype(BF16)
    o_ref[...] = jnp.dot(a, w_ref[...].astype(BF16), preferred_element_type=F32) + b_ref[...]


def _modulation(c_all, w_ada, b_ada):
    m, d = c_all.shape
    n = w_ada.shape[1]
    tn = 1024
    return pl.pallas_call(
        _mod_kernel,
        grid=(n // tn,),
        in_specs=[pl.BlockSpec((m, d), lambda j: (0, 0)),
                  pl.BlockSpec((d, tn), lambda j: (0, j)),
                  pl.BlockSpec((1, tn), lambda j: (0, j))],
        out_specs=pl.BlockSpec((m, tn), lambda j: (0, j)),
        out_shape=jax.ShapeDtypeStruct((m, n), F32),
        compiler_params=_cparams(("arbitrary",)),
        name="adaln_mod",
    )(c_all, w_ada, b_ada.reshape(1, n))


def _inproj_kernel(x_ref, sc_ref, sh_ref, gn_ref, w_ref, gain_ref, flag_ref, o_ref, h_scr):
    j = pl.program_id(1)

    @pl.when(j == 0)
    def _():
        x = x_ref[...]
        r = lax.rsqrt(jnp.mean(x * x, axis=-1, keepdims=True) + EPS)
        h = (x * r) * gn_ref[...] * (1.0 + sc_ref[...]) + sh_ref[...]
        h_scr[...] = h.reshape(h_scr.shape).astype(BF16)

    y = jnp.dot(h_scr[...], w_ref[...], preferred_element_type=F32)

    @pl.when(j < T_Q)
    def _():
        o_ref[...] = _gelu(y)

    @pl.when((j >= T_Q) & (j < T_NSA))
    def _():
        parts = []
        for hh in range(PROJ_TILE // HEAD_DIM):
            yh = y[:, hh * HEAD_DIM:(hh + 1) * HEAD_DIM]
            parts.append(yh * lax.rsqrt(jnp.mean(yh * yh, axis=-1, keepdims=True) + EPS))
        yn = jnp.concatenate(parts, axis=1) * gain_ref[...]
        o_ref[...] = jnp.where(flag_ref[...] > 0.5, yn, y)

    @pl.when(j >= T_NSA)
    def _():
        o_ref[...] = _sigmoid(y)


def _in_projection(x3, mod4, g_n1, w_in_p, gain, flag, bt, tt):
    nb, tb, d = x3.shape
    tpb = tb // tt
    tm = bt * tt
    n = nb * tb
    grid = (n // tm, N_TILES)
    return pl.pallas_call(
        _inproj_kernel,
        grid=grid,
        in_specs=[pl.BlockSpec((bt, tt, d), lambda i, j: (i // tpb, i % tpb, 0)),
                  pl.BlockSpec((bt, None, 1, d), lambda i, j: (i // tpb, 1, 0, 0)),
                  pl.BlockSpec((bt, None, 1, d), lambda i, j: (i // tpb, 0, 0, 0)),
                  pl.BlockSpec((1, 1, d), lambda i, j: (0, 0, 0)),
                  pl.BlockSpec((d, PROJ_TILE), lambda i, j: (0, j)),
                  pl.BlockSpec((None, 1, PROJ_TILE), lambda i, j: (j, 0, 0)),
                  pl.BlockSpec((None, 1, PROJ_TILE), lambda i, j: (j, 0, 0))],
        out_specs=pl.BlockSpec((None, tm, PROJ_TILE), lambda i, j: (j, i, 0)),
        out_shape=jax.ShapeDtypeStruct((N_TILES, n, PROJ_TILE), F32),
        scratch_shapes=[pltpu.VMEM((tm, d), BF16)],
        compiler_params=_cparams(("parallel", "arbitrary")),
        name="in_projection",
    )(x3, mod4, mod4, g_n1.reshape(1, 1, d), w_in_p, gain, flag)


def _mixa_kernel(u0_ref, u1_ref, v0_ref, v1_ref, lg_ref, lb_ref, wm_ref, bs_ref, ya_ref, vch_ref):
    v = jnp.concatenate([v0_ref[...], v1_ref[...]], axis=1)
    mu = jnp.mean(v, axis=-1, keepdims=True)
    var = jnp.mean(jnp.square(v - mu), axis=-1, keepdims=True)
    vln = ((v - mu) * lax.rsqrt(var + EPS)) * lg_ref[...] + lb_ref[...]
    vch_ref[...] = vln
    u = jnp.concatenate([u0_ref[...], u1_ref[...]], axis=1)
    vb = vln.astype(BF16)
    gd = vln.shape[1] // A_GROUPS
    for g in range(A_GROUPS):
        sl = slice(g * gd, (g + 1) * gd)
        s = jnp.dot(wm_ref[g], vb[:, sl], preferred_element_type=F32) + bs_ref[g]
        ya_ref[:, sl] = (u[:, sl] * s).astype(BF16)


def _mixer_a(proj, ln_g, ln_b, wm, bsb, vch_blocks, vch_map):
    n = proj.shape[1]
    aw = 2 * PROJ_TILE

    def tile(k):
        return pl.BlockSpec((None, CHUNK, PROJ_TILE), lambda i, k=k: (k, i, 0))

    return pl.pallas_call(
        _mixa_kernel,
        grid=(n // CHUNK,),
        in_specs=[tile(T_U), tile(T_U + 1), tile(T_V), tile(T_V + 1),
                  pl.BlockSpec((1, aw), lambda i: (0, 0)),
                  pl.BlockSpec((1, aw), lambda i: (0, 0)),
                  pl.BlockSpec((A_GROUPS, CHUNK, CHUNK), lambda i: (0, 0, 0)),
                  pl.BlockSpec((A_GROUPS, CHUNK, CHUNK), lambda i: (0, 0, 0))],
        out_specs=[pl.BlockSpec((CHUNK, aw), lambda i: (i, 0)),
                   pl.BlockSpec((CHUNK, aw), lambda i: (vch_map(i), 0))],
        out_shape=[jax.ShapeDtypeStruct((n, aw), BF16),
                   jax.ShapeDtypeStruct((vch_blocks * CHUNK, aw), F32)],
        compiler_params=_cparams(("arbitrary",)),
        name="mixer_a",
    )(proj, proj, proj, proj, ln_g.reshape(1, aw), ln_b.reshape(1, aw), wm, bsb)


def _compress_tail(hid, w2_ref, gain_ref, do_rms):
    out = jnp.dot(_gelu(hid).astype(BF16), w2_ref[...], preferred_element_type=F32)
    if do_rms:
        out = out * lax.rsqrt(jnp.mean(out * out, axis=-1, keepdims=True) + EPS) * gain_ref[...]
    return out


def _compress_prompt_kernel(x0_ref, x1_ref, x2_ref, x3_ref, pe_ref, w1_ref, w2_ref, gain_ref, o_ref, lhs_scr, *, nb, do_rms):
    for s_ in range(CMP_BLOCK):
        for g, x_ref in enumerate((x0_ref, x1_ref, x2_ref, x3_ref)):
            rows = x_ref[pl.ds(s_, nb, stride=CMP_BLOCK), :]
            lhs_scr[g * nb:(g + 1) * nb, s_ * HEAD_DIM:(s_ + 1) * HEAD_DIM] = (rows + pe_ref[s_:s_ + 1, :]).astype(BF16)
    hid = jnp.dot(lhs_scr[...], w1_ref[...], preferred_element_type=F32)
    out = _compress_tail(hid, w2_ref, gain_ref, do_rms)
    for g in range(N_KV):
        o_ref[:, g * HEAD_DIM:(g + 1) * HEAD_DIM] = out[g * nb:(g + 1) * nb]


def _compress_prompt(proj, tile, bsz, t, pe, w1b, w2b, gain, do_rms):
    nb = t // CMP_BLOCK
    hid = w1b.shape[1]
    return pl.pallas_call(
        functools.partial(_compress_prompt_kernel, nb=nb, do_rms=do_rms),
        grid=(bsz,),
        in_specs=[pl.BlockSpec((None, t, HEAD_DIM), lambda b, g=g: (tile, b, g)) for g in range(N_KV)] + [
                  pl.BlockSpec((CMP_BLOCK, HEAD_DIM), lambda b: (0, 0)),
                  pl.BlockSpec((CMP_BLOCK * HEAD_DIM, hid), lambda b: (0, 0)),
                  pl.BlockSpec((hid, HEAD_DIM), lambda b: (0, 0)),
                  pl.BlockSpec((1, HEAD_DIM), lambda b: (0, 0))],
        out_specs=pl.BlockSpec((nb, KV_WIDTH), lambda b: (b, 0)),
        out_shape=jax.ShapeDtypeStruct((bsz * nb, KV_WIDTH), F32),
        scratch_shapes=[pltpu.VMEM((N_KV * nb, CMP_BLOCK * HEAD_DIM), BF16)],
        compiler_params=_cparams(("arbitrary",)),
        name="compress_prompt",
    )(proj, proj, proj, proj, pe, w1b, w2b, gain.reshape(1, HEAD_DIM))


def _compress_pool_kernel(x_ref, pe8_ref, w1_ref, w2_ref, gain_ref, o_ref, lhs_scr, *, do_rms):
    tb = x_ref.shape[0]
    m = tb * SUBLANE
    hid = w2_ref.shape[0]
    for j in range(CMP_BLOCK // 2):
        xj = x_ref[:, SUBLANE * j:SUBLANE * (j + 1), :] + pe8_ref[j]
        lhs_scr[:, j * HEAD_DIM:(j + 1) * HEAD_DIM] = xj.reshape(m, HEAD_DIM).astype(BF16)
    acc = jnp.dot(lhs_scr[...], w1_ref[...], preferred_element_type=F32)
    hidv = acc[:, :hid] + pltpu.roll(acc[:, hid:], m - N_KV, 0)
    out = _compress_tail(hidv, w2_ref, gain_ref, do_rms)
    o_ref[...] = out.reshape(tb, SUBLANE, HEAD_DIM)


def _compress_pool(x3, pe8, w1x, w2b, gain, do_rms, tb=64):
    nblk = x3.shape[0]
    tb = min(tb, nblk)
    hid = w2b.shape[0]
    kdim = CMP_BLOCK // 2 * HEAD_DIM
    return pl.pallas_call(
        functools.partial(_compress_pool_kernel, do_rms=do_rms),
        grid=(nblk // tb,),
        in_specs=[pl.BlockSpec((tb, CMP_BLOCK * N_KV, HEAD_DIM), lambda i: (i, 0, 0)),
                  pl.BlockSpec((CMP_BLOCK // 2, SUBLANE, HEAD_DIM), lambda i: (0, 0, 0)),
                  pl.BlockSpec((kdim, 2 * hid), lambda i: (0, 0)),
                  pl.BlockSpec((hid, HEAD_DIM), lambda i: (0, 0)),
                  pl.BlockSpec((1, HEAD_DIM), lambda i: (0, 0))],
        out_specs=pl.BlockSpec((tb, SUBLANE, HEAD_DIM), lambda i: (i, 0, 0)),
        out_shape=jax.ShapeDtypeStruct((nblk, SUBLANE, HEAD_DIM), F32),
        scratch_shapes=[pltpu.VMEM((tb * SUBLANE, kdim), BF16)],
        compiler_params=_cparams(("arbitrary",)),
        name="compress_pool",
    )(x3, pe8, w1x, w2b, gain.reshape(1, HEAD_DIM))


def _bias_table_kernel(rb_ref, o_ref, c_ref, *, ts):
    g = pl.program_id(0)
    i = lax.broadcasted_iota(jnp.int32, (ts, ts), 0)
    j = lax.broadcasted_iota(jnp.int32, (ts, ts), 1)
    for d in range(2):
        dist = d * ts + i - j
        for r in range(GQA):
            rbs = [rb_ref[k, g * GQA + r] for k in range(N_BUCKETS)]
            o_ref[d, r * ts:(r + 1) * ts, :] = _bias_chain(dist, rbs) - rbs[-1]
    ic = lax.broadcasted_iota(jnp.int32, (ts, LANE), 0)
    nc = lax.broadcasted_iota(jnp.int32, (ts, LANE), 1) - LANE // 2
    dist_c = ic - (nc * CMP_BLOCK + CMP_BLOCK - 1)
    for r in range(GQA):
        rbs = [rb_ref[k, g * GQA + r] for k in range(N_BUCKETS)]
        c_ref[r * ts:(r + 1) * ts, :] = _bias_chain(dist_c, rbs)


def _bias_tables(rel_bias, ts):
    return pl.pallas_call(
        functools.partial(_bias_table_kernel, ts=ts),
        grid=(N_KV,),
        in_specs=[pl.BlockSpec(memory_space=pltpu.SMEM)],
        out_specs=[pl.BlockSpec((None, 2, GQA * ts, ts), lambda g: (g, 0, 0, 0)),
                   pl.BlockSpec((None, GQA * ts, LANE), lambda g: (g, 0, 0))],
        out_shape=[jax.ShapeDtypeStruct((N_KV, 2, GQA * ts, ts), F32),
                   jax.ShapeDtypeStruct((N_KV, GQA * ts, LANE), F32)],
        compiler_params=_cparams(("arbitrary",)),
        name="bias_tables",
    )(rel_bias)


def _rank_select(score, n_sel):
    t = score.shape[1]
    ngrp = -(-n_sel // SUBLANE)
    jrow = lax.broadcasted_iota(jnp.int32, (SUBLANE, t), 0)
    sel = []
    for gb in range(ngrp):
        blk = score[gb * SUBLANE:(gb + 1) * SUBLANE, :]
        rank = jnp.zeros((SUBLANE, t), F32)
        for i in range(n_sel):
            row = score[i:i + 1, :]
            if i < gb * SUBLANE:
                beats = row >= blk
            elif i >= (gb + 1) * SUBLANE:
                beats = row > blk
            else:
                beats = (row > blk) | ((jrow > i - gb * SUBLANE) & (row == blk))
            rank = rank + jnp.where(beats, 1.0, 0.0)
        keep = (rank < float(min(N_SEL, n_sel))) & (jrow + gb * SUBLANE < n_sel)
        sel.append(jnp.where(keep, 1.0, 0.0))
    sel.append(jnp.zeros((score.shape[0] - ngrp * SUBLANE, t), F32))
    return jnp.concatenate(sel, axis=0)


def _cmp_kernel(ct_ref, q0_ref, q1_ref, q2_ref, q3_ref, nsa_ref, kc_ref, vc_ref, oc_ref, sel_ref, *, tq, nb, n_sel):
    qt = pl.program_id(1)
    nsa = nsa_ref[...]
    row = lax.broadcasted_iota(jnp.int32, (tq, LANE), 0) + qt * tq
    col = lax.broadcasted_iota(jnp.int32, (tq, LANE), 1)
    dist = row - (col * CMP_BLOCK + CMP_BLOCK - 1)
    valid = (dist >= 0) & (col < nb)
    cur = row // SEL_BLOCK
    forced = (col == 0) | (col == cur) | (col == cur - 1)
    shift = (qt * (tq // CMP_BLOCK) + LANE // 2) % LANE
    pad = jnp.zeros((LANE - nb, HEAD_DIM), F32)
    for g in range(N_KV):
        q = (q0_ref, q1_ref, q2_ref, q3_ref)[g][...]
        qst = jnp.concatenate([q[:, r * HEAD_DIM:(r + 1) * HEAD_DIM] for r in range(GQA)], axis=0).astype(BF16)
        kg = jnp.concatenate([kc_ref[:, g * HEAD_DIM:(g + 1) * HEAD_DIM], pad], axis=0).astype(BF16)
        vg = jnp.concatenate([vc_ref[:, g * HEAD_DIM:(g + 1) * HEAD_DIM], pad], axis=0).astype(BF16)
        s = _dot_nt(qst, kg) * ATTN_SCALE
        ps = []
        imp = jnp.zeros((tq, LANE), F32)
        for r in range(GQA):
            h = g * GQA + r
            b = pltpu.roll(ct_ref[g, r * tq:(r + 1) * tq, :], shift, 1)
            sr = jnp.where(valid, s[r * tq:(r + 1) * tq] + b, NEG)
            m = jnp.max(sr, axis=-1, keepdims=True)
            p = jnp.where(valid, jnp.exp(sr - m), 0.0)
            den = jnp.sum(p, axis=-1, keepdims=True)
            p = p * (1.0 / jnp.maximum(den, 1e-30))
            imp = imp + p
            ps.append(p)
        o = jnp.dot(jnp.concatenate(ps, axis=0).astype(BF16), vg, preferred_element_type=F32)
        for r in range(GQA):
            h = g * GQA + r
            oc_ref[:, h * HEAD_DIM:(h + 1) * HEAD_DIM] = o[r * tq:(r + 1) * tq] * nsa[:, 3 * h:3 * h + 1]
        score = jnp.where(col <= cur, imp + jnp.where(forced, FORCE_BONUS, 0.0), NEG)
        score = jnp.where(col < n_sel, score, -3e38)
        sel_ref[g] = _rank_select(score.T, n_sel).T


def _cmp_select(proj, ctab, kcmp, vcmp, bsz, t, tq=ATT_TILE):
    nb = kcmp.shape[0] // bsz
    n_sel = -(-t // SEL_BLOCK)
    nq = t // tq
    n = proj.shape[1]
    assert nb <= LANE // 2 and SEL_BLOCK == CMP_BLOCK

    def tile(k):
        return pl.BlockSpec((None, tq, PROJ_TILE), lambda b, i, k=k: (k, b * nq + i, 0))

    return pl.pallas_call(
        functools.partial(_cmp_kernel, tq=tq, nb=nb, n_sel=n_sel),
        grid=(bsz, nq),
        in_specs=[pl.BlockSpec((N_KV, GQA * tq, LANE), lambda b, i: (0, 0, 0)),
                  tile(T_Q), tile(T_Q + 1), tile(T_Q + 2), tile(T_Q + 3), tile(T_NSA),
                  pl.BlockSpec((nb, KV_WIDTH), lambda b, i: (b, 0)),
                  pl.BlockSpec((nb, KV_WIDTH), lambda b, i: (b, 0))],
        out_specs=[pl.BlockSpec((tq, N_HEADS * HEAD_DIM), lambda b, i: (b * nq + i, 0)),
                   pl.BlockSpec((None, N_KV, tq, LANE), lambda b, i: (b, 0, i, 0))],
        out_shape=[jax.ShapeDtypeStruct((n, N_HEADS * HEAD_DIM), F32),
                   jax.ShapeDtypeStruct((bsz, N_KV, t, LANE), F32)],
        compiler_params=_cparams(("parallel", "arbitrary")),
        name="cmp_select",
    )(ctab, proj, proj, proj, proj, proj, kcmp, vcmp)


MASK_BIG = 2.0 ** 100
AUG = 2 * HEAD_DIM
ROW_BLOCK = 128


def _attn_kernel(rb_ref, q_ref, nsa_ref, k_ref, v_ref, tb_ref, prev_ref, *rest, mode, tq, branch):
    if mode == "sel":
        sel_ref, o_ref, kb, vb, qa, s_scr, p_scr, m_s, a_s, acc_s = rest
    else:
        o_ref, kb, vb, qa, s_scr, p_scr, m_s, a_s, acc_s = rest
    g = pl.program_id(1)
    qt = pl.program_id(2)
    tk = tq
    rows4 = GQA * tq
    t_all = kb.shape[0]

    @pl.when(qt == 0)
    def _():
        krow = lax.broadcasted_iota(jnp.int32, (t_all, LANE), 0)
        lane = lax.broadcasted_iota(jnp.int32, (t_all, LANE), 1)
        onehot = ((lane < SEL_BLOCK) & (krow // SEL_BLOCK == lane)) | (lane == SEL_BLOCK) | (lane == SEL_BLOCK + 1)
        kb[:, 0:HEAD_DIM] = k_ref[...].astype(BF16)
        kb[:, HEAD_DIM:AUG] = onehot.astype(BF16)
        vb[:, 0:HEAD_DIM] = v_ref[...].astype(BF16)
        vb[:, HEAD_DIM:AUG] = (lane == 0).astype(BF16)

    q = q_ref[...]
    lane_q = lax.broadcasted_iota(jnp.int32, (tq, LANE), 1)
    if mode == "sel":
        selm = jnp.where(lane_q < SEL_BLOCK, (sel_ref[...] - 1.0) * MASK_BIG, 0.0)
    else:
        selm = jnp.zeros((tq, LANE), F32)
    for r in range(GQA):
        b_far = jnp.full((tq, LANE), rb_ref[N_BUCKETS - 1, g * GQA + r], F32)
        b_hi = b_far.astype(BF16).astype(F32)
        ext = jnp.where(lane_q == SEL_BLOCK, b_hi, jnp.where(lane_q == SEL_BLOCK + 1, b_far - b_hi, selm))
        qa[r * tq:(r + 1) * tq, 0:HEAD_DIM] = (q[:, r * HEAD_DIM:(r + 1) * HEAD_DIM] * ATTN_SCALE).astype(BF16)
        qa[r * tq:(r + 1) * tq, HEAD_DIM:AUG] = ext.astype(BF16)
    m_s[...] = jnp.full(m_s.shape, NEG, F32)
    acc_s[...] = jnp.zeros(acc_s.shape, F32)
    nrb = rows4 // ROW_BLOCK
    rowpos = lax.broadcasted_iota(jnp.int32, (ROW_BLOCK, tk), 0)
    colpos = lax.broadcasted_iota(jnp.int32, (ROW_BLOCK, tk), 1)

    def chunk(kt, table, mask_kind):
        k0 = pl.multiple_of(kt * tk, tk)
        s_scr[...] = _dot_nt(qa[...], kb[pl.ds(k0, tk), :])
        for rb in range(nrb):
            rs = slice(rb * ROW_BLOCK, (rb + 1) * ROW_BLOCK)
            s = s_scr[rs, :]
            if table is not None:
                s = s + tb_ref[table, rs, :]
            if mask_kind is not None:
                rp = rowpos + (rb * ROW_BLOCK) % tq
                keep = (rp >= colpos) if mask_kind == "causal" else (colpos > rp)
                s = jnp.where(keep, s, NEG)
            m_old = m_s[rs, :]
            m_new = jnp.maximum(m_old, jnp.max(s, axis=-1, keepdims=True))
            p_scr[rs, :] = jnp.exp(s - jnp.concatenate([m_new] * (tk // LANE), axis=1)).astype(BF16)
            a_s[rs, :] = jnp.exp(m_old - m_new)
            m_s[rs, :] = m_new
        pv = jnp.dot(p_scr[...], vb[pl.ds(k0, tk), :], preferred_element_type=F32)
        a = a_s[...]
        acc_s[...] = jnp.concatenate([a] * (AUG // LANE), axis=1) * acc_s[...] + pv

    chunk(qt, 0, "causal")

    @pl.when(qt >= 1)
    def _():
        chunk(qt - 1, 1, None)

    if mode == "sel":
        def far_body(kt, carry):
            chunk(kt, None, None)
            return carry
        lax.fori_loop(0, jnp.maximum(qt - 1, 0), far_body, 0)
    else:
        @pl.when(qt >= 2)
        def _():
            chunk(qt - 2, None, "window")

    acc = acc_s[...]
    o = acc[:, 0:HEAD_DIM] * (1.0 / jnp.maximum(acc[:, HEAD_DIM:HEAD_DIM + 1], 1e-30))
    nsa = nsa_ref[...]
    lane = lax.broadcasted_iota(jnp.int32, nsa.shape, 1)
    for r in range(GQA):
        gidx = (g * GQA + r) * 3 + branch
        gate = jnp.sum(jnp.where(lane == gidx, nsa, 0.0), axis=-1, keepdims=True)
        hs = slice(r * HEAD_DIM, (r + 1) * HEAD_DIM)
        o_ref[:, hs] = prev_ref[:, hs] + o[r * tq:(r + 1) * tq] * gate


def _prompt_attention(proj, rel_bias, tables, prev, bsz, t, mode, sel=None, tq=ATT_TILE):
    nq = t // tq
    n = proj.shape[1]
    assert t // SEL_BLOCK <= SEL_BLOCK and tq % ROW_BLOCK == 0
    if mode == "sel":
        tk_, tv_, branch = T_KS, T_VS, 1
    else:
        tk_, tv_, branch = T_KW, T_VW, 2
        assert WINDOW == 2 * tq
    in_specs = [pl.BlockSpec(memory_space=pltpu.SMEM),
                pl.BlockSpec((None, tq, PROJ_TILE), lambda b, g, i: (T_Q + g, b * nq + i, 0)),
                pl.BlockSpec((None, tq, PROJ_TILE), lambda b, g, i: (T_NSA, b * nq + i, 0)),
                pl.BlockSpec((None, t, HEAD_DIM), lambda b, g, i: (tk_, b, g)),
                pl.BlockSpec((None, t, HEAD_DIM), lambda b, g, i: (tv_, b, g)),
                pl.BlockSpec((None, 2, GQA * tq, tq), lambda b, g, i: (g, 0, 0, 0)),
                pl.BlockSpec((tq, GQA * HEAD_DIM), lambda b, g, i: (b * nq + i, g))]
    args = [rel_bias, proj, proj, proj, proj, tables, prev]
    if mode == "sel":
        in_specs += [pl.BlockSpec((None, None, tq, LANE), lambda b, g, i: (b, g, i, 0))]
        args += [sel]
    return pl.pallas_call(
        functools.partial(_attn_kernel, mode=mode, tq=tq, branch=branch),
        grid=(bsz, N_KV, nq),
        in_specs=in_specs,
        out_specs=pl.BlockSpec((tq, GQA * HEAD_DIM), lambda b, g, i: (b * nq + i, g)),
        out_shape=jax.ShapeDtypeStruct((n, N_HEADS * HEAD_DIM), F32),
        scratch_shapes=[pltpu.VMEM((t, AUG), BF16), pltpu.VMEM((t, AUG), BF16),
                        pltpu.VMEM((GQA * tq, AUG), BF16),
                        pltpu.VMEM((GQA * tq, tq), F32), pltpu.VMEM((GQA * tq, tq), BF16),
                        pltpu.VMEM((GQA * tq, LANE), F32), pltpu.VMEM((GQA * tq, LANE), F32),
                        pltpu.VMEM((GQA * tq, AUG), F32)],
        compiler_params=_cparams(("parallel", "parallel", "arbitrary")),
        name="attn_" + mode,
    )(*args)


def _masked_softmax(s, mask):
    s = jnp.where(mask, s, NEG)
    m = jnp.max(s, axis=-1, keepdims=True)
    p = jnp.where(mask, jnp.exp(s - m), 0.0)
    den = jnp.sum(p, axis=-1, keepdims=True)
    return p * (1.0 / jnp.maximum(den, 1e-30))


def _near_far_bias(rb_ref, g, dist_near, tdec, width, near):
    rows = []
    for r in range(GQA):
        h = g * GQA + r
        rbs = [rb_ref[k, h] for k in range(N_BUCKETS)]
        nb_ = _bias_chain(dist_near, rbs)
        rows.append(jnp.concatenate([jnp.full((tdec, width - near), rbs[-1], F32), nb_], axis=1))
    return jnp.concatenate(rows, axis=0)


def _sattn_kernel(pt_ref, rb_ref, q0_ref, q1_ref, q2_ref, q3_ref, ksn_ref, vsn_ref, kwn_ref, vwn_ref, nsa_ref,
                  kc_ref, vc_ref, *rest, npg, page, tdec, wlen):
    kpages = rest[:npg]
    vpages = rest[npg:2 * npg]
    skw_ref, svw_ref, e_ref, yb_ref, kwo_ref, vwo_ref, kbuf, vbuf, wkb, wvb = rest[2 * npg:]
    past = npg * page
    lk = past + LANE
    nb = kc_ref.shape[0]
    n_sel = -(-(past + tdec) // SEL_BLOCK)
    rows_w = wlen * N_KV
    near = 2 * LANE
    rq = GQA * tdec
    nsa = nsa_ref[...]

    kwo_ref[0:rows_w - tdec * N_KV, :] = skw_ref[tdec * N_KV:rows_w, :]
    vwo_ref[0:rows_w - tdec * N_KV, :] = svw_ref[tdec * N_KV:rows_w, :]
    for g in range(N_KV):
        kwo_ref[pl.ds(rows_w - tdec * N_KV + g, tdec, stride=N_KV), :] = kwn_ref[:, g * HEAD_DIM:(g + 1) * HEAD_DIM]
        vwo_ref[pl.ds(rows_w - tdec * N_KV + g, tdec, stride=N_KV), :] = vwn_ref[:, g * HEAD_DIM:(g + 1) * HEAD_DIM]

    trow1 = lax.broadcasted_iota(jnp.int32, (tdec, LANE), 0) + past
    col1 = lax.broadcasted_iota(jnp.int32, (tdec, LANE), 1)
    dist_c = trow1 - (col1 * CMP_BLOCK + CMP_BLOCK - 1)
    valid_c = (dist_c >= 0) & (col1 < nb)
    cur = trow1 // SEL_BLOCK
    forced = (col1 == 0) | (col1 == cur) | (col1 == cur - 1)

    trow_s = (lax.broadcasted_iota(jnp.int32, (rq, lk), 0) & (tdec - 1)) + past
    pos_s = lax.broadcasted_iota(jnp.int32, (rq, lk), 1)
    causal_s = pos_s <= trow_s
    dist_sn = (lax.broadcasted_iota(jnp.int32, (tdec, near), 0) + past) - (lax.broadcasted_iota(jnp.int32, (tdec, near), 1) + lk - near)

    wl = wlen + LANE
    qidx_w = (lax.broadcasted_iota(jnp.int32, (rq, wl), 0) & (tdec - 1)) + wlen
    kidx_w = lax.broadcasted_iota(jnp.int32, (rq, wl), 1)
    dist_w = qidx_w - kidx_w
    mask_w = (dist_w >= 0) & (dist_w < WINDOW)
    dist_wn = (lax.broadcasted_iota(jnp.int32, (tdec, near), 0) + wlen) - (lax.broadcasted_iota(jnp.int32, (tdec, near), 1) + wl - near)

    zpad = jnp.zeros((LANE - tdec, HEAD_DIM), F32)
    cpad = jnp.zeros((LANE - nb, HEAD_DIM), F32)
    for g in range(N_KV):
        gs = slice(g * HEAD_DIM, (g + 1) * HEAD_DIM)
        q = (q0_ref, q1_ref, q2_ref, q3_ref)[g][...]
        qst = jnp.concatenate([q[:, r * HEAD_DIM:(r + 1) * HEAD_DIM] for r in range(GQA)], axis=0).astype(BF16)

        kg = jnp.concatenate([kc_ref[:, g, :], cpad], axis=0).astype(BF16)
        vg = jnp.concatenate([vc_ref[:, g, :], cpad], axis=0).astype(BF16)
        s = _dot_nt(qst, kg) * ATTN_SCALE
        ps = []
        imp = jnp.zeros((tdec, LANE), F32)
        for r in range(GQA):
            h = g * GQA + r
            b = _bias_chain(dist_c, [rb_ref[k, h] for k in range(N_BUCKETS)])
            p = _masked_softmax(s[r * tdec:(r + 1) * tdec] + b, valid_c)
            imp = imp + p
            ps.append(p)
        o_c = jnp.dot(jnp.concatenate(ps, axis=0).astype(BF16), vg, preferred_element_type=F32)

        score = jnp.where(col1 <= cur, imp + jnp.where(forced, FORCE_BONUS, 0.0), NEG)
        score = jnp.where(col1 < n_sel, score, -3e38)
        rank = jnp.zeros((tdec, LANE), F32)
        for i in range(n_sel):
            ci = score[:, i:i + 1]
            rank = rank + ((ci > score) | ((ci == score) & (col1 > i))).astype(F32)
        sel = ((rank < float(min(N_SEL, n_sel))) & (col1 < n_sel)).astype(F32)

        for p_ in range(npg):
            kbuf[p_ * page:(p_ + 1) * page, :] = kpages[p_][pl.ds(g, page, stride=N_KV), :].astype(BF16)
            vbuf[p_ * page:(p_ + 1) * page, :] = vpages[p_][pl.ds(g, page, stride=N_KV), :].astype(BF16)
        kbuf[past:lk, :] = jnp.concatenate([ksn_ref[:, gs], zpad], axis=0).astype(BF16)
        vbuf[past:lk, :] = jnp.concatenate([vsn_ref[:, gs], zpad], axis=0).astype(BF16)
        s = _dot_nt(qst, kbuf[...]) * ATTN_SCALE + _near_far_bias(rb_ref, g, dist_sn, tdec, lk, near)
        sel4 = jnp.concatenate([sel] * GQA, axis=0).astype(BF16)
        mask = (jnp.dot(sel4, e_ref[...], preferred_element_type=F32) > 0.5) & causal_s
        o_s = jnp.dot(_masked_softmax(s, mask).astype(BF16), vbuf[...], preferred_element_type=F32)

        wkb[0:wlen, :] = skw_ref[pl.ds(g, wlen, stride=N_KV), :].astype(BF16)
        wvb[0:wlen, :] = svw_ref[pl.ds(g, wlen, stride=N_KV), :].astype(BF16)
        wkb[wlen:wl, :] = jnp.concatenate([kwn_ref[:, gs], zpad], axis=0).astype(BF16)
        wvb[wlen:wl, :] = jnp.concatenate([vwn_ref[:, gs], zpad], axis=0).astype(BF16)
        s = _dot_nt(qst, wkb[...]) * ATTN_SCALE + _near_far_bias(rb_ref, g, dist_wn, tdec, wl, near)
        o_w = jnp.dot(_masked_softmax(s, mask_w).astype(BF16), wvb[...], preferred_element_type=F32)

        for r in range(GQA):
            h = g * GQA + r
            rs = slice(r * tdec, (r + 1) * tdec)
            yb_ref[:, h * HEAD_DIM:(h + 1) * HEAD_DIM] = (nsa[:, 3 * h:3 * h + 1] * o_c[rs]
                                                         + nsa[:, 3 * h + 1:3 * h + 2] * o_s[rs]
                                                         + nsa[:, 3 * h + 2:3 * h + 3] * o_w[rs])


def _sample_attention(proj, rel_bias, page_table, kcmp_g, vcmp_g, ck_sel, cv_sel, skw, svw, emat, bsz, tdec):
    npg = page_table.shape[1]
    page = ck_sel.shape[1] // N_KV
    wlen = skw.shape[1] // N_KV
    nb = kcmp_g.shape[1]
    past = npg * page
    lk = past + LANE
    assert tdec == SUBLANE and FAR_DIST <= LANE and wlen == WINDOW

    def tile(k):
        return pl.BlockSpec((None, tdec, PROJ_TILE), lambda b, pt, k=k: (k, b, 0))

    def pagespec(p_):
        return pl.BlockSpec((None, page * N_KV, HEAD_DIM), lambda b, pt, p_=p_: (pt[b, p_], 0, 0))

    in_specs = ([pl.BlockSpec(memory_space=pltpu.SMEM)]
                + [tile(T_Q + g) for g in range(N_KV)]
                + [tile(T_KS), tile(T_VS), tile(T_KW), tile(T_VW), tile(T_NSA)]
                + [pl.BlockSpec((None, nb, SUBLANE, HEAD_DIM), lambda b, pt: (b, 0, 0, 0))] * 2
                + [pagespec(p_) for p_ in range(npg)] * 2
                + [pl.BlockSpec((None, wlen * N_KV, HEAD_DIM), lambda b, pt: (b, 0, 0))] * 2
                + [pl.BlockSpec((LANE, lk), lambda b, pt: (0, 0))])
    grid_spec = pltpu.PrefetchScalarGridSpec(
        num_scalar_prefetch=1,
        grid=(bsz,),
        in_specs=in_specs,
        out_specs=[pl.BlockSpec((tdec, N_HEADS * HEAD_DIM), lambda b, pt: (b, 0)),
                   pl.BlockSpec((None, wlen * N_KV, HEAD_DIM), lambda b, pt: (b, 0, 0)),
                   pl.BlockSpec((None, wlen * N_KV, HEAD_DIM), lambda b, pt: (b, 0, 0))],
        scratch_shapes=[pltpu.VMEM((lk, HEAD_DIM), BF16), pltpu.VMEM((lk, HEAD_DIM), BF16),
                        pltpu.VMEM((wlen + LANE, HEAD_DIM), BF16), pltpu.VMEM((wlen + LANE, HEAD_DIM), BF16)])
    return pl.pallas_call(
        functools.partial(_sattn_kernel, npg=npg, page=page, tdec=tdec, wlen=wlen),
        grid_spec=grid_spec,
        out_shape=[jax.ShapeDtypeStruct((bsz * tdec, N_HEADS * HEAD_DIM), F32),
                   jax.ShapeDtypeStruct((bsz, wlen * N_KV, HEAD_DIM), F32),
                   jax.ShapeDtypeStruct((bsz, wlen * N_KV, HEAD_DIM), F32)],
        compiler_params=_cparams(("arbitrary",)),
        name="sample_attention",
    )(page_table, rel_bias, *([proj] * 9), kcmp_g, vcmp_g, *([ck_sel] * npg), *([cv_sel] * npg), skw, svw, emat)


def _merge_kernel(ya_ref, yb_ref, wa_ref, wb_ref, ga_ref, gb_ref, t_ref, yb_scr):
    @pl.when(pl.program_id(1) == 0)
    def _():
        yb_scr[...] = yb_ref[...].astype(BF16)

    a = jnp.dot(ya_ref[...], wa_ref[...], preferred_element_type=F32)
    b = jnp.dot(yb_scr[...], wb_ref[...], preferred_element_type=F32)
    t_ref[...] = (ga_ref[...] * a + gb_ref[...] * b).astype(BF16)


def _merge(proj, ya, yb, wa_b, wb_b, tm=1024):
    n, aw = ya.shape
    d = wb_b.shape[0]
    tm = min(tm, n)
    nj = d // PROJ_TILE
    return pl.pallas_call(
        _merge_kernel,
        grid=(n // tm, nj),
        in_specs=[pl.BlockSpec((tm, aw), lambda i, j: (i, 0)),
                  pl.BlockSpec((tm, d), lambda i, j: (i, 0)),
                  pl.BlockSpec((aw, PROJ_TILE), lambda i, j: (0, j)),
                  pl.BlockSpec((d, PROJ_TILE), lambda i, j: (0, j)),
                  pl.BlockSpec((None, tm, PROJ_TILE), lambda i, j: (T_GA + j, i, 0)),
                  pl.BlockSpec((None, tm, PROJ_TILE), lambda i, j: (T_GB + j, i, 0))],
        out_specs=pl.BlockSpec((tm, PROJ_TILE), lambda i, j: (i, j)),
        out_shape=jax.ShapeDtypeStruct((n, d), BF16),
        scratch_shapes=[pltpu.VMEM((tm, d), BF16)],
        compiler_params=_cparams(("parallel", "arbitrary")),
        name="merge",
    )(ya, yb, wa_b, wb_b, proj, proj)


def _outproj_kernel(t_ref, x_ref, gt_ref, sc_ref, sh_ref, gn_ref, wo_ref, *rest):
    x1_ref, h2_ref = rest[-2:]
    y = jnp.dot(t_ref[...], wo_ref[...], preferred_element_type=F32)
    x1 = x_ref[...] + gt_ref[...] * y.reshape(x_ref.shape)
    x1_ref[...] = x1
    r = lax.rsqrt(jnp.mean(x1 * x1, axis=-1, keepdims=True) + EPS)
    h2 = (x1 * r) * gn_ref[...] * (1.0 + sc_ref[...]) + sh_ref[...]
    h2_ref[...] = h2.reshape(h2_ref.shape).astype(BF16)


def _out_projection(tmix, x3, mod4, g_n2, wo_b, bt, tt, n_total, row0, h2_buf=None):
    nb, tb, d = x3.shape
    tpb = tb // tt
    tm = bt * tt
    n = nb * tb
    assert row0 % tm == 0
    off = row0 // tm
    extra_specs = [] if h2_buf is None else [pl.BlockSpec(memory_space=pl.ANY)]
    extra_args = [] if h2_buf is None else [h2_buf]
    aliases = {} if h2_buf is None else {7: 1}

    def modspec(k):
        return pl.BlockSpec((bt, None, 1, d), lambda i, k=k: (i // tpb, k, 0, 0))

    return pl.pallas_call(
        _outproj_kernel,
        grid=(n // tm,),
        in_specs=[pl.BlockSpec((tm, d), lambda i: (i, 0)),
                  pl.BlockSpec((bt, tt, d), lambda i: (i // tpb, i % tpb, 0)),
                  modspec(2), modspec(4), modspec(3),
                  pl.BlockSpec((1, 1, d), lambda i: (0, 0, 0)),
                  pl.BlockSpec((d, d), lambda i: (0, 0))] + extra_specs,
        out_specs=[pl.BlockSpec((bt, tt, d), lambda i: (i // tpb, i % tpb, 0)),
                   pl.BlockSpec((tm, d), lambda i: (off + i, 0))],
        out_shape=[jax.ShapeDtypeStruct((nb, tb, d), F32), jax.ShapeDtypeStruct((n_total, d), BF16)],
        input_output_aliases=aliases,
        compiler_params=_cparams(("arbitrary",)),
        name="out_projection",
    )(tmix, x3, mod4, mod4, mod4, g_n2.reshape(1, 1, d), wo_b, *extra_args)


def _peer_scores_kernel(h_ref, wpq_ref, sk1_ref, sk2_ref, s1_ref, s2_ref):
    pq = jnp.dot(h_ref[...], wpq_ref[...], preferred_element_type=F32)
    kd = sk1_ref.shape[1]
    for hd in range(PEER_HEADS):
        q1 = pq[:, hd * 2 * kd:hd * 2 * kd + kd].astype(BF16)
        q2 = pq[:, hd * 2 * kd + kd:(hd + 1) * 2 * kd].astype(BF16)
        s1_ref[hd] = _dot_nt(sk1_ref[...], q1)
        s2_ref[hd] = _dot_nt(sk2_ref[...], q2)


def _peer_scores(h2, wpq_b, sk1_b, sk2_b, tm=512):
    n, d = h2.shape
    dq = wpq_b.shape[1]
    nk, kd = sk1_b.shape
    return pl.pallas_call(
        _peer_scores_kernel,
        grid=(n // tm,),
        in_specs=[pl.BlockSpec((tm, d), lambda i: (i, 0)),
                  pl.BlockSpec((d, dq), lambda i: (0, 0)),
                  pl.BlockSpec((nk, kd), lambda i: (0, 0)),
                  pl.BlockSpec((nk, kd), lambda i: (0, 0))],
        out_specs=[pl.BlockSpec((PEER_HEADS, nk, tm), lambda i: (0, 0, i))] * 2,
        out_shape=[jax.ShapeDtypeStruct((PEER_HEADS, nk, n), F32)] * 2,
        compiler_params=_cparams(("arbitrary",)),
        name="peer_scores",
    )(h2, wpq_b, sk1_b, sk2_b)


def _staircase():
    return [(a, b) for a in range(PEER_TOPK) for b in range(PEER_TOPK) if (a + 1) * (b + 1) <= PEER_TOPK]


def _extract_top(s, rows_f, exact):
    vals = []
    rank = jnp.full(s.shape, float(PEER_TOPK), F32)
    for a in range(PEER_TOPK):
        m = jnp.max(s, axis=0, keepdims=True)
        hit = s == m
        if exact:
            hit = rows_f == jnp.min(jnp.where(hit, rows_f, 1e9), axis=0, keepdims=True)
        rank = jnp.where(hit, float(a), rank)
        s = jnp.where(hit, -jnp.inf, s)
        vals.append(m)
    return vals, rank


def _peer_topk_kernel(s1_ref, s2_ref, cnt_ref, e1_ref, rk_ref, e2_ref):
    nk, tn = s1_ref.shape[1], s1_ref.shape[2]
    rows_f = lax.broadcasted_iota(jnp.int32, (nk, tn), 0).astype(F32)
    pairs = _staircase()
    npad = -(-len(pairs) // SUBLANE) * SUBLANE
    prow = lax.broadcasted_iota(jnp.int32, (npad, tn), 0)
    flat_f = jnp.full((npad, tn), 1e9, F32)
    arow_f = jnp.full((npad, tn), -1.0, F32)
    for i, (a, b) in enumerate(pairs):
        flat_f = jnp.where(prow == i, float(a * PEER_TOPK + b), flat_f)
        arow_f = jnp.where(prow == i, float(a), arow_f)
    k_f = float(PEER_TOPK)

    def one_head(hd, exact):
        s1 = s1_ref[hd]
        s2 = s2_ref[hd]
        v1, rank1 = _extract_top(s1, rows_f, exact)
        v2, rank2 = _extract_top(s2, rows_f, exact)
        cand = jnp.full((npad, tn), -jnp.inf, F32)
        for i, (a, b) in enumerate(pairs):
            cand = jnp.where(prow == i, v1[a] + v2[b], cand)
        m0 = v1[0] + v2[0]
        c = cand
        selected = jnp.zeros((npad, tn), F32)
        for _ in range(PEER_TOPK):
            m = jnp.max(c, axis=0, keepdims=True)
            hit = c == m
            if exact:
                hit = flat_f == jnp.min(jnp.where(hit, flat_f, 2e9), axis=0, keepdims=True)
            selected = jnp.where(hit, 1.0, selected)
            c = jnp.where(hit, -jnp.inf, c)
        z = jnp.sum(jnp.where(selected > 0.5, jnp.exp(cand - m0), 0.0), axis=0, keepdims=True)
        cnt1 = jnp.zeros((nk, tn), F32)
        for a in range(PEER_TOPK):
            cnt_a = jnp.sum(jnp.where(arow_f == float(a), selected, 0.0), axis=0, keepdims=True)
            cnt1 = jnp.where(rank1 == float(a), cnt_a, cnt1)
        cnt_ref[hd] = cnt1
        e1_ref[hd] = jnp.exp(s1 - v1[0]) * (1.0 / z)
        rk_ref[hd] = rank2
        e2_ref[hd] = jnp.exp(s2 - v2[0])
        if exact:
            return None
        n1 = jnp.sum(jnp.where(rank1 < k_f, 1.0, 0.0), axis=0, keepdims=True)
        n2 = jnp.sum(jnp.where(rank2 < k_f, 1.0, 0.0), axis=0, keepdims=True)
        n3 = jnp.sum(selected, axis=0, keepdims=True)
        return jnp.where((n1 != k_f) | (n2 != k_f) | (n3 != k_f), 1.0, 0.0)

    def body(hd, carry):
        tie = one_head(hd, False)

        @pl.when(jnp.max(tie) > 0.0)
        def _():
            one_head(hd, True)

        return carry

    lax.fori_loop(0, PEER_HEADS, body, 0)


def _peer_topk(s1t, s2t, tn=256):
    nh, nk, n = s1t.shape
    spec = pl.BlockSpec((nh, nk, tn), lambda i: (0, 0, i))
    return pl.pallas_call(
        _peer_topk_kernel,
        grid=(n // tn,),
        in_specs=[spec, spec],
        out_specs=[spec] * 4,
        out_shape=[jax.ShapeDtypeStruct((nh, nk, n), F32)] * 4,
        compiler_params=_cparams(("arbitrary",)),
        name="peer_topk",
    )(s1t, s2t)


PEER_SUB = 256


def _peer_dense_kernel(h_ref, eu_ref, ev_ref, cnt_ref, e1_ref, rk_ref, e2_ref, o_ref, at_scr, wa_scr, *, te):
    e = pl.program_id(1)
    nk = rk_ref.shape[1]
    tm, d = h_ref.shape

    @pl.when(e == 0)
    def _():
        o_ref[...] = jnp.zeros(o_ref.shape, F32)

    n_i1 = PEER_SUB // nk
    nsub = te // PEER_SUB
    assert te // nk == SUBLANE
    i1_base = pl.multiple_of(e * SUBLANE, SUBLANE)
    tok_piece = 2 * LANE
    n_tok = tm // tok_piece
    col_piece = 2 * LANE
    n_col = d // col_piece
    tiles = [(il, tb) for il in range(n_i1) for tb in range(tm // LANE)]

    def pre_activation(sb, k):
        ts_ = slice(k * tok_piece, (k + 1) * tok_piece)
        at_scr[sb, :, ts_] = _dot_nt(eu_ref[sb * PEER_SUB:(sb + 1) * PEER_SUB, :], h_ref[ts_, :])

    def down_projection(sb, k):
        cs_ = slice(k * col_piece, (k + 1) * col_piece)
        o_ref[:, cs_] += _dot_tn(wa_scr[sb], ev_ref[sb * PEER_SUB:(sb + 1) * PEER_SUB, cs_])

    def gate_tile(sb, il, tb):
        j1 = sb * n_i1 + il
        ks = slice(il * nk, (il + 1) * nk)
        cs = slice(tb * LANE, (tb + 1) * LANE)
        w = jnp.zeros((nk, LANE), F32)
        for hd in range(PEER_HEADS):
            c = cnt_ref[hd, pl.ds(i1_base, SUBLANE), cs][j1:j1 + 1, :]
            g1 = e1_ref[hd, pl.ds(i1_base, SUBLANE), cs][j1:j1 + 1, :]
            w = w + jnp.where(rk_ref[hd, :, cs] < c, e2_ref[hd, :, cs] * g1, 0.0)
        wa_scr[sb, ks, cs] = (w * _gelu(at_scr[sb, ks, cs])).astype(BF16)

    for k in range(n_tok):
        pre_activation(0, k)
    nslot = max(len(tiles), n_col)
    for sb in range(nsub):
        for k in range(nslot):
            if sb >= 1 and k < n_col:
                down_projection(sb - 1, k)
            if sb + 1 < nsub and k % (nslot // n_tok) == 0:
                pre_activation(sb + 1, k // (nslot // n_tok))
            if k < len(tiles):
                gate_tile(sb, *tiles[k])
    for k in range(n_col):
        down_projection(nsub - 1, k)


def _peer_dense(h2, eu_b, ev_b, cnt1, e1, rk2, e2, tm=512, te=1024):
    n, d = h2.shape
    ne = eu_b.shape[0]
    nh, nk, _ = cnt1.shape
    res = pl.BlockSpec((nh, nk, tm), lambda i, e: (0, 0, i))
    return pl.pallas_call(
        functools.partial(_peer_dense_kernel, te=te),
        grid=(n // tm, ne // te),
        in_specs=[pl.BlockSpec((tm, d), lambda i, e: (i, 0)),
                  pl.BlockSpec((te, d), lambda i, e: (e, 0)),
                  pl.BlockSpec((te, d), lambda i, e: (e, 0)),
                  res, res, res, res],
        out_specs=pl.BlockSpec((tm, d), lambda i, e: (i, 0)),
        out_shape=jax.ShapeDtypeStruct((n, d), F32),
        scratch_shapes=[pltpu.VMEM((te // PEER_SUB, PEER_SUB, tm), F32), pltpu.VMEM((te // PEER_SUB, PEER_SUB, tm), BF16)],
        compiler_params=_cparams(("parallel", "arbitrary")),
        name="peer_dense",
    )(h2, eu_b, ev_b, cnt1, e1, rk2, e2)


def _final_kernel(x1_ref, p_ref, gt_ref, o_ref):
    o_ref[...] = x1_ref[...] + gt_ref[...] * p_ref[...].reshape(x1_ref.shape)


def _final_residual(x1, peer, row0, mod4, bt, tt):
    nb, tb, d = x1.shape
    tpb = tb // tt
    tm = bt * tt
    n = nb * tb
    off = row0 // tm
    return pl.pallas_call(
        _final_kernel,
        grid=(n // tm,),
        in_specs=[pl.BlockSpec((bt, tt, d), lambda i: (i // tpb, i % tpb, 0)),
                  pl.BlockSpec((tm, d), lambda i: (off + i, 0)),
                  pl.BlockSpec((bt, None, 1, d), lambda i: (i // tpb, 5, 0, 0))],
        out_specs=pl.BlockSpec((bt, tt, d), lambda i: (i // tpb, i % tpb, 0)),
        out_shape=jax.ShapeDtypeStruct((nb, tb, d), F32),
        compiler_params=_cparams(("arbitrary",)),
        name="final_residual",
    )(x1, peer, mod4)


def _block_expand(n_cols, width=LANE):
    j = np.arange(width)[:, None]
    s = np.arange(n_cols)[None, :]
    return (s // SEL_BLOCK == j).astype(np.float32)


def _forward(x_prompt, x_sample, cache_k_cmp, cache_v_cmp, cache_k_sel, cache_v_sel, state_k_win, state_v_win,
             page_table, c_prompt, c_sample, rel_bias, w_ada, b_ada, g_n1, g_n2, w_in, ln_v_g, ln_v_b, w_s, b_s,
             g_q, g_k, pe_k, w_c1k, w_c2k, pe_v, w_c1v, w_c2v, w_a, w_b, w_o, w_pq, sk1, sk2, expert_u, expert_v):
    assert w_ada.shape[0] == 1, "single layer"
    bp, tp, d = x_prompt.shape
    bs, ts, _ = x_sample.shape
    np_, ns_ = bp * tp, bs * ts
    (w_ada, b_ada, g_n1, g_n2, w_in, ln_v_g, ln_v_b, w_s, b_s, g_q, g_k, pe_k, w_c1k, w_c2k, pe_v, w_c1v, w_c2v,
     w_a, w_b, w_o, w_pq, sk1, sk2, expert_u, expert_v) = [a[0] for a in (
         w_ada, b_ada, g_n1, g_n2, w_in, ln_v_g, ln_v_b, w_s, b_s, g_q, g_k, pe_k, w_c1k, w_c2k, pe_v, w_c1v, w_c2v,
         w_a, w_b, w_o, w_pq, sk1, sk2, expert_u, expert_v)]

    n_gate = 3 * N_HEADS
    c0 = T_NSA * PROJ_TILE
    w_in_p = jnp.concatenate([w_in[:, :c0],
                              jnp.pad(w_in[:, c0:c0 + n_gate], ((0, 0), (0, PROJ_TILE - n_gate))),
                              w_in[:, c0 + n_gate:]], axis=1).astype(BF16)
    ones = jnp.ones((PROJ_TILE,), F32)
    zeros = jnp.zeros((PROJ_TILE,), F32)
    rep = PROJ_TILE // HEAD_DIM
    gains = [ones] * N_TILES
    flags = [zeros] * N_TILES
    for k in range(T_Q, T_KC):
        gains[k], flags[k] = jnp.tile(g_q, rep), ones
    gains[T_KS], flags[T_KS] = jnp.tile(g_k[1], rep), ones
    gains[T_KW], flags[T_KW] = jnp.tile(g_k[2], rep), ones
    gain = jnp.stack(gains)[:, None, :]
    flag = jnp.stack(flags)[:, None, :]
    tril = jnp.tril(w_s)
    wm_p = tril.astype(BF16)
    bsb_p = jnp.broadcast_to(b_s[:, :, None], (A_GROUPS, CHUNK, CHUNK))
    nrep = CHUNK // ts
    wm_s = jnp.einsum("ab,gij->gaibj", jnp.eye(nrep, dtype=F32), tril[:, :ts, :ts]).reshape(A_GROUPS, CHUNK, CHUNK).astype(BF16)
    bsb_s = jnp.broadcast_to(jnp.tile(b_s[:, :ts], (1, nrep))[:, :, None], (A_GROUPS, CHUNK, CHUNK))
    w1k_b, w2k_b, w1v_b, w2v_b = [a.astype(BF16) for a in (w_c1k, w_c2k, w_c1v, w_c2v)]
    wa_b, wb_b, wo_b, wpq_b = [a.astype(BF16) for a in (w_a, w_b, w_o, w_pq)]
    sk1_b, sk2_b = sk1.astype(BF16), sk2.astype(BF16)
    eu_b, ev_b = expert_u.astype(BF16), expert_v.astype(BF16)
    one_gain = jnp.ones((HEAD_DIM,), F32)

    def pool_weights(w1, pe):
        hid = w1.shape[1]
        w1x = w1.reshape(CMP_BLOCK // 2, 2, HEAD_DIM, hid).transpose(0, 2, 1, 3).reshape(CMP_BLOCK // 2 * HEAD_DIM, 2 * hid)
        pe8 = jnp.repeat(pe.reshape(CMP_BLOCK // 2, 2, 1, HEAD_DIM), N_KV, axis=2).reshape(CMP_BLOCK // 2, SUBLANE, HEAD_DIM)
        return w1x.astype(BF16), pe8

    w1k_x, pe8_k = pool_weights(w_c1k, pe_k)
    w1v_x, pe8_v = pool_weights(w_c1v, pe_v)

    nc = bp + bs
    ncp = -(-nc // SUBLANE) * SUBLANE
    c_all = jnp.pad(jnp.concatenate([c_prompt, c_sample], axis=0), ((0, ncp - nc), (0, 0)))
    mod = _modulation(c_all, w_ada, b_ada)
    mod_p = mod[:bp].reshape(bp, 6, 1, d)
    mod_s = mod[bp:nc].reshape(bs, 6, 1, d)

    tm_p = min(1024, tp)
    bt_s = min(1024 // ts, bs)

    proj_p = _in_projection(x_prompt, mod_p, g_n1, w_in_p, gain, flag, 1, tm_p)
    proj_s = _in_projection(x_sample, mod_s, g_n1, w_in_p, gain, flag, bt_s, ts)

    cpb = tp // CHUNK
    ya_p, vch_p = _mixer_a(proj_p, ln_v_g, ln_v_b, wm_p, bsb_p, bp, lambda i: i // cpb)
    ya_s, vch_s = _mixer_a(proj_s, ln_v_g, ln_v_b, wm_s, bsb_s, ns_ // CHUNK, lambda i: i)

    kcmp_p = _compress_prompt(proj_p, T_KC, bp, tp, pe_k, w1k_b, w2k_b, g_k[0], True)
    vcmp_p = _compress_prompt(proj_p, T_VC, bp, tp, pe_v, w1v_b, w2v_b, one_gain, False)
    n_phys, page = cache_k_cmp.shape[1], cache_k_cmp.shape[2]
    bpp = page // CMP_BLOCK
    blk_rows = CMP_BLOCK * N_KV
    kcmp_pool = _compress_pool(cache_k_cmp.reshape(n_phys * bpp, blk_rows, HEAD_DIM), pe8_k, w1k_x, w2k_b, g_k[0], True)
    vcmp_pool = _compress_pool(cache_v_cmp.reshape(n_phys * bpp, blk_rows, HEAD_DIM), pe8_v, w1v_x, w2v_b, one_gain, False)
    npg = page_table.shape[1]
    kcmp_s = kcmp_pool.reshape(n_phys, bpp * SUBLANE * HEAD_DIM)[page_table].reshape(bs, npg * bpp, SUBLANE, HEAD_DIM)
    vcmp_s = vcmp_pool.reshape(n_phys, bpp * SUBLANE * HEAD_DIM)[page_table].reshape(bs, npg * bpp, SUBLANE, HEAD_DIM)

    tables, ctab = _bias_tables(rel_bias, ATT_TILE)
    oc_p, sel_p = _cmp_select(proj_p, ctab, kcmp_p, vcmp_p, bp, tp)
    yb_p = _prompt_attention(proj_p, rel_bias, tables, oc_p, bp, tp, "sel", sel_p)
    yb_p = _prompt_attention(proj_p, rel_bias, tables, yb_p, bp, tp, "win")

    past = npg * page
    emat_s = jnp.asarray(_block_expand(past + LANE), BF16)
    yb_s, kwin_s, vwin_s = _sample_attention(
        proj_s, rel_bias, page_table, kcmp_s, vcmp_s,
        cache_k_sel.reshape(n_phys, page * N_KV, HEAD_DIM), cache_v_sel.reshape(n_phys, page * N_KV, HEAD_DIM),
        state_k_win.reshape(bs, -1, HEAD_DIM), state_v_win.reshape(bs, -1, HEAD_DIM), emat_s, bs, ts)

    t_p = _merge(proj_p, ya_p, yb_p, wa_b, wb_b)
    t_s = _merge(proj_s, ya_s, yb_s, wa_b, wb_b)
    x1_p, h2 = _out_projection(t_p, x_prompt, mod_p, g_n2, wo_b, 1, min(256, tp), np_ + ns_, 0)
    x1_s, h2 = _out_projection(t_s, x_sample, mod_s, g_n2, wo_b, min(256 // ts, bs), ts, np_ + ns_, np_, h2)

    s1t, s2t = _peer_scores(h2, wpq_b, sk1_b, sk2_b)
    cnt1, e1, rk2, e2 = _peer_topk(s1t, s2t)
    peer = _peer_dense(h2, eu_b, ev_b, cnt1, e1, rk2, e2)
    y_p = _final_residual(x1_p, peer, 0, mod_p, 1, min(512, tp))
    y_s = _final_residual(x1_s, peer, np_, mod_s, min(512 // ts, bs), ts)

    def kv_p(k):
        return proj_p[k].reshape(1, bp, tp, N_KV, HEAD_DIM)

    def kv_s(k):
        return proj_s[k].reshape(1, bs, ts, N_KV, HEAD_DIM)

    wb_p = min(WINDOW, tp)
    wlen = state_k_win.shape[2]
    return (y_p, y_s,
            kv_p(T_KC), kv_p(T_VC), kv_p(T_KS), kv_p(T_VS),
            kv_p(T_KW)[:, :, tp - wb_p:], kv_p(T_VW)[:, :, tp - wb_p:],
            vch_p.reshape(1, bp, CHUNK, -1),
            kv_s(T_KC), kv_s(T_VC), kv_s(T_KS), kv_s(T_VS),
            kwin_s.reshape(1, bs, wlen, N_KV, HEAD_DIM), vwin_s.reshape(1, bs, wlen, N_KV, HEAD_DIM),
            vch_s.reshape(1, bs, ts, -1))


def kernel(x_prompt, x_sample, cache_k_cmp, cache_v_cmp, cache_k_sel, cache_v_sel, state_k_win, state_v_win, page_table, c_prompt, c_sample, rel_bias, w_ada, b_ada, g_n1, g_n2, w_in, ln_v_g, ln_v_b, w_s, b_s, g_q, g_k, pe_k, w_c1k, w_c2k, pe_v, w_c1v, w_c2v, w_a, w_b, w_o, w_pq, sk1, sk2, expert_u, expert_v):
    return _forward(x_prompt, x_sample, cache_k_cmp, cache_v_cmp, cache_k_sel, cache_v_sel, state_k_win, state_v_win,
                    page_table, c_prompt, c_sample, rel_bias, w_ada, b_ada, g_n1, g_n2, w_in, ln_v_g, ln_v_b, w_s, b_s,
                    g_q, g_k, pe_k, w_c1k, w_c2k, pe_v, w_c1v, w_c2v, w_a, w_b, w_o, w_pq, sk1, sk2, expert_u, expert_v)
```

```python
import functools
import math

import numpy as np
import jax
import jax.numpy as jnp
from jax import lax
from jax.experimental import pallas as pl
from jax.experimental.pallas import tpu as pltpu

F32 = jnp.float32
BF16 = jnp.bfloat16

N_HEADS = 16
HEAD_DIM = 128
N_KV = 4
GQA = N_HEADS // N_KV
KV_WIDTH = N_KV * HEAD_DIM
CHUNK = 128
A_GROUPS = 8
CMP_BLOCK = 64
SEL_BLOCK = 64
N_SEL = 16
WINDOW = 512
N_BUCKETS = 32
MAX_DISTANCE = 128
N_KEYS = 128
PEER_HEADS = 8
PEER_TOPK = 16
ATTN_SCALE = HEAD_DIM ** -0.5
NEG = -1e30
FORCE_BONUS = 1e4
EPS = 1e-6
LANE = 128
SUBLANE = 8
PROJ_TILE = 512
ATT_TILE = 256
VMEM_LIMIT = 56 * 1024 * 1024

T_U, T_V, T_Q, T_KC, T_VC, T_KS, T_VS, T_KW, T_VW, T_NSA, T_GA, T_GB, N_TILES = 0, 2, 4, 8, 9, 10, 11, 12, 13, 14, 15, 19, 23


def _bucket_thresholds():
    n = np.arange(0, 2 * MAX_DISTANCE)
    nf = np.maximum(n, 1).astype(np.float32)
    half = N_BUCKETS // 2
    large = half + (np.log(nf / half) / math.log(MAX_DISTANCE / half) * (N_BUCKETS - half)).astype(np.int32)
    b = np.where(n < half, n, np.minimum(large, N_BUCKETS - 1))
    assert np.all(np.diff(b) >= 0) and b[-1] == N_BUCKETS - 1
    return [int(np.argmax(b >= k)) for k in range(N_BUCKETS)]


BUCKET_THR = _bucket_thresholds()
FAR_DIST = BUCKET_THR[-1]
assert FAR_DIST <= MAX_DISTANCE


def _cparams(sem, vmem=VMEM_LIMIT):
    return pltpu.CompilerParams(dimension_semantics=sem, vmem_limit_bytes=vmem)


def _gelu(x):
    c = 2.0 * math.sqrt(2.0 / math.pi)
    u = (x * x) * (-0.044715 * c) - c
    return x * (1.0 / (1.0 + jnp.exp(x * u)))


def _sigmoid(x):
    return 1.0 / (1.0 + jnp.exp(-x))


def _dot_nt(a, b):
    return lax.dot_general(a, b, (((1,), (1,)), ((), ())), preferred_element_type=F32)


def _dot_tn(a, b):
    return lax.dot_general(a, b, (((0,), (0,)), ((), ())), preferred_element_type=F32)


def _bias_chain(dist, rbs):
    b = jnp.full(dist.shape, rbs[0], F32)
    for k in range(1, N_BUCKETS):
        b = jnp.where(dist >= BUCKET_THR[k], rbs[k], b)
    return b


def _mod_kernel(c_ref, w_ref, b_ref, o_ref):
    c = c_ref[...]
    a = (c * _sigmoid(c)).astype(BF16)
    o_ref[...] = jnp.dot(a, w_ref[...].astype(BF16), preferred_element_type=F32) + b_ref[...]


def _modulation(c_all, w_ada, b_ada):
    m, d = c_all.shape
    n = w_ada.shape[1]
    tn = 1024
    return pl.pallas_call(
        _mod_kernel,
        grid=(n // tn,),
        in_specs=[pl.BlockSpec((m, d), lambda j: (0, 0)),
                  pl.BlockSpec((d, tn), lambda j: (0, j)),
                  pl.BlockSpec((1, tn), lambda j: (0, j))],
        out_specs=pl.BlockSpec((m, tn), lambda j: (0, j)),
        out_shape=jax.ShapeDtypeStruct((m, n), F32),
        compiler_params=_cparams(("arbitrary",)),
        name="adaln_mod",
    )(c_all, w_ada, b_ada.reshape(1, n))


def _inproj_kernel(x_ref, sc_ref, sh_ref, gn_ref, w_ref, gain_ref, flag_ref, o_ref, h_scr):
    j = pl.program_id(1)

    @pl.when(j == 0)
    def _():
        x = x_ref[...]
        r = lax.rsqrt(jnp.mean(x * x, axis=-1, keepdims=True) + EPS)
        h = (x * r) * gn_ref[...] * (1.0 + sc_ref[...]) + sh_ref[...]
        h_scr[...] = h.reshape(h_scr.shape).astype(BF16)

    y = jnp.dot(h_scr[...], w_ref[...], preferred_element_type=F32)

    @pl.when(j < T_Q)
    def _():
        o_ref[...] = _gelu(y)

    @pl.when((j >= T_Q) & (j < T_NSA))
    def _():
        parts = []
        for hh in range(PROJ_TILE // HEAD_DIM):
            yh = y[:, hh * HEAD_DIM:(hh + 1) * HEAD_DIM]
            parts.append(yh * lax.rsqrt(jnp.mean(yh * yh, axis=-1, keepdims=True) + EPS))
        yn = jnp.concatenate(parts, axis=1) * gain_ref[...]
        o_ref[...] = jnp.where(flag_ref[...] > 0.5, yn, y)

    @pl.when(j >= T_NSA)
    def _():
        o_ref[...] = _sigmoid(y)


def _in_projection(x3, mod4, g_n1, w_in_p, gain, flag, bt, tt):
    nb, tb, d = x3.shape
    tpb = tb // tt
    tm = bt * tt
    n = nb * tb
    grid = (n // tm, N_TILES)
    return pl.pallas_call(
        _inproj_kernel,
        grid=grid,
        in_specs=[pl.BlockSpec((bt, tt, d), lambda i, j: (i // tpb, i % tpb, 0)),
                  pl.BlockSpec((bt, None, 1, d), lambda i, j: (i // tpb, 1, 0, 0)),
                  pl.BlockSpec((bt, None, 1, d), lambda i, j: (i // tpb, 0, 0, 0)),
                  pl.BlockSpec((1, 1, d), lambda i, j: (0, 0, 0)),
                  pl.BlockSpec((d, PROJ_TILE), lambda i, j: (0, j)),
                  pl.BlockSpec((None, 1, PROJ_TILE), lambda i, j: (j, 0, 0)),
                  pl.BlockSpec((None, 1, PROJ_TILE), lambda i, j: (j, 0, 0))],
        out_specs=pl.BlockSpec((None, tm, PROJ_TILE), lambda i, j: (j, i, 0)),
        out_shape=jax.ShapeDtypeStruct((N_TILES, n, PROJ_TILE), F32),
        scratch_shapes=[pltpu.VMEM((tm, d), BF16)],
        compiler_params=_cparams(("parallel", "arbitrary")),
        name="in_projection",
    )(x3, mod4, mod4, g_n1.reshape(1, 1, d), w_in_p, gain, flag)


def _mixa_kernel(u0_ref, u1_ref, v0_ref, v1_ref, lg_ref, lb_ref, wm_ref, bs_ref, ya_ref, vch_ref):
    v = jnp.concatenate([v0_ref[...], v1_ref[...]], axis=1)
    mu = jnp.mean(v, axis=-1, keepdims=True)
    var = jnp.mean(jnp.square(v - mu), axis=-1, keepdims=True)
    vln = ((v - mu) * lax.rsqrt(var + EPS)) * lg_ref[...] + lb_ref[...]
    vch_ref[...] = vln
    u = jnp.concatenate([u0_ref[...], u1_ref[...]], axis=1)
    vb = vln.astype(BF16)
    gd = vln.shape[1] // A_GROUPS
    for g in range(A_GROUPS):
        sl = slice(g * gd, (g + 1) * gd)
        s = jnp.dot(wm_ref[g], vb[:, sl], preferred_element_type=F32) + bs_ref[g]
        ya_ref[:, sl] = (u[:, sl] * s).astype(BF16)


def _mixer_a(proj, ln_g, ln_b, wm, bsb, vch_blocks, vch_map):
    n = proj.shape[1]
    aw = 2 * PROJ_TILE

    def tile(k):
        return pl.BlockSpec((None, CHUNK, PROJ_TILE), lambda i, k=k: (k, i, 0))

    return pl.pallas_call(
        _mixa_kernel,
        grid=(n // CHUNK,),
        in_specs=[tile(T_U), tile(T_U + 1), tile(T_V), tile(T_V + 1),
                  pl.BlockSpec((1, aw), lambda i: (0, 0)),
                  pl.BlockSpec((1, aw), lambda i: (0, 0)),
                  pl.BlockSpec((A_GROUPS, CHUNK, CHUNK), lambda i: (0, 0, 0)),
                  pl.BlockSpec((A_GROUPS, CHUNK, CHUNK), lambda i: (0, 0, 0))],
        out_specs=[pl.BlockSpec((CHUNK, aw), lambda i: (i, 0)),
                   pl.BlockSpec((CHUNK, aw), lambda i: (vch_map(i), 0))],
        out_shape=[jax.ShapeDtypeStruct((n, aw), BF16),
                   jax.ShapeDtypeStruct((vch_blocks * CHUNK, aw), F32)],
        compiler_params=_cparams(("arbitrary",)),
        name="mixer_a",
    )(proj, proj, proj, proj, ln_g.reshape(1, aw), ln_b.reshape(1, aw), wm, bsb)


def _compress_tail(hid, w2_ref, gain_ref, do_rms):
    out = jnp.dot(_gelu(hid).astype(BF16), w2_ref[...], preferred_element_type=F32)
    if do_rms:
        out = out * lax.rsqrt(jnp.mean(out * out, axis=-1, keepdims=True) + EPS) * gain_ref[...]
    return out


def _compress_prompt_kernel(x0_ref, x1_ref, x2_ref, x3_ref, pe_ref, w1_ref, w2_ref, gain_ref, o_ref, lhs_scr, *, nb, do_rms):
    for s_ in range(CMP_BLOCK):
        for g, x_ref in enumerate((x0_ref, x1_ref, x2_ref, x3_ref)):
            rows = x_ref[pl.ds(s_, nb, stride=CMP_BLOCK), :]
            lhs_scr[g * nb:(g + 1) * nb, s_ * HEAD_DIM:(s_ + 1) * HEAD_DIM] = (rows + pe_ref[s_:s_ + 1, :]).astype(BF16)
    hid = jnp.dot(lhs_scr[...], w1_ref[...], preferred_element_type=F32)
    out = _compress_tail(hid, w2_ref, gain_ref, do_rms)
    for g in range(N_KV):
        o_ref[:, g * HEAD_DIM:(g + 1) * HEAD_DIM] = out[g * nb:(g + 1) * nb]


def _compress_prompt(proj, tile, bsz, t, pe, w1b, w2b, gain, do_rms):
    nb = t // CMP_BLOCK
    hid = w1b.shape[1]
    return pl.pallas_call(
        functools.partial(_compress_prompt_kernel, nb=nb, do_rms=do_rms),
        grid=(bsz,),
        in_specs=[pl.BlockSpec((None, t, HEAD_DIM), lambda b, g=g: (tile, b, g)) for g in range(N_KV)] + [
                  pl.BlockSpec((CMP_BLOCK, HEAD_DIM), lambda b: (0, 0)),
                  pl.BlockSpec((CMP_BLOCK * HEAD_DIM, hid), lambda b: (0, 0)),
                  pl.BlockSpec((hid, HEAD_DIM), lambda b: (0, 0)),
                  pl.BlockSpec((1, HEAD_DIM), lambda b: (0, 0))],
        out_specs=pl.BlockSpec((nb, KV_WIDTH), lambda b: (b, 0)),
        out_shape=jax.ShapeDtypeStruct((bsz * nb, KV_WIDTH), F32),
        scratch_shapes=[pltpu.VMEM((N_KV * nb, CMP_BLOCK * HEAD_DIM), BF16)],
        compiler_params=_cparams(("arbitrary",)),
        name="compress_prompt",
    )(proj, proj, proj, proj, pe, w1b, w2b, gain.reshape(1, HEAD_DIM))


def _compress_pool_kernel(x_ref, pe8_ref, w1_ref, w2_ref, gain_ref, o_ref, lhs_scr, *, do_rms):
    tb = x_ref.shape[0]
    m = tb * SUBLANE
    hid = w2_ref.shape[0]
    for j in range(CMP_BLOCK // 2):
        xj = x_ref[:, SUBLANE * j:SUBLANE * (j + 1), :] + pe8_ref[j]
        lhs_scr[:, j * HEAD_DIM:(j + 1) * HEAD_DIM] = xj.reshape(m, HEAD_DIM).astype(BF16)
    acc = jnp.dot(lhs_scr[...], w1_ref[...], preferred_element_type=F32)
    hidv = acc[:, :hid] + pltpu.roll(acc[:, hid:], m - N_KV, 0)
    out = _compress_tail(hidv, w2_ref, gain_ref, do_rms)
    o_ref[...] = out.reshape(tb, SUBLANE, HEAD_DIM)


def _compress_pool(x3, pe8, w1x, w2b, gain, do_rms, tb=128):
    nblk = x3.shape[0]
    tb = min(tb, nblk)
    hid = w2b.shape[0]
    kdim = CMP_BLOCK // 2 * HEAD_DIM
    return pl.pallas_call(
        functools.partial(_compress_pool_kernel, do_rms=do_rms),
        grid=(nblk // tb,),
        in_specs=[pl.BlockSpec((tb, CMP_BLOCK * N_KV, HEAD_DIM), lambda i: (i, 0, 0)),
                  pl.BlockSpec((CMP_BLOCK // 2, SUBLANE, HEAD_DIM), lambda i: (0, 0, 0), pipeline_mode=pl.Buffered(1)),
                  pl.BlockSpec((kdim, 2 * hid), lambda i: (0, 0), pipeline_mode=pl.Buffered(1)),
                  pl.BlockSpec((hid, HEAD_DIM), lambda i: (0, 0), pipeline_mode=pl.Buffered(1)),
                  pl.BlockSpec((1, HEAD_DIM), lambda i: (0, 0), pipeline_mode=pl.Buffered(1))],
        out_specs=pl.BlockSpec((tb, SUBLANE, HEAD_DIM), lambda i: (i, 0, 0)),
        out_shape=jax.ShapeDtypeStruct((nblk, SUBLANE, HEAD_DIM), F32),
        scratch_shapes=[pltpu.VMEM((tb * SUBLANE, kdim), BF16)],
        compiler_params=_cparams(("arbitrary",)),
        name="compress_pool",
    )(x3, pe8, w1x, w2b, gain.reshape(1, HEAD_DIM))


def _bias_table_kernel(rb_ref, o_ref, c_ref, *, ts):
    g = pl.program_id(0)
    i = lax.broadcasted_iota(jnp.int32, (ts, ts), 0)
    j = lax.broadcasted_iota(jnp.int32, (ts, ts), 1)
    for d in range(2):
        dist = d * ts + i - j
        for r in range(GQA):
            rbs = [rb_ref[k, g * GQA + r] for k in range(N_BUCKETS)]
            o_ref[d, r * ts:(r + 1) * ts, :] = _bias_chain(dist, rbs) - rbs[-1]
    ic = lax.broadcasted_iota(jnp.int32, (ts, LANE), 0)
    nc = lax.broadcasted_iota(jnp.int32, (ts, LANE), 1) - LANE // 2
    dist_c = ic - (nc * CMP_BLOCK + CMP_BLOCK - 1)
    for r in range(GQA):
        rbs = [rb_ref[k, g * GQA + r] for k in range(N_BUCKETS)]
        c_ref[r * ts:(r + 1) * ts, :] = _bias_chain(dist_c, rbs)


def _bias_tables(rel_bias, ts):
    return pl.pallas_call(
        functools.partial(_bias_table_kernel, ts=ts),
        grid=(N_KV,),
        in_specs=[pl.BlockSpec(memory_space=pltpu.SMEM)],
        out_specs=[pl.BlockSpec((None, 2, GQA * ts, ts), lambda g: (g, 0, 0, 0)),
                   pl.BlockSpec((None, GQA * ts, LANE), lambda g: (g, 0, 0))],
        out_shape=[jax.ShapeDtypeStruct((N_KV, 2, GQA * ts, ts), F32),
                   jax.ShapeDtypeStruct((N_KV, GQA * ts, LANE), F32)],
        compiler_params=_cparams(("arbitrary",)),
        name="bias_tables",
    )(rel_bias)


def _rank_select(score, n_sel):
    t = score.shape[1]
    ngrp = -(-n_sel // SUBLANE)
    jrow = lax.broadcasted_iota(jnp.int32, (SUBLANE, t), 0)
    sel = []
    for gb in range(ngrp):
        blk = score[gb * SUBLANE:(gb + 1) * SUBLANE, :]
        rank = jnp.zeros((SUBLANE, t), F32)
        for i in range(n_sel):
            row = score[i:i + 1, :]
            if i < gb * SUBLANE:
                beats = row >= blk
            elif i >= (gb + 1) * SUBLANE:
                beats = row > blk
            else:
                beats = (row > blk) | ((jrow > i - gb * SUBLANE) & (row == blk))
            rank = rank + jnp.where(beats, 1.0, 0.0)
        keep = (rank < float(min(N_SEL, n_sel))) & (jrow + gb * SUBLANE < n_sel)
        sel.append(jnp.where(keep, 1.0, 0.0))
    sel.append(jnp.zeros((score.shape[0] - ngrp * SUBLANE, t), F32))
    return jnp.concatenate(sel, axis=0)


def _cmp_kernel(ct_ref, q0_ref, q1_ref, q2_ref, q3_ref, nsa_ref, kc_ref, vc_ref, oc_ref, sel_ref, *, tq, nb, n_sel):
    qt = pl.program_id(1)
    nsa = nsa_ref[...]
    row = lax.broadcasted_iota(jnp.int32, (tq, LANE), 0) + qt * tq
    col = lax.broadcasted_iota(jnp.int32, (tq, LANE), 1)
    dist = row - (col * CMP_BLOCK + CMP_BLOCK - 1)
    valid = (dist >= 0) & (col < nb)
    cur = row // SEL_BLOCK
    forced = (col == 0) | (col == cur) | (col == cur - 1)
    shift = (qt * (tq // CMP_BLOCK) + LANE // 2) % LANE
    pad = jnp.zeros((LANE - nb, HEAD_DIM), F32)
    for g in range(N_KV):
        q = (q0_ref, q1_ref, q2_ref, q3_ref)[g][...]
        qst = jnp.concatenate([q[:, r * HEAD_DIM:(r + 1) * HEAD_DIM] for r in range(GQA)], axis=0).astype(BF16)
        kg = jnp.concatenate([kc_ref[:, g * HEAD_DIM:(g + 1) * HEAD_DIM], pad], axis=0).astype(BF16)
        vg = jnp.concatenate([vc_ref[:, g * HEAD_DIM:(g + 1) * HEAD_DIM], pad], axis=0).astype(BF16)
        s = _dot_nt(qst, kg) * ATTN_SCALE
        ps = []
        imp = jnp.zeros((tq, LANE), F32)
        for r in range(GQA):
            h = g * GQA + r
            b = pltpu.roll(ct_ref[g, r * tq:(r + 1) * tq, :], shift, 1)
            sr = jnp.where(valid, s[r * tq:(r + 1) * tq] + b, NEG)
            m = jnp.max(sr, axis=-1, keepdims=True)
            p = jnp.where(valid, jnp.exp(sr - m), 0.0)
            den = jnp.sum(p, axis=-1, keepdims=True)
            p = p * (1.0 / jnp.maximum(den, 1e-30))
            imp = imp + p
            ps.append(p)
        o = jnp.dot(jnp.concatenate(ps, axis=0).astype(BF16), vg, preferred_element_type=F32)
        for r in range(GQA):
            h = g * GQA + r
            oc_ref[:, h * HEAD_DIM:(h + 1) * HEAD_DIM] = o[r * tq:(r + 1) * tq] * nsa[:, 3 * h:3 * h + 1]
        score = jnp.where(col <= cur, imp + jnp.where(forced, FORCE_BONUS, 0.0), NEG)
        score = jnp.where(col < n_sel, score, -3e38)
        sel_ref[g] = _rank_select(score.T, n_sel).T


def _cmp_select(proj, ctab, kcmp, vcmp, bsz, t, tq=ATT_TILE):
    nb = kcmp.shape[0] // bsz
    n_sel = -(-t // SEL_BLOCK)
    nq = t // tq
    n = proj.shape[1]
    assert nb <= LANE // 2 and SEL_BLOCK == CMP_BLOCK

    def tile(k):
        return pl.BlockSpec((None, tq, PROJ_TILE), lambda b, i, k=k: (k, b * nq + i, 0))

    return pl.pallas_call(
        functools.partial(_cmp_kernel, tq=tq, nb=nb, n_sel=n_sel),
        grid=(bsz, nq),
        in_specs=[pl.BlockSpec((N_KV, GQA * tq, LANE), lambda b, i: (0, 0, 0)),
                  tile(T_Q), tile(T_Q + 1), tile(T_Q + 2), tile(T_Q + 3), tile(T_NSA),
                  pl.BlockSpec((nb, KV_WIDTH), lambda b, i: (b, 0)),
                  pl.BlockSpec((nb, KV_WIDTH), lambda b, i: (b, 0))],
        out_specs=[pl.BlockSpec((tq, N_HEADS * HEAD_DIM), lambda b, i: (b * nq + i, 0)),
                   pl.BlockSpec((None, N_KV, tq, LANE), lambda b, i: (b, 0, i, 0))],
        out_shape=[jax.ShapeDtypeStruct((n, N_HEADS * HEAD_DIM), F32),
                   jax.ShapeDtypeStruct((bsz, N_KV, t, LANE), F32)],
        compiler_params=_cparams(("parallel", "arbitrary")),
        name="cmp_select",
    )(ctab, proj, proj, proj, proj, proj, kcmp, vcmp)


MASK_BIG = 2.0 ** 100
AUG = 2 * HEAD_DIM
ROW_BLOCK = 128


def _attn_kernel(rb_ref, q_ref, nsa_ref, k_ref, v_ref, tb_ref, prev_ref, *rest, mode, tq, branch):
    if mode == "sel":
        sel_ref, o_ref, kb, vb, qa, s_scr, p_scr, m_s, a_s, acc_s = rest
    else:
        o_ref, kb, vb, qa, s_scr, p_scr, m_s, a_s, acc_s = rest
    g = pl.program_id(1)
    qt = pl.program_id(2)
    tk = tq
    rows4 = GQA * tq
    t_all = kb.shape[0]

    @pl.when(qt == 0)
    def _():
        krow = lax.broadcasted_iota(jnp.int32, (t_all, LANE), 0)
        lane = lax.broadcasted_iota(jnp.int32, (t_all, LANE), 1)
        onehot = ((lane < SEL_BLOCK) & (krow // SEL_BLOCK == lane)) | (lane == SEL_BLOCK) | (lane == SEL_BLOCK + 1)
        kb[:, 0:HEAD_DIM] = k_ref[...].astype(BF16)
        kb[:, HEAD_DIM:AUG] = onehot.astype(BF16)
        vb[:, 0:HEAD_DIM] = v_ref[...].astype(BF16)
        vb[:, HEAD_DIM:AUG] = (lane == 0).astype(BF16)

    q = q_ref[...]
    lane_q = lax.broadcasted_iota(jnp.int32, (tq, LANE), 1)
    if mode == "sel":
        selm = jnp.where(lane_q < SEL_BLOCK, (sel_ref[...] - 1.0) * MASK_BIG, 0.0)
    else:
        selm = jnp.zeros((tq, LANE), F32)
    for r in range(GQA):
        b_far = jnp.full((tq, LANE), rb_ref[N_BUCKETS - 1, g * GQA + r], F32)
        b_hi = b_far.astype(BF16).astype(F32)
        ext = jnp.where(lane_q == SEL_BLOCK, b_hi, jnp.where(lane_q == SEL_BLOCK + 1, b_far - b_hi, selm))
        qa[r * tq:(r + 1) * tq, 0:HEAD_DIM] = (q[:, r * HEAD_DIM:(r + 1) * HEAD_DIM] * ATTN_SCALE).astype(BF16)
        qa[r * tq:(r + 1) * tq, HEAD_DIM:AUG] = ext.astype(BF16)
    m_s[...] = jnp.full(m_s.shape, NEG, F32)
    acc_s[...] = jnp.zeros(acc_s.shape, F32)
    nrb = rows4 // ROW_BLOCK
    rowpos = lax.broadcasted_iota(jnp.int32, (ROW_BLOCK, tk), 0)
    colpos = lax.broadcasted_iota(jnp.int32, (ROW_BLOCK, tk), 1)

    def chunk(kt, table, mask_kind):
        k0 = pl.multiple_of(kt * tk, tk)
        s_scr[...] = _dot_nt(qa[...], kb[pl.ds(k0, tk), :])
        for rb in range(nrb):
            rs = slice(rb * ROW_BLOCK, (rb + 1) * ROW_BLOCK)
            s = s_scr[rs, :]
            if table is not None:
                s = s + tb_ref[table, rs, :]
            if mask_kind is not None:
                rp = rowpos + (rb * ROW_BLOCK) % tq
                keep = (rp >= colpos) if mask_kind == "causal" else (colpos > rp)
                s = jnp.where(keep, s, NEG)
            m_old = m_s[rs, :]
            m_new = jnp.maximum(m_old, jnp.max(s, axis=-1, keepdims=True))
            p_scr[rs, :] = jnp.exp(s - jnp.concatenate([m_new] * (tk // LANE), axis=1)).astype(BF16)
            a_s[rs, :] = jnp.exp(m_old - m_new)
            m_s[rs, :] = m_new
        pv = jnp.dot(p_scr[...], vb[pl.ds(k0, tk), :], preferred_element_type=F32)
        a = a_s[...]
        acc_s[...] = jnp.concatenate([a] * (AUG // LANE), axis=1) * acc_s[...] + pv

    chunk(qt, 0, "causal")

    @pl.when(qt >= 1)
    def _():
        chunk(qt - 1, 1, None)

    if mode == "sel":
        def far_body(kt, carry):
            chunk(kt, None, None)
            return carry
        lax.fori_loop(0, jnp.maximum(qt - 1, 0), far_body, 0)
    else:
        @pl.when(qt >= 2)
        def _():
            chunk(qt - 2, None, "window")

    acc = acc_s[...]
    o = acc[:, 0:HEAD_DIM] * (1.0 / jnp.maximum(acc[:, HEAD_DIM:HEAD_DIM + 1], 1e-30))
    nsa = nsa_ref[...]
    lane = lax.broadcasted_iota(jnp.int32, nsa.shape, 1)
    for r in range(GQA):
        gidx = (g * GQA + r) * 3 + branch
        gate = jnp.sum(jnp.where(lane == gidx, nsa, 0.0), axis=-1, keepdims=True)
        hs = slice(r * HEAD_DIM, (r + 1) * HEAD_DIM)
        o_ref[:, hs] = prev_ref[:, hs] + o[r * tq:(r + 1) * tq] * gate


def _prompt_attention(proj, rel_bias, tables, prev, bsz, t, mode, sel=None, tq=ATT_TILE):
    nq = t // tq
    n = proj.shape[1]
    assert t // SEL_BLOCK <= SEL_BLOCK and tq % ROW_BLOCK == 0
    if mode == "sel":
        tk_, tv_, branch = T_KS, T_VS, 1
    else:
        tk_, tv_, branch = T_KW, T_VW, 2
        assert WINDOW == 2 * tq
    in_specs = [pl.BlockSpec(memory_space=pltpu.SMEM),
                pl.BlockSpec((None, tq, PROJ_TILE), lambda b, g, i: (T_Q + g, b * nq + i, 0)),
                pl.BlockSpec((None, tq, PROJ_TILE), lambda b, g, i: (T_NSA, b * nq + i, 0)),
                pl.BlockSpec((None, t, HEAD_DIM), lambda b, g, i: (tk_, b, g)),
                pl.BlockSpec((None, t, HEAD_DIM), lambda b, g, i: (tv_, b, g)),
                pl.BlockSpec((None, 2, GQA * tq, tq), lambda b, g, i: (g, 0, 0, 0)),
                pl.BlockSpec((tq, GQA * HEAD_DIM), lambda b, g, i: (b * nq + i, g))]
    args = [rel_bias, proj, proj, proj, proj, tables, prev]
    if mode == "sel":
        in_specs += [pl.BlockSpec((None, None, tq, LANE), lambda b, g, i: (b, g, i, 0))]
        args += [sel]
    return pl.pallas_call(
        functools.partial(_attn_kernel, mode=mode, tq=tq, branch=branch),
        grid=(bsz, N_KV, nq),
        in_specs=in_specs,
        out_specs=pl.BlockSpec((tq, GQA * HEAD_DIM), lambda b, g, i: (b * nq + i, g)),
        out_shape=jax.ShapeDtypeStruct((n, N_HEADS * HEAD_DIM), F32),
        scratch_shapes=[pltpu.VMEM((t, AUG), BF16), pltpu.VMEM((t, AUG), BF16),
                        pltpu.VMEM((GQA * tq, AUG), BF16),
                        pltpu.VMEM((GQA * tq, tq), F32), pltpu.VMEM((GQA * tq, tq), BF16),
                        pltpu.VMEM((GQA * tq, LANE), F32), pltpu.VMEM((GQA * tq, LANE), F32),
                        pltpu.VMEM((GQA * tq, AUG), F32)],
        compiler_params=_cparams(("parallel", "parallel", "arbitrary")),
        name="attn_" + mode,
    )(*args)


def _masked_softmax(s, mask):
    s = jnp.where(mask, s, NEG)
    m = jnp.max(s, axis=-1, keepdims=True)
    p = jnp.where(mask, jnp.exp(s - m), 0.0)
    den = jnp.sum(p, axis=-1, keepdims=True)
    return p * (1.0 / jnp.maximum(den, 1e-30))


def _near_far_bias(rb_ref, g, dist_near, tdec, width, near):
    rows = []
    for r in range(GQA):
        h = g * GQA + r
        rbs = [rb_ref[k, h] for k in range(N_BUCKETS)]
        nb_ = _bias_chain(dist_near, rbs)
        rows.append(jnp.concatenate([jnp.full((tdec, width - near), rbs[-1], F32), nb_], axis=1))
    return jnp.concatenate(rows, axis=0)


def _sattn_kernel(pt_ref, rb_ref, q0_ref, q1_ref, q2_ref, q3_ref, ksn_ref, vsn_ref, kwn_ref, vwn_ref, nsa_ref,
                  kc_ref, vc_ref, *rest, npg, page, tdec, wlen):
    kpages = rest[:npg]
    vpages = rest[npg:2 * npg]
    skw_ref, svw_ref, e_ref, yb_ref, kwo_ref, vwo_ref, kbuf, vbuf, wkb, wvb = rest[2 * npg:]
    past = npg * page
    lk = past + LANE
    nb = kc_ref.shape[0]
    n_sel = -(-(past + tdec) // SEL_BLOCK)
    rows_w = wlen * N_KV
    near = 2 * LANE
    rq = GQA * tdec
    nsa = nsa_ref[...]

    kwo_ref[0:rows_w - tdec * N_KV, :] = skw_ref[tdec * N_KV:rows_w, :]
    vwo_ref[0:rows_w - tdec * N_KV, :] = svw_ref[tdec * N_KV:rows_w, :]
    for g in range(N_KV):
        kwo_ref[pl.ds(rows_w - tdec * N_KV + g, tdec, stride=N_KV), :] = kwn_ref[:, g * HEAD_DIM:(g + 1) * HEAD_DIM]
        vwo_ref[pl.ds(rows_w - tdec * N_KV + g, tdec, stride=N_KV), :] = vwn_ref[:, g * HEAD_DIM:(g + 1) * HEAD_DIM]

    trow1 = lax.broadcasted_iota(jnp.int32, (tdec, LANE), 0) + past
    col1 = lax.broadcasted_iota(jnp.int32, (tdec, LANE), 1)
    dist_c = trow1 - (col1 * CMP_BLOCK + CMP_BLOCK - 1)
    valid_c = (dist_c >= 0) & (col1 < nb)
    cur = trow1 // SEL_BLOCK
    forced = (col1 == 0) | (col1 == cur) | (col1 == cur - 1)

    trow_s = (lax.broadcasted_iota(jnp.int32, (rq, lk), 0) & (tdec - 1)) + past
    pos_s = lax.broadcasted_iota(jnp.int32, (rq, lk), 1)
    causal_s = pos_s <= trow_s
    dist_sn = (lax.broadcasted_iota(jnp.int32, (tdec, near), 0) + past) - (lax.broadcasted_iota(jnp.int32, (tdec, near), 1) + lk - near)

    wl = wlen + LANE
    qidx_w = (lax.broadcasted_iota(jnp.int32, (rq, wl), 0) & (tdec - 1)) + wlen
    kidx_w = lax.broadcasted_iota(jnp.int32, (rq, wl), 1)
    dist_w = qidx_w - kidx_w
    mask_w = (dist_w >= 0) & (dist_w < WINDOW)
    dist_wn = (lax.broadcasted_iota(jnp.int32, (tdec, near), 0) + wlen) - (lax.broadcasted_iota(jnp.int32, (tdec, near), 1) + wl - near)

    zpad = jnp.zeros((LANE - tdec, HEAD_DIM), F32)
    cpad = jnp.zeros((LANE - nb, HEAD_DIM), F32)
    for g in range(N_KV):
        gs = slice(g * HEAD_DIM, (g + 1) * HEAD_DIM)
        q = (q0_ref, q1_ref, q2_ref, q3_ref)[g][...]
        qst = jnp.concatenate([q[:, r * HEAD_DIM:(r + 1) * HEAD_DIM] for r in range(GQA)], axis=0).astype(BF16)

        kg = jnp.concatenate([kc_ref[:, g, :], cpad], axis=0).astype(BF16)
        vg = jnp.concatenate([vc_ref[:, g, :], cpad], axis=0).astype(BF16)
        s = _dot_nt(qst, kg) * ATTN_SCALE
        ps = []
        imp = jnp.zeros((tdec, LANE), F32)
        for r in range(GQA):
            h = g * GQA + r
            b = _bias_chain(dist_c, [rb_ref[k, h] for k in range(N_BUCKETS)])
            p = _masked_softmax(s[r * tdec:(r + 1) * tdec] + b, valid_c)
            imp = imp + p
            ps.append(p)
        o_c = jnp.dot(jnp.concatenate(ps, axis=0).astype(BF16), vg, preferred_element_type=F32)

        score = jnp.where(col1 <= cur, imp + jnp.where(forced, FORCE_BONUS, 0.0), NEG)
        score = jnp.where(col1 < n_sel, score, -3e38)
        rank = jnp.zeros((tdec, LANE), F32)
        for i in range(n_sel):
            ci = score[:, i:i + 1]
            rank = rank + ((ci > score) | ((ci == score) & (col1 > i))).astype(F32)
        sel = ((rank < float(min(N_SEL, n_sel))) & (col1 < n_sel)).astype(F32)

        for p_ in range(npg):
            kbuf[p_ * page:(p_ + 1) * page, :] = kpages[p_][pl.ds(g, page, stride=N_KV), :].astype(BF16)
            vbuf[p_ * page:(p_ + 1) * page, :] = vpages[p_][pl.ds(g, page, stride=N_KV), :].astype(BF16)
        kbuf[past:lk, :] = jnp.concatenate([ksn_ref[:, gs], zpad], axis=0).astype(BF16)
        vbuf[past:lk, :] = jnp.concatenate([vsn_ref[:, gs], zpad], axis=0).astype(BF16)
        s = _dot_nt(qst, kbuf[...]) * ATTN_SCALE + _near_far_bias(rb_ref, g, dist_sn, tdec, lk, near)
        sel4 = jnp.concatenate([sel] * GQA, axis=0).astype(BF16)
        mask = (jnp.dot(sel4, e_ref[...], preferred_element_type=F32) > 0.5) & causal_s
        o_s = jnp.dot(_masked_softmax(s, mask).astype(BF16), vbuf[...], preferred_element_type=F32)

        wkb[0:wlen, :] = skw_ref[pl.ds(g, wlen, stride=N_KV), :].astype(BF16)
        wvb[0:wlen, :] = svw_ref[pl.ds(g, wlen, stride=N_KV), :].astype(BF16)
        wkb[wlen:wl, :] = jnp.concatenate([kwn_ref[:, gs], zpad], axis=0).astype(BF16)
        wvb[wlen:wl, :] = jnp.concatenate([vwn_ref[:, gs], zpad], axis=0).astype(BF16)
        s = _dot_nt(qst, wkb[...]) * ATTN_SCALE + _near_far_bias(rb_ref, g, dist_wn, tdec, wl, near)
        o_w = jnp.dot(_masked_softmax(s, mask_w).astype(BF16), wvb[...], preferred_element_type=F32)

        for r in range(GQA):
            h = g * GQA + r
            rs = slice(r * tdec, (r + 1) * tdec)
            yb_ref[:, h * HEAD_DIM:(h + 1) * HEAD_DIM] = (nsa[:, 3 * h:3 * h + 1] * o_c[rs]
                                                         + nsa[:, 3 * h + 1:3 * h + 2] * o_s[rs]
                                                         + nsa[:, 3 * h + 2:3 * h + 3] * o_w[rs])


def _sample_attention(proj, rel_bias, page_table, kcmp_g, vcmp_g, ck_sel, cv_sel, skw, svw, emat, bsz, tdec):
    npg = page_table.shape[1]
    page = ck_sel.shape[1] // N_KV
    wlen = skw.shape[1] // N_KV
    nb = kcmp_g.shape[1]
    past = npg * page
    lk = past + LANE
    assert tdec == SUBLANE and FAR_DIST <= LANE and wlen == WINDOW

    def tile(k):
        return pl.BlockSpec((None, tdec, PROJ_TILE), lambda b, pt, k=k: (k, b, 0))

    def pagespec(p_):
        return pl.BlockSpec((None, page * N_KV, HEAD_DIM), lambda b, pt, p_=p_: (pt[b, p_], 0, 0))

    in_specs = ([pl.BlockSpec(memory_space=pltpu.SMEM)]
                + [tile(T_Q + g) for g in range(N_KV)]
                + [tile(T_KS), tile(T_VS), tile(T_KW), tile(T_VW), tile(T_NSA)]
                + [pl.BlockSpec((None, nb, SUBLANE, HEAD_DIM), lambda b, pt: (b, 0, 0, 0))] * 2
                + [pagespec(p_) for p_ in range(npg)] * 2
                + [pl.BlockSpec((None, wlen * N_KV, HEAD_DIM), lambda b, pt: (b, 0, 0))] * 2
                + [pl.BlockSpec((LANE, lk), lambda b, pt: (0, 0))])
    grid_spec = pltpu.PrefetchScalarGridSpec(
        num_scalar_prefetch=1,
        grid=(bsz,),
        in_specs=in_specs,
        out_specs=[pl.BlockSpec((tdec, N_HEADS * HEAD_DIM), lambda b, pt: (b, 0)),
                   pl.BlockSpec((None, wlen * N_KV, HEAD_DIM), lambda b, pt: (b, 0, 0)),
                   pl.BlockSpec((None, wlen * N_KV, HEAD_DIM), lambda b, pt: (b, 0, 0))],
        scratch_shapes=[pltpu.VMEM((lk, HEAD_DIM), BF16), pltpu.VMEM((lk, HEAD_DIM), BF16),
                        pltpu.VMEM((wlen + LANE, HEAD_DIM), BF16), pltpu.VMEM((wlen + LANE, HEAD_DIM), BF16)])
    return pl.pallas_call(
        functools.partial(_sattn_kernel, npg=npg, page=page, tdec=tdec, wlen=wlen),
        grid_spec=grid_spec,
        out_shape=[jax.ShapeDtypeStruct((bsz * tdec, N_HEADS * HEAD_DIM), F32),
                   jax.ShapeDtypeStruct((bsz, wlen * N_KV, HEAD_DIM), F32),
                   jax.ShapeDtypeStruct((bsz, wlen * N_KV, HEAD_DIM), F32)],
        compiler_params=_cparams(("arbitrary",)),
        name="sample_attention",
    )(page_table, rel_bias, *([proj] * 9), kcmp_g, vcmp_g, *([ck_sel] * npg), *([cv_sel] * npg), skw, svw, emat)


def _merge_kernel(ya_ref, yb_ref, wa_ref, wb_ref, ga_ref, gb_ref, t_ref, yb_scr):
    @pl.when(pl.program_id(1) == 0)
    def _():
        yb_scr[...] = yb_ref[...].astype(BF16)

    a = jnp.dot(ya_ref[...], wa_ref[...], preferred_element_type=F32)
    b = jnp.dot(yb_scr[...], wb_ref[...], preferred_element_type=F32)
    t_ref[...] = (ga_ref[...] * a + gb_ref[...] * b).astype(BF16)


def _merge(proj, ya, yb, wa_b, wb_b, tm=1024):
    n, aw = ya.shape
    d = wb_b.shape[0]
    tm = min(tm, n)
    nj = d // PROJ_TILE
    return pl.pallas_call(
        _merge_kernel,
        grid=(n // tm, nj),
        in_specs=[pl.BlockSpec((tm, aw), lambda i, j: (i, 0)),
                  pl.BlockSpec((tm, d), lambda i, j: (i, 0)),
                  pl.BlockSpec((aw, PROJ_TILE), lambda i, j: (0, j)),
                  pl.BlockSpec((d, PROJ_TILE), lambda i, j: (0, j)),
                  pl.BlockSpec((None, tm, PROJ_TILE), lambda i, j: (T_GA + j, i, 0)),
                  pl.BlockSpec((None, tm, PROJ_TILE), lambda i, j: (T_GB + j, i, 0))],
        out_specs=pl.BlockSpec((tm, PROJ_TILE), lambda i, j: (i, j)),
        out_shape=jax.ShapeDtypeStruct((n, d), BF16),
        scratch_shapes=[pltpu.VMEM((tm, d), BF16)],
        compiler_params=_cparams(("parallel", "arbitrary")),
        name="merge",
    )(ya, yb, wa_b, wb_b, proj, proj)


def _outproj_kernel(t_ref, x_ref, gt_ref, sc_ref, sh_ref, gn_ref, wo_ref, *rest):
    x1_ref, h2_ref = rest[-2:]
    y = jnp.dot(t_ref[...], wo_ref[...], preferred_element_type=F32)
    x1 = x_ref[...] + gt_ref[...] * y.reshape(x_ref.shape)
    x1_ref[...] = x1
    r = lax.rsqrt(jnp.mean(x1 * x1, axis=-1, keepdims=True) + EPS)
    h2 = (x1 * r) * gn_ref[...] * (1.0 + sc_ref[...]) + sh_ref[...]
    h2_ref[...] = h2.reshape(h2_ref.shape).astype(BF16)


def _out_projection(tmix, x3, mod4, g_n2, wo_b, bt, tt, n_total, row0, h2_buf=None):
    nb, tb, d = x3.shape
    tpb = tb // tt
    tm = bt * tt
    n = nb * tb
    assert row0 % tm == 0
    off = row0 // tm
    extra_specs = [] if h2_buf is None else [pl.BlockSpec(memory_space=pl.ANY)]
    extra_args = [] if h2_buf is None else [h2_buf]
    aliases = {} if h2_buf is None else {7: 1}

    def modspec(k):
        return pl.BlockSpec((bt, None, 1, d), lambda i, k=k: (i // tpb, k, 0, 0))

    return pl.pallas_call(
        _outproj_kernel,
        grid=(n // tm,),
        in_specs=[pl.BlockSpec((tm, d), lambda i: (i, 0)),
                  pl.BlockSpec((bt, tt, d), lambda i: (i // tpb, i % tpb, 0)),
                  modspec(2), modspec(4), modspec(3),
                  pl.BlockSpec((1, 1, d), lambda i: (0, 0, 0)),
                  pl.BlockSpec((d, d), lambda i: (0, 0))] + extra_specs,
        out_specs=[pl.BlockSpec((bt, tt, d), lambda i: (i // tpb, i % tpb, 0)),
                   pl.BlockSpec((tm, d), lambda i: (off + i, 0))],
        out_shape=[jax.ShapeDtypeStruct((nb, tb, d), F32), jax.ShapeDtypeStruct((n_total, d), BF16)],
        input_output_aliases=aliases,
        compiler_params=_cparams(("arbitrary",)),
        name="out_projection",
    )(tmix, x3, mod4, mod4, mod4, g_n2.reshape(1, 1, d), wo_b, *extra_args)


def _peer_scores_kernel(h_ref, wpq_ref, sk1_ref, sk2_ref, s1_ref, s2_ref):
    pq = jnp.dot(h_ref[...], wpq_ref[...], preferred_element_type=F32)
    kd = sk1_ref.shape[1]
    for hd in range(PEER_HEADS):
        q1 = pq[:, hd * 2 * kd:hd * 2 * kd + kd].astype(BF16)
        q2 = pq[:, hd * 2 * kd + kd:(hd + 1) * 2 * kd].astype(BF16)
        s1_ref[hd] = _dot_nt(sk1_ref[...], q1)
        s2_ref[hd] = _dot_nt(sk2_ref[...], q2)


def _peer_scores(h2, wpq_b, sk1_b, sk2_b, tm=512):
    n, d = h2.shape
    dq = wpq_b.shape[1]
    nk, kd = sk1_b.shape
    return pl.pallas_call(
        _peer_scores_kernel,
        grid=(n // tm,),
        in_specs=[pl.BlockSpec((tm, d), lambda i: (i, 0)),
                  pl.BlockSpec((d, dq), lambda i: (0, 0)),
                  pl.BlockSpec((nk, kd), lambda i: (0, 0)),
                  pl.BlockSpec((nk, kd), lambda i: (0, 0))],
        out_specs=[pl.BlockSpec((PEER_HEADS, nk, tm), lambda i: (0, 0, i))] * 2,
        out_shape=[jax.ShapeDtypeStruct((PEER_HEADS, nk, n), F32)] * 2,
        compiler_params=_cparams(("arbitrary",)),
        name="peer_scores",
    )(h2, wpq_b, sk1_b, sk2_b)


def _staircase():
    return [(a, b) for a in range(PEER_TOPK) for b in range(PEER_TOPK) if (a + 1) * (b + 1) <= PEER_TOPK]


def _extract_top(s, rows_f, exact):
    vals = []
    rank = jnp.full(s.shape, float(PEER_TOPK), F32)
    for a in range(PEER_TOPK):
        m = jnp.max(s, axis=0, keepdims=True)
        hit = s == m
        if exact:
            hit = rows_f == jnp.min(jnp.where(hit, rows_f, 1e9), axis=0, keepdims=True)
        rank = jnp.where(hit, float(a), rank)
        s = jnp.where(hit, -jnp.inf, s)
        vals.append(m)
    return vals, rank


def _peer_topk_kernel(s1_ref, s2_ref, cnt_ref, e1_ref, rk_ref, e2_ref):
    nk, tn = s1_ref.shape[1], s1_ref.shape[2]
    rows_f = lax.broadcasted_iota(jnp.int32, (nk, tn), 0).astype(F32)
    pairs = _staircase()
    npad = -(-len(pairs) // SUBLANE) * SUBLANE
    prow = lax.broadcasted_iota(jnp.int32, (npad, tn), 0)
    flat_f = jnp.full((npad, tn), 1e9, F32)
    arow_f = jnp.full((npad, tn), -1.0, F32)
    for i, (a, b) in enumerate(pairs):
        flat_f = jnp.where(prow == i, float(a * PEER_TOPK + b), flat_f)
        arow_f = jnp.where(prow == i, float(a), arow_f)
    k_f = float(PEER_TOPK)

    def one_head(hd, exact):
        s1 = s1_ref[hd]
        s2 = s2_ref[hd]
        v1, rank1 = _extract_top(s1, rows_f, exact)
        v2, rank2 = _extract_top(s2, rows_f, exact)
        cand = jnp.full((npad, tn), -jnp.inf, F32)
        for i, (a, b) in enumerate(pairs):
            cand = jnp.where(prow == i, v1[a] + v2[b], cand)
        m0 = v1[0] + v2[0]
        c = cand
        selected = jnp.zeros((npad, tn), F32)
        for _ in range(PEER_TOPK):
            m = jnp.max(c, axis=0, keepdims=True)
            hit = c == m
            if exact:
                hit = flat_f == jnp.min(jnp.where(hit, flat_f, 2e9), axis=0, keepdims=True)
            selected = jnp.where(hit, 1.0, selected)
            c = jnp.where(hit, -jnp.inf, c)
        z = jnp.sum(jnp.where(selected > 0.5, jnp.exp(cand - m0), 0.0), axis=0, keepdims=True)
        cnt1 = jnp.zeros((nk, tn), F32)
        for a in range(PEER_TOPK):
            cnt_a = jnp.sum(jnp.where(arow_f == float(a), selected, 0.0), axis=0, keepdims=True)
            cnt1 = jnp.where(rank1 == float(a), cnt_a, cnt1)
        cnt_ref[hd] = cnt1
        e1_ref[hd] = jnp.exp(s1 - v1[0]) * (1.0 / z)
        rk_ref[hd] = rank2
        e2_ref[hd] = jnp.exp(s2 - v2[0])
        if exact:
            return None
        n1 = jnp.sum(jnp.where(rank1 < k_f, 1.0, 0.0), axis=0, keepdims=True)
        n2 = jnp.sum(jnp.where(rank2 < k_f, 1.0, 0.0), axis=0, keepdims=True)
        n3 = jnp.sum(selected, axis=0, keepdims=True)
        return jnp.where((n1 != k_f) | (n2 != k_f) | (n3 != k_f), 1.0, 0.0)

    def body(hd, carry):
        tie = one_head(hd, False)

        @pl.when(jnp.max(tie) > 0.0)
        def _():
            one_head(hd, True)

        return carry

    lax.fori_loop(0, PEER_HEADS, body, 0)


def _peer_topk(s1t, s2t, tn=256):
    nh, nk, n = s1t.shape
    spec = pl.BlockSpec((nh, nk, tn), lambda i: (0, 0, i))
    return pl.pallas_call(
        _peer_topk_kernel,
        grid=(n // tn,),
        in_specs=[spec, spec],
        out_specs=[spec] * 4,
        out_shape=[jax.ShapeDtypeStruct((nh, nk, n), F32)] * 4,
        compiler_params=_cparams(("arbitrary",)),
        name="peer_topk",
    )(s1t, s2t)


PEER_SUB = 256


def _peer_dense_kernel(h_ref, eu_ref, ev_ref, cnt_ref, e1_ref, rk_ref, e2_ref, o_ref, at_scr, wa_scr, *, te):
    e = pl.program_id(1)
    nk = rk_ref.shape[1]
    tm, d = h_ref.shape

    @pl.when(e == 0)
    def _():
        o_ref[...] = jnp.zeros(o_ref.shape, F32)

    n_i1 = PEER_SUB // nk
    nsub = te // PEER_SUB
    assert te // nk == SUBLANE
    i1_base = pl.multiple_of(e * SUBLANE, SUBLANE)
    tok_piece = 2 * LANE
    n_tok = tm // tok_piece
    col_piece = 2 * LANE
    n_col = d // col_piece
    tiles = [(il, tb) for il in range(n_i1) for tb in range(tm // LANE)]

    def pre_activation(sb, k):
        ts_ = slice(k * tok_piece, (k + 1) * tok_piece)
        at_scr[sb, :, ts_] = _dot_nt(eu_ref[sb * PEER_SUB:(sb + 1) * PEER_SUB, :], h_ref[ts_, :])

    def down_projection(sb, k):
        cs_ = slice(k * col_piece, (k + 1) * col_piece)
        o_ref[:, cs_] += _dot_tn(wa_scr[sb], ev_ref[sb * PEER_SUB:(sb + 1) * PEER_SUB, cs_])

    def gate_tile(sb, il, tb):
        j1 = sb * n_i1 + il
        ks = slice(il * nk, (il + 1) * nk)
        cs = slice(tb * LANE, (tb + 1) * LANE)
        w = jnp.zeros((nk, LANE), F32)
        for hd in range(PEER_HEADS):
            c = cnt_ref[hd, pl.ds(i1_base, SUBLANE), cs][j1:j1 + 1, :]
            g1 = e1_ref[hd, pl.ds(i1_base, SUBLANE), cs][j1:j1 + 1, :]
            w = w + jnp.where(rk_ref[hd, :, cs] < c, e2_ref[hd, :, cs] * g1, 0.0)
        wa_scr[sb, ks, cs] = (w * _gelu(at_scr[sb, ks, cs])).astype(BF16)

    for k in range(n_tok):
        pre_activation(0, k)
    nslot = max(len(tiles), n_col)
    for sb in range(nsub):
        for k in range(nslot):
            if sb >= 1 and k < n_col:
                down_projection(sb - 1, k)
            if sb + 1 < nsub and k % (nslot // n_tok) == 0:
                pre_activation(sb + 1, k // (nslot // n_tok))
            if k < len(tiles):
                gate_tile(sb, *tiles[k])
    for k in range(n_col):
        down_projection(nsub - 1, k)


def _peer_dense(h2, eu_b, ev_b, cnt1, e1, rk2, e2, tm=512, te=1024):
    n, d = h2.shape
    ne = eu_b.shape[0]
    nh, nk, _ = cnt1.shape
    res = pl.BlockSpec((nh, nk, tm), lambda i, e: (0, 0, i))
    return pl.pallas_call(
        functools.partial(_peer_dense_kernel, te=te),
        grid=(n // tm, ne // te),
        in_specs=[pl.BlockSpec((tm, d), lambda i, e: (i, 0)),
                  pl.BlockSpec((te, d), lambda i, e: (e, 0)),
                  pl.BlockSpec((te, d), lambda i, e: (e, 0)),
                  res, res, res, res],
        out_specs=pl.BlockSpec((tm, d), lambda i, e: (i, 0)),
        out_shape=jax.ShapeDtypeStruct((n, d), F32),
        scratch_shapes=[pltpu.VMEM((te // PEER_SUB, PEER_SUB, tm), F32), pltpu.VMEM((te // PEER_SUB, PEER_SUB, tm), BF16)],
        compiler_params=_cparams(("parallel", "arbitrary")),
        name="peer_dense",
    )(h2, eu_b, ev_b, cnt1, e1, rk2, e2)


def _final_kernel(x1_ref, p_ref, gt_ref, o_ref):
    o_ref[...] = x1_ref[...] + gt_ref[...] * p_ref[...].reshape(x1_ref.shape)


def _final_residual(x1, peer, row0, mod4, bt, tt):
    nb, tb, d = x1.shape
    tpb = tb // tt
    tm = bt * tt
    n = nb * tb
    off = row0 // tm
    return pl.pallas_call(
        _final_kernel,
        grid=(n // tm,),
        in_specs=[pl.BlockSpec((bt, tt, d), lambda i: (i // tpb, i % tpb, 0)),
                  pl.BlockSpec((tm, d), lambda i: (off + i, 0)),
                  pl.BlockSpec((bt, None, 1, d), lambda i: (i // tpb, 5, 0, 0))],
        out_specs=pl.BlockSpec((bt, tt, d), lambda i: (i // tpb, i % tpb, 0)),
        out_shape=jax.ShapeDtypeStruct((nb, tb, d), F32),
        compiler_params=_cparams(("arbitrary",)),
        name="final_residual",
    )(x1, peer, mod4)


def _block_expand(n_cols, width=LANE):
    j = np.arange(width)[:, None]
    s = np.arange(n_cols)[None, :]
    return (s // SEL_BLOCK == j).astype(np.float32)


def _forward(x_prompt, x_sample, cache_k_cmp, cache_v_cmp, cache_k_sel, cache_v_sel, state_k_win, state_v_win,
             page_table, c_prompt, c_sample, rel_bias, w_ada, b_ada, g_n1, g_n2, w_in, ln_v_g, ln_v_b, w_s, b_s,
             g_q, g_k, pe_k, w_c1k, w_c2k, pe_v, w_c1v, w_c2v, w_a, w_b, w_o, w_pq, sk1, sk2, expert_u, expert_v):
    assert w_ada.shape[0] == 1, "single layer"
    bp, tp, d = x_prompt.shape
    bs, ts, _ = x_sample.shape
    np_, ns_ = bp * tp, bs * ts
    (w_ada, b_ada, g_n1, g_n2, w_in, ln_v_g, ln_v_b, w_s, b_s, g_q, g_k, pe_k, w_c1k, w_c2k, pe_v, w_c1v, w_c2v,
     w_a, w_b, w_o, w_pq, sk1, sk2, expert_u, expert_v) = [a[0] for a in (
         w_ada, b_ada, g_n1, g_n2, w_in, ln_v_g, ln_v_b, w_s, b_s, g_q, g_k, pe_k, w_c1k, w_c2k, pe_v, w_c1v, w_c2v,
         w_a, w_b, w_o, w_pq, sk1, sk2, expert_u, expert_v)]

    n_gate = 3 * N_HEADS
    c0 = T_NSA * PROJ_TILE
    w_in_p = jnp.concatenate([w_in[:, :c0],
                              jnp.pad(w_in[:, c0:c0 + n_gate], ((0, 0), (0, PROJ_TILE - n_gate))),
                              w_in[:, c0 + n_gate:]], axis=1).astype(BF16)
    ones = jnp.ones((PROJ_TILE,), F32)
    zeros = jnp.zeros((PROJ_TILE,), F32)
    rep = PROJ_TILE // HEAD_DIM
    gains = [ones] * N_TILES
    flags = [zeros] * N_TILES
    for k in range(T_Q, T_KC):
        gains[k], flags[k] = jnp.tile(g_q, rep), ones
    gains[T_KS], flags[T_KS] = jnp.tile(g_k[1], rep), ones
    gains[T_KW], flags[T_KW] = jnp.tile(g_k[2], rep), ones
    gain = jnp.stack(gains)[:, None, :]
    flag = jnp.stack(flags)[:, None, :]
    tril = jnp.tril(w_s)
    wm_p = tril.astype(BF16)
    bsb_p = jnp.broadcast_to(b_s[:, :, None], (A_GROUPS, CHUNK, CHUNK))
    nrep = CHUNK // ts
    wm_s = jnp.einsum("ab,gij->gaibj", jnp.eye(nrep, dtype=F32), tril[:, :ts, :ts]).reshape(A_GROUPS, CHUNK, CHUNK).astype(BF16)
    bsb_s = jnp.broadcast_to(jnp.tile(b_s[:, :ts], (1, nrep))[:, :, None], (A_GROUPS, CHUNK, CHUNK))
    w1k_b, w2k_b, w1v_b, w2v_b = [a.astype(BF16) for a in (w_c1k, w_c2k, w_c1v, w_c2v)]
    wa_b, wb_b, wo_b, wpq_b = [a.astype(BF16) for a in (w_a, w_b, w_o, w_pq)]
    sk1_b, sk2_b = sk1.astype(BF16), sk2.astype(BF16)
    eu_b, ev_b = expert_u.astype(BF16), expert_v.astype(BF16)
    one_gain = jnp.ones((HEAD_DIM,), F32)

    def pool_weights(w1, pe):
        hid = w1.shape[1]
        w1x = w1.reshape(CMP_BLOCK // 2, 2, HEAD_DIM, hid).transpose(0, 2, 1, 3).reshape(CMP_BLOCK // 2 * HEAD_DIM, 2 * hid)
        pe8 = jnp.repeat(pe.reshape(CMP_BLOCK // 2, 2, 1, HEAD_DIM), N_KV, axis=2).reshape(CMP_BLOCK // 2, SUBLANE, HEAD_DIM)
        return w1x.astype(BF16), pe8

    w1k_x, pe8_k = pool_weights(w_c1k, pe_k)
    w1v_x, pe8_v = pool_weights(w_c1v, pe_v)

    nc = bp + bs
    ncp = -(-nc // SUBLANE) * SUBLANE
    c_all = jnp.pad(jnp.concatenate([c_prompt, c_sample], axis=0), ((0, ncp - nc), (0, 0)))
    mod = _modulation(c_all, w_ada, b_ada)
    mod_p = mod[:bp].reshape(bp, 6, 1, d)
    mod_s = mod[bp:nc].reshape(bs, 6, 1, d)

    tm_p = min(1024, tp)
    bt_s = min(1024 // ts, bs)

    proj_p = _in_projection(x_prompt, mod_p, g_n1, w_in_p, gain, flag, 1, tm_p)
    proj_s = _in_projection(x_sample, mod_s, g_n1, w_in_p, gain, flag, bt_s, ts)

    cpb = tp // CHUNK
    ya_p, vch_p = _mixer_a(proj_p, ln_v_g, ln_v_b, wm_p, bsb_p, bp, lambda i: i // cpb)
    ya_s, vch_s = _mixer_a(proj_s, ln_v_g, ln_v_b, wm_s, bsb_s, ns_ // CHUNK, lambda i: i)

    kcmp_p = _compress_prompt(proj_p, T_KC, bp, tp, pe_k, w1k_b, w2k_b, g_k[0], True)
    vcmp_p = _compress_prompt(proj_p, T_VC, bp, tp, pe_v, w1v_b, w2v_b, one_gain, False)
    n_phys, page = cache_k_cmp.shape[1], cache_k_cmp.shape[2]
    bpp = page // CMP_BLOCK
    blk_rows = CMP_BLOCK * N_KV
    kcmp_pool = _compress_pool(cache_k_cmp.reshape(n_phys * bpp, blk_rows, HEAD_DIM), pe8_k, w1k_x, w2k_b, g_k[0], True)
    vcmp_pool = _compress_pool(cache_v_cmp.reshape(n_phys * bpp, blk_rows, HEAD_DIM), pe8_v, w1v_x, w2v_b, one_gain, False)
    npg = page_table.shape[1]
    kcmp_s = kcmp_pool.reshape(n_phys, bpp * SUBLANE * HEAD_DIM)[page_table].reshape(bs, npg * bpp, SUBLANE, HEAD_DIM)
    vcmp_s = vcmp_pool.reshape(n_phys, bpp * SUBLANE * HEAD_DIM)[page_table].reshape(bs, npg * bpp, SUBLANE, HEAD_DIM)

    tables, ctab = _bias_tables(rel_bias, ATT_TILE)
    oc_p, sel_p = _cmp_select(proj_p, ctab, kcmp_p, vcmp_p, bp, tp)
    yb_p = _prompt_attention(proj_p, rel_bias, tables, oc_p, bp, tp, "sel", sel_p)
    yb_p = _prompt_attention(proj_p, rel_bias, tables, yb_p, bp, tp, "win")

    past = npg * page
    emat_s = jnp.asarray(_block_expand(past + LANE), BF16)
    yb_s, kwin_s, vwin_s = _sample_attention(
        proj_s, rel_bias, page_table, kcmp_s, vcmp_s,
        cache_k_sel.reshape(n_phys, page * N_KV, HEAD_DIM), cache_v_sel.reshape(n_phys, page * N_KV, HEAD_DIM),
        state_k_win.reshape(bs, -1, HEAD_DIM), state_v_win.reshape(bs, -1, HEAD_DIM), emat_s, bs, ts)

    t_p = _merge(proj_p, ya_p, yb_p, wa_b, wb_b)
    t_s = _merge(proj_s, ya_s, yb_s, wa_b, wb_b)
    x1_p, h2 = _out_projection(t_p, x_prompt, mod_p, g_n2, wo_b, 1, min(256, tp), np_ + ns_, 0)
    x1_s, h2 = _out_projection(t_s, x_sample, mod_s, g_n2, wo_b, min(256 // ts, bs), ts, np_ + ns_, np_, h2)

    s1t, s2t = _peer_scores(h2, wpq_b, sk1_b, sk2_b)
    cnt1, e1, rk2, e2 = _peer_topk(s1t, s2t)
    peer = _peer_dense(h2, eu_b, ev_b, cnt1, e1, rk2, e2)
    y_p = _final_residual(x1_p, peer, 0, mod_p, 1, min(512, tp))
    y_s = _final_residual(x1_s, peer, np_, mod_s, min(512 // ts, bs), ts)

    def kv_p(k):
        return proj_p[k].reshape(1, bp, tp, N_KV, HEAD_DIM)

    def kv_s(k):
        return proj_s[k].reshape(1, bs, ts, N_KV, HEAD_DIM)

    wb_p = min(WINDOW, tp)
    wlen = state_k_win.shape[2]
    return (y_p, y_s,
            kv_p(T_KC), kv_p(T_VC), kv_p(T_KS), kv_p(T_VS),
            kv_p(T_KW)[:, :, tp - wb_p:], kv_p(T_VW)[:, :, tp - wb_p:],
            vch_p.reshape(1, bp, CHUNK, -1),
            kv_s(T_KC), kv_s(T_VC), kv_s(T_KS), kv_s(T_VS),
            kwin_s.reshape(1, bs, wlen, N_KV, HEAD_DIM), vwin_s.reshape(1, bs, wlen, N_KV, HEAD_DIM),
            vch_s.reshape(1, bs, ts, -1))


def kernel(x_prompt, x_sample, cache_k_cmp, cache_v_cmp, cache_k_sel, cache_v_sel, state_k_win, state_v_win, page_table, c_prompt, c_sample, rel_bias, w_ada, b_ada, g_n1, g_n2, w_in, ln_v_g, ln_v_b, w_s, b_s, g_q, g_k, pe_k, w_c1k, w_c2k, pe_v, w_c1v, w_c2v, w_a, w_b, w_o, w_pq, sk1, sk2, expert_u, expert_v):
    return _forward(x_prompt, x_sample, cache_k_cmp, cache_v_cmp, cache_k_sel, cache_v_sel, state_k_win, state_v_win,
                    page_table, c_prompt, c_sample, rel_bias, w_ada, b_ada, g_n1, g_n2, w_in, ln_v_g, ln_v_b, w_s, b_s,
                    g_q, g_k, pe_k, w_c1k, w_c2k, pe_v, w_c1v, w_c2v, w_a, w_b, w_o, w_pq, sk1, sk2, expert_u, expert_v)
```
